```python
import math
import jax, jax.numpy as jnp
from jax import lax
import numpy as np

D_MODEL = 2048
BATCH = 32
SEQ = 256
DEPTH = 4
DEC_BATCH = 4
DEC_SEQ = 4096
PAST_LEN = 512

GRID_W = 64
BLOCK = 128
HEAD_DIM = 64
N_FREQ = HEAD_DIM // 4
ROPE_BASE = 10000.0
ATTN_SCALE = HEAD_DIM ** -0.5
RMS_EPS = 1e-6
N_MOD = 6
N_BRANCH = 4
BRANCH_W = D_MODEL // 4
LRU_W = BRANCH_W
LRU_BLOCKS = 8
LRU_BW = LRU_W // LRU_BLOCKS
LRU_C = 8.0
CONV_W = 4
CONV_PAD = (1, 2)
WIN_HEADS = BRANCH_W // HEAD_DIM
WIN_KV = 2
WIN_G = WIN_HEADS // WIN_KV
WINDOW = 128
GRID_HEADS = BRANCH_W // HEAD_DIM
GRID_KV = 2
GRID_G = GRID_HEADS // GRID_KV
DIFF_HEADS = BRANCH_W // (2 * HEAD_DIM)
D_FF = 4 * D_MODEL
IN_SIZES = (LRU_W, LRU_W,
            WIN_HEADS * HEAD_DIM, WIN_KV * HEAD_DIM, WIN_KV * HEAD_DIM,
            GRID_HEADS * HEAD_DIM, GRID_KV * HEAD_DIM, GRID_KV * HEAD_DIM,
            DIFF_HEADS * 2 * HEAD_DIM, DIFF_HEADS * 2 * HEAD_DIM, DIFF_HEADS * 2 * HEAD_DIM)
D_IN = sum(IN_SIZES)

kernel_name = 'hybrid_flow_prefix_trunk_step'


def rms_norm(x, g, eps=RMS_EPS):
    xf = x.astype(jnp.float32)
    y = xf * lax.rsqrt(jnp.mean(xf * xf, axis=-1, keepdims=True) + eps)
    return y.astype(x.dtype) * g


def modulation(cond, w_ada, b_ada):
    m = jax.nn.silu(cond) @ w_ada + b_ada
    return m.reshape(m.shape[:-1] + (N_MOD, D_MODEL))


def grid_rope(n_tokens):
    n_rows = n_tokens // GRID_W
    row = jnp.repeat(jnp.arange(n_rows, dtype=jnp.float32), GRID_W)
    col = jnp.tile(jnp.arange(GRID_W, dtype=jnp.float32), n_rows)
    inv = ROPE_BASE ** (-jnp.arange(N_FREQ, dtype=jnp.float32) / N_FREQ)
    ang = jnp.stack([row[:, None] * inv, col[:, None] * inv], axis=1)
    return jnp.cos(ang), jnp.sin(ang)


def apply_rope(x, cos, sin):
    shp = x.shape
    xr = x.reshape(shp[0], shp[1], -1, 2, 2, N_FREQ)
    x1, x2 = xr[..., 0, :], xr[..., 1, :]
    c = cos[None, :, None].astype(x.dtype)
    s = sin[None, :, None].astype(x.dtype)
    out = jnp.stack([x1 * c - x2 * s, x2 * c + x1 * s], axis=-2)
    return out.reshape(shp)


def sweep_query_blocks(fn, q):
    b, t = q.shape[0], q.shape[1]
    nb = t // BLOCK
    qb = jnp.moveaxis(q.reshape((b, nb, BLOCK) + q.shape[2:]), 1, 0)
    out = lax.map(fn, qb)
    return jnp.moveaxis(out, 0, 1).reshape((b, t) + out.shape[3:])


def gqa_dense(qb, k, v, sink):
    s = jnp.einsum('bqhgd,bkhd->bhgqk', qb, k, preferred_element_type=jnp.float32) * ATTN_SCALE
    n_keys = k.shape[1]
    if sink is not None:
        sink_col = jnp.broadcast_to(sink.astype(jnp.float32)[None, :, :, None, None], s.shape[:-1] + (1,))
        s = jnp.concatenate([s, sink_col], axis=-1)
    p = jax.nn.softmax(s, axis=-1)[..., :n_keys]
    return jnp.einsum('bhgqk,bkhd->bqhgd', p.astype(v.dtype), v)


def banded_window_attention(q, k, v, kc, vc, sink):
    b, t = q.shape[0], q.shape[1]
    nb = t // BLOCK
    n_ctx = kc.shape[1]
    qb = q.reshape(b, nb, BLOCK, WIN_KV, WIN_G, HEAD_DIM)
    pad = ((0, 0), (BLOCK, BLOCK), (0, 0), (0, 0))
    kp = jnp.pad(k, pad).reshape(b, nb + 2, BLOCK, WIN_KV, HEAD_DIM)
    vp = jnp.pad(v, pad).reshape(b, nb + 2, BLOCK, WIN_KV, HEAD_DIM)
    kw = jnp.concatenate([kp[:, :-2], kp[:, 1:-1], kp[:, 2:]], axis=2)
    vw = jnp.concatenate([vp[:, :-2], vp[:, 1:-1], vp[:, 2:]], axis=2)
    qpos = jnp.arange(t).reshape(nb, BLOCK)
    kpos = jnp.arange(nb)[:, None] * BLOCK - BLOCK + jnp.arange(3 * BLOCK)[None, :]
    mask = ((jnp.abs(qpos[:, :, None] - kpos[:, None, :]) <= WINDOW)
            & (kpos[:, None, :] >= 0) & (kpos[:, None, :] < t))
    s_loc = jnp.einsum('bnqhgd,bnkhd->bnhgqk', qb, kw, preferred_element_type=jnp.float32) * ATTN_SCALE
    s_loc = jnp.where(mask[None, :, None, None], s_loc, -jnp.inf)
    s_ctx = jnp.einsum('bnqhgd,bshd->bnhgqs', qb, kc, preferred_element_type=jnp.float32) * ATTN_SCALE
    sink_col = jnp.broadcast_to(sink.astype(jnp.float32)[None, None, :, :, None, None], s_loc.shape[:-1] + (1,))
    p = jax.nn.softmax(jnp.concatenate([s_loc, s_ctx, sink_col], axis=-1), axis=-1)
    p_loc = p[..., :3 * BLOCK].astype(v.dtype)
    p_ctx = p[..., 3 * BLOCK:3 * BLOCK + n_ctx].astype(v.dtype)
    out = (jnp.einsum('bnhgqk,bnkhd->bnqhgd', p_loc, vw)
           + jnp.einsum('bnhgqs,bshd->bnqhgd', p_ctx, vc))
    return out.reshape(b, t, WIN_KV * WIN_G * HEAD_DIM)


def diff_dense(qb, k, v, lam):
    s = jnp.einsum('bqhjd,bkhjd->bhjqk', qb, k, preferred_element_type=jnp.float32) * ATTN_SCALE
    p = jax.nn.softmax(s, axis=-1)
    w = p[:, :, 0] - lam * p[:, :, 1]
    return jnp.einsum('bhqk,bkhe->bqhe', w.astype(v.dtype), v)


def lru_combine(left, right):
    a_l, b_l = left
    a_r, b_r = right
    return a_l * a_r, a_r * b_l + b_r


def linear_scan(a, b, h0, reverse):
    if h0 is not None:
        idx = -1 if reverse else 0
        b = b.at[:, idx].add(a[:, idx] * h0)
    _, h = lax.associative_scan(lru_combine, (a, b), reverse=reverse, axis=1)
    return h


def lru_branch(xa, ya, conv_w, conv_b, wr, br, wi, bi, lam, h0):
    b, t, _ = xa.shape
    u = lax.conv_general_dilated(xa, conv_w[:, None, :], window_strides=(1,), padding=[CONV_PAD],
                                 dimension_numbers=('NWC', 'WIO', 'NWC'),
                                 feature_group_count=LRU_W) + conv_b
    ub = u.reshape(b, t, LRU_BLOCKS, LRU_BW)
    r = jax.nn.sigmoid(jnp.einsum('btnc,kncd->btknd', ub, wr, preferred_element_type=jnp.float32)
                       .reshape(b, t, 2, LRU_W) + br)
    ig = jax.nn.sigmoid(jnp.einsum('btnc,kncd->btknd', ub, wi, preferred_element_type=jnp.float32)
                        .reshape(b, t, 2, LRU_W) + bi)
    log_a = -LRU_C * r * jax.nn.softplus(-lam.astype(jnp.float32))
    a = jnp.exp(log_a)
    inp = jnp.sqrt(-jnp.expm1(2.0 * log_a)) * ig * u.astype(jnp.float32)[:, :, None, :]
    h0f = None if h0 is None else h0[:, 0].astype(jnp.float32)
    h0b = None if h0 is None else h0[:, 1].astype(jnp.float32)
    hf = linear_scan(a[:, :, 0], inp[:, :, 0], h0f, False)
    hb = linear_scan(a[:, :, 1], inp[:, :, 1], h0b, True)
    out = (hf + hb).astype(xa.dtype) * jax.nn.gelu(ya)
    final = None if h0 is not None else jnp.stack([hf[:, -1], hb[:, 0]], axis=1).astype(xa.dtype)
    return out, final


def window_branch(q, k, v, qn, kn, sink, rope, ctx_kv):
    b, t, _ = q.shape
    q = rms_norm(q.reshape(b, t, WIN_HEADS, HEAD_DIM), qn)
    k = rms_norm(k.reshape(b, t, WIN_KV, HEAD_DIM), kn)
    v = v.reshape(b, t, WIN_KV, HEAD_DIM)
    sink = sink.reshape(WIN_KV, WIN_G)
    if ctx_kv is None:
        o = sweep_query_blocks(lambda qb: gqa_dense(qb, k, v, sink),
                               q.reshape(b, t, WIN_KV, WIN_G, HEAD_DIM))
    else:
        cos, sin = rope
        qr = apply_rope(q, cos, sin).reshape(b, t, WIN_KV, WIN_G, HEAD_DIM)
        o = banded_window_attention(qr, apply_rope(k, cos, sin), v, ctx_kv[0], ctx_kv[1], sink)
    return o.reshape(b, t, BRANCH_W), k, v


def grid_branch(q, k, v, qn, kn, rope, ctx_kv):
    b, t, _ = q.shape
    q = rms_norm(q.reshape(b, t, GRID_HEADS, HEAD_DIM), qn)
    k = rms_norm(k.reshape(b, t, GRID_KV, HEAD_DIM), kn)
    v = v.reshape(b, t, GRID_KV, HEAD_DIM)
    if ctx_kv is None:
        keys, vals = k, v
    else:
        cos, sin = rope
        q = apply_rope(q, cos, sin)
        keys = jnp.concatenate([apply_rope(k, cos, sin), ctx_kv[0]], axis=1)
        vals = jnp.concatenate([v, ctx_kv[1]], axis=1)
    o = sweep_query_blocks(lambda qb: gqa_dense(qb, keys, vals, None),
                           q.reshape(b, t, GRID_KV, GRID_G, HEAD_DIM))
    return o.reshape(b, t, BRANCH_W), k, v


def diff_branch(q, k, v, qn, kn, lq1, lk1, lq2, lk2, out_g, layer, rope, ctx_kv):
    b, t, _ = q.shape
    q = rms_norm(q.reshape(b, t, DIFF_HEADS, 2, HEAD_DIM), qn)
    k = rms_norm(k.reshape(b, t, DIFF_HEADS, 2, HEAD_DIM), kn)
    v = v.reshape(b, t, DIFF_HEADS, 2 * HEAD_DIM)
    lam_init = 0.8 - 0.6 * math.exp(-0.3 * layer)
    lam = (jnp.exp(jnp.sum(lq1.astype(jnp.float32) * lk1.astype(jnp.float32)))
           - jnp.exp(jnp.sum(lq2.astype(jnp.float32) * lk2.astype(jnp.float32))) + lam_init)
    if ctx_kv is None:
        keys, vals = k, v
    else:
        cos, sin = rope
        q = apply_rope(q, cos, sin)
        keys = jnp.concatenate([apply_rope(k, cos, sin), ctx_kv[0]], axis=1)
        vals = jnp.concatenate([v, ctx_kv[1]], axis=1)
    o = sweep_query_blocks(lambda qb: diff_dense(qb, keys, vals, lam), q)
    o = rms_norm(o, out_g) * (1.0 - lam_init)
    return o.reshape(b, t, BRANCH_W), k, v


def trunk_layer(x, mod, lp, layer, rope, ctx):
    b, t, _ = x.shape
    shift1, scale1, gate1, shift2, scale2, gate2 = (mod[..., i, :] for i in range(N_MOD))
    h = rms_norm(x, lp['norm1_g']) * (1 + scale1) + shift1
    z = h @ lp['w_in']
    points = [int(p) for p in np.cumsum(IN_SIZES)[:-1]]
    xa, ya, qw, kw, vw, qg, kg, vg, qd, kd, vd = jnp.split(z, points, axis=-1)
    if ctx is None:
        win_ctx = grid_ctx = diff_ctx = lru_h0 = None
    else:
        win_ctx, grid_ctx, diff_ctx, lru_h0 = (ctx[0], ctx[1]), (ctx[2], ctx[3]), (ctx[4], ctx[5]), ctx[6]
    a_out, lru_final = lru_branch(xa, ya, lp['conv_w'], lp['conv_b'], lp['lru_wr'], lp['lru_br'],
                                  lp['lru_wi'], lp['lru_bi'], lp['lru_lambda'], lru_h0)
    w_out, k_w, v_w = window_branch(qw, kw, vw, lp['win_qn'], lp['win_kn'], lp['win_sink'], rope, win_ctx)
    g_out, k_g, v_g = grid_branch(qg, kg, vg, lp['grid_qn'], lp['grid_kn'], rope, grid_ctx)
    d_out, k_d, v_d = diff_branch(qd, kd, vd, lp['diff_qn'], lp['diff_kn'], lp['diff_lq1'], lp['diff_lk1'],
                                  lp['diff_lq2'], lp['diff_lk2'], lp['diff_out_g'], layer, rope, diff_ctx)
    branches = jnp.stack([a_out, w_out, g_out, d_out], axis=2)
    proj = jnp.einsum('btnc,ncd->btnd', branches, lp['w_branch'])
    gates = jax.nn.sigmoid(h @ lp['w_gate'] + lp['b_gate']).reshape(b, t, N_BRANCH, D_MODEL)
    x = x + gate1 * (jnp.sum(gates * proj, axis=2) @ lp['w_o'])
    h2 = rms_norm(x, lp['norm2_g']) * (1 + scale2) + shift2
    ff = jnp.square(jax.nn.relu(h2 @ lp['w_ff1'] + lp['b_ff1'])) @ lp['w_ff2'] + lp['b_ff2']
    x = x + gate2 * ff
    if ctx is None:
        return x, (k_w, v_w, k_g, v_g, k_d, v_d, lru_final)
    return x, None


def setup_inputs(seed: int = 0) -> dict:
    key = jax.random.key(seed)
    ks = iter(jax.random.split(key, 48))

    def nrm(shape, scale):
        return jax.random.normal(next(ks), shape, jnp.float32) * scale

    def gain(shape):
        return 1.0 + nrm(shape, 0.02)

    a_base = jax.random.uniform(next(ks), (DEPTH, 2, LRU_W), jnp.float32, minval=0.9, maxval=0.999)
    s_base = a_base ** (1.0 / LRU_C)
    lru_lambda = jnp.log(s_base) - jnp.log1p(-s_base)
    return {
        'x_prompt': nrm((BATCH, SEQ, D_MODEL), 1.0),
        'x_sample': nrm((DEC_BATCH, DEC_SEQ, D_MODEL), 1.0),
        'cache_win_k': nrm((DEC_BATCH, DEPTH, PAST_LEN, WIN_KV, HEAD_DIM), 1.0),
        'cache_win_v': nrm((DEC_BATCH, DEPTH, PAST_LEN, WIN_KV, HEAD_DIM), 1.0),
        'cache_grid_k': nrm((DEC_BATCH, DEPTH, PAST_LEN, GRID_KV, HEAD_DIM), 1.0),
        'cache_grid_v': nrm((DEC_BATCH, DEPTH, PAST_LEN, GRID_KV, HEAD_DIM), 1.0),
        'cache_diff_k': nrm((DEC_BATCH, DEPTH, PAST_LEN, DIFF_HEADS, 2, HEAD_DIM), 1.0),
        'cache_diff_v': nrm((DEC_BATCH, DEPTH, PAST_LEN, DIFF_HEADS, 2 * HEAD_DIM), 1.0),
        'state_lru': nrm((DEC_BATCH, DEPTH, 2, LRU_W), 0.5),
        'c': nrm((DEC_BATCH, D_MODEL), 1.0),
        'c_ctx': nrm((D_MODEL,), 1.0),
        'w_ada': nrm((DEPTH, D_MODEL, N_MOD * D_MODEL), 0.5 * D_MODEL ** -0.5),
        'b_ada': nrm((DEPTH, N_MOD * D_MODEL), 0.02),
        'norm1_g': gain((DEPTH, D_MODEL)),
        'norm2_g': gain((DEPTH, D_MODEL)),
        'w_in': nrm((DEPTH, D_MODEL, D_IN), D_MODEL ** -0.5),
        'conv_w': nrm((DEPTH, CONV_W, LRU_W), CONV_W ** -0.5),
        'conv_b': nrm((DEPTH, LRU_W), 0.02),
        'lru_wr': nrm((DEPTH, 2, LRU_BLOCKS, LRU_BW, LRU_BW), LRU_BW ** -0.5),
        'lru_br': nrm((DEPTH, 2, LRU_W), 0.1),
        'lru_wi': nrm((DEPTH, 2, LRU_BLOCKS, LRU_BW, LRU_BW), LRU_BW ** -0.5),
        'lru_bi': nrm((DEPTH, 2, LRU_W), 0.1),
        'lru_lambda': lru_lambda,
        'win_qn': gain((DEPTH, HEAD_DIM)),
        'win_kn': gain((DEPTH, HEAD_DIM)),
        'win_sink': nrm((DEPTH, WIN_HEADS), 0.5),
        'grid_qn': gain((DEPTH, HEAD_DIM)),
        'grid_kn': gain((DEPTH, HEAD_DIM)),
        'diff_qn': gain((DEPTH, HEAD_DIM)),
        'diff_kn': gain((DEPTH, HEAD_DIM)),
        'diff_lq1': nrm((DEPTH, HEAD_DIM), 0.1),
        'diff_lk1': nrm((DEPTH, HEAD_DIM), 0.1),
        'diff_lq2': nrm((DEPTH, HEAD_DIM), 0.1),
        'diff_lk2': nrm((DEPTH, HEAD_DIM), 0.1),
        'diff_out_g': gain((DEPTH, 2 * HEAD_DIM)),
        'w_branch': nrm((DEPTH, N_BRANCH, BRANCH_W, D_MODEL), BRANCH_W ** -0.5),
        'w_gate': nrm((DEPTH, D_MODEL, N_BRANCH * D_MODEL), D_MODEL ** -0.5),
        'b_gate': nrm((DEPTH, N_BRANCH * D_MODEL), 0.02),
        'w_o': nrm((DEPTH, D_MODEL, D_MODEL), D_MODEL ** -0.5),
        'w_ff1': nrm((DEPTH, D_MODEL, D_FF), D_MODEL ** -0.5),
        'b_ff1': nrm((DEPTH, D_FF), 0.02),
        'w_ff2': nrm((DEPTH, D_FF, D_MODEL), D_FF ** -0.5),
        'b_ff2': nrm((DEPTH, D_MODEL), 0.02),
    }


def reference(x_prompt, x_sample, cache_win_k, cache_win_v, cache_grid_k, cache_grid_v, cache_diff_k,
              cache_diff_v, state_lru, c, c_ctx, w_ada, b_ada, norm1_g, norm2_g, w_in, conv_w, conv_b,
              lru_wr, lru_br, lru_wi, lru_bi, lru_lambda, win_qn, win_kn, win_sink, grid_qn, grid_kn,
              diff_qn, diff_kn, diff_lq1, diff_lk1, diff_lq2, diff_lk2, diff_out_g, w_branch, w_gate,
              b_gate, w_o, w_ff1, b_ff1, w_ff2, b_ff2):
    rope = grid_rope(x_sample.shape[1])
    y_p, y_s = x_prompt, x_sample
    wk, wv, gk, gv, dk, dv, ls = [], [], [], [], [], [], []
    for l in range(DEPTH):
        lp = {
            'norm1_g': norm1_g[l], 'norm2_g': norm2_g[l], 'w_in': w_in[l],
            'conv_w': conv_w[l], 'conv_b': conv_b[l],
            'lru_wr': lru_wr[l], 'lru_br': lru_br[l], 'lru_wi': lru_wi[l], 'lru_bi': lru_bi[l],
            'lru_lambda': lru_lambda[l],
            'win_qn': win_qn[l], 'win_kn': win_kn[l], 'win_sink': win_sink[l],
            'grid_qn': grid_qn[l], 'grid_kn': grid_kn[l],
            'diff_qn': diff_qn[l], 'diff_kn': diff_kn[l], 'diff_lq1': diff_lq1[l], 'diff_lk1': diff_lk1[l],
            'diff_lq2': diff_lq2[l], 'diff_lk2': diff_lk2[l], 'diff_out_g': diff_out_g[l],
            'w_branch': w_branch[l], 'w_gate': w_gate[l], 'b_gate': b_gate[l], 'w_o': w_o[l],
            'w_ff1': w_ff1[l], 'b_ff1': b_ff1[l], 'w_ff2': w_ff2[l], 'b_ff2': b_ff2[l],
        }
        y_p, ctx_l = trunk_layer(y_p, modulation(c_ctx, w_ada[l], b_ada[l]), lp, l, None, None)
        wk.append(ctx_l[0]); wv.append(ctx_l[1]); gk.append(ctx_l[2]); gv.append(ctx_l[3])
        dk.append(ctx_l[4]); dv.append(ctx_l[5]); ls.append(ctx_l[6])
        cached = (cache_win_k[:, l], cache_win_v[:, l], cache_grid_k[:, l], cache_grid_v[:, l],
                  cache_diff_k[:, l], cache_diff_v[:, l], state_lru[:, l])
        y_s, _ = trunk_layer(y_s, modulation(c, w_ada[l], b_ada[l])[:, None], lp, l, rope, cached)
    new_win_k = jnp.stack(wk, axis=1)
    new_win_v = jnp.stack(wv, axis=1)
    new_grid_k = jnp.stack(gk, axis=1)
    new_grid_v = jnp.stack(gv, axis=1)
    new_diff_k = jnp.stack(dk, axis=1)
    new_diff_v = jnp.stack(dv, axis=1)
    new_lru_state = jnp.stack(ls, axis=1)
    return (y_p, y_s, new_win_k, new_win_v, new_grid_k, new_grid_v, new_diff_k, new_diff_v, new_lru_state)
```

```python
import functools
import math

import jax
import jax.numpy as jnp
import numpy as np
from jax import lax
from jax.experimental import pallas as pl
from jax.experimental.pallas import tpu as pltpu

F32 = jnp.float32
BF16 = jnp.bfloat16

D_MODEL = 2048
BATCH = 32
SEQ = 256
DEPTH = 4
DEC_BATCH = 4
DEC_SEQ = 4096
PAST_LEN = 512
GRID_W = 64
BLOCK = 128
HEAD_DIM = 64
N_FREQ = HEAD_DIM // 4
ROPE_BASE = 10000.0
ATTN_SCALE = HEAD_DIM ** -0.5
RMS_EPS = 1e-6
N_MOD = 6
N_BRANCH = 4
BRANCH_W = D_MODEL // 4
LRU_W = BRANCH_W
LRU_BLOCKS = 8
LRU_BW = LRU_W // LRU_BLOCKS
LRU_C = 8.0
CONV_W = 4
WIN_HEADS = 8
WIN_KV = 2
DIFF_HEADS = 4
D_FF = 4 * D_MODEL
D_IN = 4096

V7X_VMEM_BYTES = 64 * 1024 * 1024
VMEM_LIMIT = V7X_VMEM_BYTES - 8 * 1024 * 1024
LANES = 128
SUBLANES = 8
NEG = -1e30
MOD_ROWS = 8

_ORIG = dict(xa=(0, 512), ya=(512, 512), qw=(1024, 512), kw=(1536, 128), vw=(1664, 128),
             qg=(1792, 512), kg=(2304, 128), vg=(2432, 128), qd=(2560, 512), kd=(3072, 512),
             vd=(3584, 512))
_ORDER = ('xa', 'ya', 'qw', 'qg', 'qd', 'kd', 'vd', 'kw', 'vw', 'kg', 'vg')
_COL = {}
_off = 0
for _n in _ORDER:
    _COL[_n] = _off
    _off += _ORIG[_n][1]
_PERM = np.concatenate([np.arange(_ORIG[n][0], _ORIG[n][0] + _ORIG[n][1]) for n in _ORDER])


def _cparams():
    return pltpu.CompilerParams(vmem_limit_bytes=VMEM_LIMIT)


def _dot(a, b):
    return jnp.dot(a, b, preferred_element_type=F32)


def _dot_nt(a, b):
    return lax.dot_general(a, b, (((1,), (1,)), ((), ())), preferred_element_type=F32)


def _mod_kernel(c_ref, w_ref, b_ref, o_ref):
    c = c_ref[...]
    s = (c * jax.nn.sigmoid(c)).astype(BF16)
    o_ref[0] = _dot(s, w_ref[0].astype(BF16)) + b_ref[0]


def _modulation(cond, w_ada, b_ada):
    tn = 1024
    n = N_MOD * D_MODEL
    return pl.pallas_call(
        _mod_kernel,
        grid=(DEPTH, n // tn),
        in_specs=[pl.BlockSpec((MOD_ROWS, D_MODEL), lambda l, j: (0, 0)),
                  pl.BlockSpec((1, D_MODEL, tn), lambda l, j: (l, 0, j)),
                  pl.BlockSpec((1, 1, tn), lambda l, j: (l, 0, j))],
        out_specs=pl.BlockSpec((1, MOD_ROWS, tn), lambda l, j: (l, 0, j)),
        out_shape=jax.ShapeDtypeStruct((DEPTH, MOD_ROWS, n), F32),
        compiler_params=_cparams(),
        name='modulation',
    )(cond, w_ada, b_ada.reshape(DEPTH, 1, n))


def _mod_index(latent, tm):
    if latent:
        per = DEC_SEQ // tm
        return lambda i, j: (1 + i // per, 0, 0)
    return lambda i, j: (0, 0, 0)


def _norm_mod(x, g, mod, shift_idx, scale_idx):
    var = jnp.mean(x * x, axis=-1, keepdims=True)
    y = x * lax.rsqrt(var + RMS_EPS) * g
    return y * (1.0 + mod[scale_idx:scale_idx + 1]) + mod[shift_idx:shift_idx + 1]


def _nm_kernel(*refs, has_bias, act):
    if has_bias:
        x_ref, mod_ref, g_ref, w_ref, b_ref, o_ref, h_scr = refs
    else:
        x_ref, mod_ref, g_ref, w_ref, o_ref, h_scr = refs

    @pl.when(pl.program_id(1) == 0)
    def _():
        h_scr[...] = _norm_mod(x_ref[...], g_ref[...], mod_ref[0], 0, 1).astype(BF16)

    acc = _dot(h_scr[...], w_ref[...])
    if has_bias:
        acc = acc + b_ref[...]
    if act == 'sigmoid':
        acc = jax.nn.sigmoid(acc)
    o_ref[...] = acc.astype(o_ref.dtype)


def _normmod_matmul(x, mod, g, w, bias, act, out_dtype, latent, name):
    m, n = x.shape[0], w.shape[1]
    tm, tn = 512, 1024
    in_specs = [pl.BlockSpec((tm, D_MODEL), lambda i, j: (i, 0)),
                pl.BlockSpec((1, N_MOD, D_MODEL), _mod_index(latent, tm)),
                pl.BlockSpec((1, D_MODEL), lambda i, j: (0, 0)),
                pl.BlockSpec((D_MODEL, tn), lambda i, j: (0, j))]
    args = [x, mod, g.reshape(1, D_MODEL), w]
    if bias is not None:
        in_specs.append(pl.BlockSpec((1, tn), lambda i, j: (0, j)))
        args.append(bias.reshape(1, n))
    return pl.pallas_call(
        functools.partial(_nm_kernel, has_bias=bias is not None, act=act),
        grid=(m // tm, n // tn),
        in_specs=in_specs,
        out_specs=pl.BlockSpec((tm, tn), lambda i, j: (i, j)),
        out_shape=jax.ShapeDtypeStruct((m, n), out_dtype),
        scratch_shapes=[pltpu.VMEM((tm, D_MODEL), BF16)],
        compiler_params=_cparams(),
        name=name,
    )(*args)


def _lane_lo():
    return lax.broadcasted_iota(jnp.int32, (1, LANES), 1) < HEAD_DIM


def _seg_matrix():
    r = lax.broadcasted_iota(jnp.int32, (LANES, LANES), 0) // HEAD_DIM
    c = lax.broadcasted_iota(jnp.int32, (LANES, LANES), 1) // HEAD_DIM
    return jnp.where(r == c, 1.0, 0.0).astype(BF16)


def _head_rmsnorm(x, gain):
    x2 = x * x
    hi = x2.astype(BF16)
    lo = (x2 - hi.astype(F32)).astype(BF16)
    seg = _seg_matrix()
    ms = (_dot(hi, seg) + _dot(lo, seg)) * (1.0 / HEAD_DIM)
    return x * lax.rsqrt(ms + RMS_EPS) * gain


def _rope(x, cos, sin_signed):
    lane = lax.broadcasted_iota(jnp.int32, (1, LANES), 1)
    first = (lane & (2 * N_FREQ - 1)) < N_FREQ
    up = pltpu.roll(x, LANES - N_FREQ, 1)
    dn = pltpu.roll(x, N_FREQ, 1)
    return x * cos + jnp.where(first, up, dn) * sin_signed


def _rope_tables(t):
    pos = jnp.arange(t)
    row = (pos // GRID_W).astype(F32)
    col = (pos % GRID_W).astype(F32)
    inv = ROPE_BASE ** (-jnp.arange(N_FREQ, dtype=F32) / N_FREQ)
    ar, ac = row[:, None] * inv, col[:, None] * inv
    cos = jnp.concatenate([jnp.cos(ar), jnp.cos(ar), jnp.cos(ac), jnp.cos(ac)], axis=-1)
    sin = jnp.concatenate([-jnp.sin(ar), jnp.sin(ar), -jnp.sin(ac), jnp.sin(ac)], axis=-1)
    return jnp.tile(cos, (1, 2)), jnp.tile(sin, (1, 2))


def _gqa_kernel(*refs, t, tq, n_ctx, banded, has_sink, has_rope, emit_k, ck):
    it = iter(refs)
    q_ref, k_ref, v_ref, qn_ref, kn_ref = (next(it) for _ in range(5))
    if has_rope:
        cq_ref, sq_ref, cka_ref, ska_ref = (next(it) for _ in range(4))
    if n_ctx:
        kc_ref, vc_ref = next(it), next(it)
    if has_sink:
        sink_ref = next(it)
    o_ref = next(it)
    if emit_k:
        ko_ref = next(it)
    k_scr, v_scr = next(it), next(it)

    g = pl.program_id(1)
    qi = pl.program_id(2)
    lo = _lane_lo()
    lat0 = BLOCK if banded else 0
    ctx0 = t + 2 * lat0
    first_head = g == 0

    def put(dst, kx, vx):
        kr = pltpu.roll(kx, HEAD_DIM, 1)
        vr = pltpu.roll(vx, HEAD_DIM, 1)
        zero = jnp.zeros_like(kx)
        k_scr[0, dst, :] = jnp.where(lo, jnp.where(first_head, kx, kr), zero).astype(BF16)
        k_scr[1, dst, :] = jnp.where(lo, zero, jnp.where(first_head, kr, kx)).astype(BF16)
        v_scr[0, dst, :] = jnp.where(lo, jnp.where(first_head, vx, vr), zero).astype(BF16)
        v_scr[1, dst, :] = jnp.where(lo, zero, jnp.where(first_head, vr, vx)).astype(BF16)

    @pl.when(qi == 0)
    def _build():
        step = min(t, 512)
        for r in range(0, t, step):
            kx = _head_rmsnorm(k_ref[r:r + step, :], kn_ref[...])
            if emit_k:
                ko_ref[r:r + step, :] = kx
            if has_rope:
                kx = _rope(kx, cka_ref[r:r + step, :], ska_ref[r:r + step, :])
            put(slice(lat0 + r, lat0 + r + step), kx, v_ref[r:r + step, :])
        if banded:
            zpad = jnp.zeros((BLOCK, LANES), BF16)
            for scr in (k_scr, v_scr):
                for var in range(2):
                    scr[var, 0:BLOCK, :] = zpad
                    scr[var, lat0 + t:lat0 + t + BLOCK, :] = zpad
        if n_ctx:
            put(slice(ctx0, ctx0 + n_ctx), kc_ref[0, 0], vc_ref[0, 0])

    def step_fn(q2, start, size, mask, carry):
        m0, m1, l, acc = carry
        k0 = k_scr[0, pl.ds(start, size), :]
        k1 = k_scr[1, pl.ds(start, size), :]
        s0 = _dot_nt(q2, k0)
        s1 = _dot_nt(q2, k1)
        if mask is not None:
            s0 = jnp.where(mask, s0, NEG)
            s1 = jnp.where(mask, s1, NEG)
        n0 = jnp.maximum(m0, jnp.max(s0, axis=-1, keepdims=True))
        n1 = jnp.maximum(m1, jnp.max(s1, axis=-1, keepdims=True))
        p0 = jnp.exp(s0 - n0)
        p1 = jnp.exp(s1 - n1)
        alpha = jnp.where(lo, jnp.exp(m0 - n0), jnp.exp(m1 - n1))
        rs = jnp.where(lo, jnp.sum(p0, axis=-1, keepdims=True), jnp.sum(p1, axis=-1, keepdims=True))
        l = alpha * l + rs
        acc = (alpha * acc + _dot(p0.astype(BF16), v_scr[0, pl.ds(start, size), :])
               + _dot(p1.astype(BF16), v_scr[1, pl.ds(start, size), :]))
        return n0, n1, l, acc

    q = q_ref[...]
    for p in range(2):
        qp = _head_rmsnorm(q[:, p * LANES:(p + 1) * LANES], qn_ref[...])
        if has_rope:
            qp = _rope(qp, cq_ref[...], sq_ref[...])
        q2 = (qp * ATTN_SCALE).astype(BF16)
        if has_sink:
            h0 = g * 4 + 2 * p
            m0 = jnp.full((tq, 1), sink_ref[h0], F32)
            m1 = jnp.full((tq, 1), sink_ref[h0 + 1], F32)
            l = jnp.ones((tq, LANES), F32)
        else:
            m0 = jnp.full((tq, 1), NEG, F32)
            m1 = jnp.full((tq, 1), NEG, F32)
            l = jnp.zeros((tq, LANES), F32)
        carry = (m0, m1, l, jnp.zeros((tq, LANES), F32))
        if banded:
            rr = lax.broadcasted_iota(jnp.int32, (tq, 3 * BLOCK), 0)
            cc = lax.broadcasted_iota(jnp.int32, (tq, 3 * BLOCK), 1)
            kpos = (qi - 1) * BLOCK + cc
            mask = (cc >= rr) & (cc - rr <= 2 * BLOCK) & (kpos >= 0) & (kpos < t)
            carry = step_fn(q2, pl.multiple_of(qi * BLOCK, BLOCK), 3 * BLOCK, mask, carry)
            if n_ctx:
                carry = step_fn(q2, ctx0, n_ctx, None, carry)
        else:
            total = t + n_ctx
            nchunk = total // ck
            if nchunk <= 2:
                for c in range(nchunk):
                    carry = step_fn(q2, c * ck, ck, None, carry)
            else:
                carry = lax.fori_loop(
                    0, nchunk,
                    lambda c, cr: step_fn(q2, pl.multiple_of(c * ck, ck), ck, None, cr), carry)
        _, _, l, acc = carry
        o_ref[:, p * LANES:(p + 1) * LANES] = (acc / l).astype(o_ref.dtype)


def _gqa_call(z, *, b, t, tq, qname, kname, vname, qn, kn, sink, rope, cache, layer, banded,
              emit_k, name):
    nq = t // tq
    n_ctx = PAST_LEN if cache is not None else 0
    rows = t + n_ctx + (2 * BLOCK if banded else 0)
    ck = min(512, t)
    qblk, kblk, vblk = _COL[qname] // 256, _COL[kname] // LANES, _COL[vname] // LANES
    const = lambda bb, g, i: (0, 0)
    in_specs = [pl.BlockSpec((tq, 256), lambda bb, g, i: (bb * nq + i, qblk + g)),
                pl.BlockSpec((t, LANES), lambda bb, g, i: (bb, kblk)),
                pl.BlockSpec((t, LANES), lambda bb, g, i: (bb, vblk)),
                pl.BlockSpec((1, LANES), const),
                pl.BlockSpec((1, LANES), const)]
    args = [z, z, z, jnp.tile(qn, 2).reshape(1, LANES), jnp.tile(kn, 2).reshape(1, LANES)]
    if rope is not None:
        cos, sin = rope
        in_specs += [pl.BlockSpec((tq, LANES), lambda bb, g, i: (i, 0)),
                     pl.BlockSpec((tq, LANES), lambda bb, g, i: (i, 0)),
                     pl.BlockSpec((t, LANES), const),
                     pl.BlockSpec((t, LANES), const)]
        args += [cos, sin, cos, sin]
    if cache is not None:
        spec = pl.BlockSpec((1, 1, n_ctx, LANES), lambda bb, g, i: (bb, layer, 0, 0))
        in_specs += [spec, spec]
        args += [cache[0], cache[1]]
    if sink is not None:
        in_specs.append(pl.BlockSpec(memory_space=pltpu.SMEM))
        args.append(sink)
    out_shape = [jax.ShapeDtypeStruct((b * t, 512), BF16)]
    out_specs = [pl.BlockSpec((tq, 256), lambda bb, g, i: (bb * nq + i, g))]
    if emit_k:
        out_shape.append(jax.ShapeDtypeStruct((b * t, LANES), F32))
        out_specs.append(pl.BlockSpec((t, LANES), lambda bb, g, i: (bb, 0)))
    res = pl.pallas_call(
        functools.partial(_gqa_kernel, t=t, tq=tq, n_ctx=n_ctx, banded=banded,
                          has_sink=sink is not None, has_rope=rope is not None, emit_k=emit_k, ck=ck),
        grid=(b, 2, nq),
        in_specs=in_specs,
        out_specs=out_specs,
        out_shape=out_shape,
        scratch_shapes=[pltpu.VMEM((2, rows, LANES), BF16), pltpu.VMEM((2, rows, LANES), BF16)],
        compiler_params=_cparams(),
        name=name,
    )(*args)
    return res if emit_k else (res[0], None)


def _diff_kernel(*refs, t, tq, n_ctx, has_rope, emit_k, ck, lam_init):
    it = iter(refs)
    q_ref, k_ref, v_ref, qn_ref, kn_ref, lp_ref, og_ref = (next(it) for _ in range(7))
    if has_rope:
        cq_ref, sq_ref, cka_ref, ska_ref = (next(it) for _ in range(4))
    if n_ctx:
        kc_ref, vc_ref = next(it), next(it)
    o_ref = next(it)
    if emit_k:
        ko_ref = next(it)
    k_scr, v_scr = next(it), next(it)

    qi = pl.program_id(2)
    lo = _lane_lo()

    def put(dst, kx, vx):
        zero = jnp.zeros_like(kx)
        k_scr[0, dst, :] = jnp.where(lo, kx, zero).astype(BF16)
        k_scr[1, dst, :] = jnp.where(lo, zero, kx).astype(BF16)
        v_scr[dst, :] = vx.astype(BF16)

    @pl.when(qi == 0)
    def _build():
        step = min(t, 512)
        for r in range(0, t, step):
            kx = _head_rmsnorm(k_ref[r:r + step, :], kn_ref[...])
            if emit_k:
                ko_ref[r:r + step, :] = kx
            if has_rope:
                kx = _rope(kx, cka_ref[r:r + step, :], ska_ref[r:r + step, :])
            put(slice(r, r + step), kx, v_ref[r:r + step, :])
        if n_ctx:
            put(slice(t, t + n_ctx), kc_ref[0, 0], vc_ref[0, 0])

    qp = _head_rmsnorm(q_ref[...], qn_ref[...])
    if has_rope:
        qp = _rope(qp, cq_ref[...], sq_ref[...])
    q2 = (qp * ATTN_SCALE).astype(BF16)

    def step_fn(start, carry):
        m0, m1, l0, l1, a0, a1 = carry
        v = v_scr[pl.ds(start, ck), :]
        s0 = _dot_nt(q2, k_scr[0, pl.ds(start, ck), :])
        s1 = _dot_nt(q2, k_scr[1, pl.ds(start, ck), :])
        n0 = jnp.maximum(m0, jnp.max(s0, axis=-1, keepdims=True))
        n1 = jnp.maximum(m1, jnp.max(s1, axis=-1, keepdims=True))
        p0 = jnp.exp(s0 - n0)
        p1 = jnp.exp(s1 - n1)
        e0 = jnp.exp(m0 - n0)
        e1 = jnp.exp(m1 - n1)
        l0 = e0 * l0 + jnp.sum(p0, axis=-1, keepdims=True)
        l1 = e1 * l1 + jnp.sum(p1, axis=-1, keepdims=True)
        a0 = e0 * a0 + _dot(p0.astype(BF16), v)
        a1 = e1 * a1 + _dot(p1.astype(BF16), v)
        return n0, n1, l0, l1, a0, a1

    col = lambda val: jnp.full((tq, 1), val, F32)
    carry = (col(NEG), col(NEG), col(0.0), col(0.0),
             jnp.zeros((tq, LANES), F32), jnp.zeros((tq, LANES), F32))
    nchunk = (t + n_ctx) // ck
    if nchunk <= 2:
        for c in range(nchunk):
            carry = step_fn(c * ck, carry)
    else:
        carry = lax.fori_loop(0, nchunk, lambda c, cr: step_fn(pl.multiple_of(c * ck, ck), cr), carry)
    _, _, l0, l1, a0, a1 = carry

    lp = lp_ref[...]
    lam = (jnp.exp(jnp.sum(lp[0:1] * lp[1:2], axis=-1, keepdims=True))
           - jnp.exp(jnp.sum(lp[2:3] * lp[3:4], axis=-1, keepdims=True)) + lam_init)
    o = a0 / l0 - lam * (a1 / l1)
    var = jnp.mean(o * o, axis=-1, keepdims=True)
    o = o * lax.rsqrt(var + RMS_EPS) * og_ref[...] * (1.0 - lam_init)
    o_ref[...] = o.astype(o_ref.dtype)


def _diff_call(z, *, b, t, tq, qn, kn, lparams, out_g, rope, cache, layer, emit_k, name):
    nq = t // tq
    n_ctx = PAST_LEN if cache is not None else 0
    ck = min(512, t)
    qblk, kblk, vblk = _COL['qd'] // LANES, _COL['kd'] // LANES, _COL['vd'] // LANES
    const = lambda bb, h, i: (0, 0)
    in_specs = [pl.BlockSpec((tq, LANES), lambda bb, h, i: (bb * nq + i, qblk + h)),
                pl.BlockSpec((t, LANES), lambda bb, h, i: (bb, kblk + h)),
                pl.BlockSpec((t, LANES), lambda bb, h, i: (bb, vblk + h)),
                pl.BlockSpec((1, LANES), const),
                pl.BlockSpec((1, LANES), const),
                pl.BlockSpec((4, HEAD_DIM), const),
                pl.BlockSpec((1, LANES), const)]
    args = [z, z, z, jnp.tile(qn, 2).reshape(1, LANES), jnp.tile(kn, 2).reshape(1, LANES),
            lparams, out_g.reshape(1, LANES)]
    if rope is not None:
        cos, sin = rope
        in_specs += [pl.BlockSpec((tq, LANES), lambda bb, h, i: (i, 0)),
                     pl.BlockSpec((tq, LANES), lambda bb, h, i: (i, 0)),
                     pl.BlockSpec((t, LANES), const),
                     pl.BlockSpec((t, LANES), const)]
        args += [cos, sin, cos, sin]
    if cache is not None:
        spec = pl.BlockSpec((1, 1, n_ctx, LANES), lambda bb, h, i: (bb, layer, 0, h))
        in_specs += [spec, spec]
        args += [cache[0], cache[1]]
    out_shape = [jax.ShapeDtypeStruct((b * t, 512), BF16)]
    out_specs = [pl.BlockSpec((tq, LANES), lambda bb, h, i: (bb * nq + i, h))]
    if emit_k:
        out_shape.append(jax.ShapeDtypeStruct((b * t, 512), F32))
        out_specs.append(pl.BlockSpec((t, LANES), lambda bb, h, i: (bb, h)))
    lam_init = 0.8 - 0.6 * math.exp(-0.3 * layer)
    res = pl.pallas_call(
        functools.partial(_diff_kernel, t=t, tq=tq, n_ctx=n_ctx, has_rope=rope is not None,
                          emit_k=emit_k, ck=ck, lam_init=lam_init),
        grid=(b, DIFF_HEADS, nq),
        in_specs=in_specs,
        out_specs=out_specs,
        out_shape=out_shape,
        scratch_shapes=[pltpu.VMEM((2, t + n_ctx, LANES), BF16), pltpu.VMEM((t + n_ctx, LANES), BF16)],
        compiler_params=_cparams(),
        name=name,
    )(*args)
    return res if emit_k else (res[0], None)


def _softplus(x):
    return jnp.maximum(x, 0.0) + jnp.log1p(jnp.exp(-jnp.abs(x)))


def _lru_kernel(*refs, tc, nc, has_h0, emit_final):
    it = iter(refs)
    xp_ref, xm_ref, xn_ref, ya_ref, cw_ref, cb_ref, wl_ref, bl_ref, lam_ref = (next(it) for _ in range(9))
    if has_h0:
        h0_ref = next(it)
    o_ref = next(it)
    if emit_final:
        fin_ref = next(it)
    hf_scr, a_scr, b_scr, carry_scr = (next(it) for _ in range(4))

    s = pl.program_id(1)
    fwd = s < nc
    c = jnp.where(fwd, s, 2 * nc - 1 - s)
    t0 = pl.multiple_of(c * tc, tc)

    @pl.when(s == 0)
    def _():
        if has_h0:
            carry_scr[...] = h0_ref[0]
        else:
            carry_scr[...] = jnp.zeros_like(carry_scr)

    prev = jnp.where(c > 0, xp_ref[...], 0.0)
    nxt = jnp.where(c < nc - 1, xn_ref[...], 0.0)
    ext = jnp.concatenate([prev, xm_ref[...], nxt], axis=0)
    n_ext = tc + 2 * SUBLANES
    u = cb_ref[...] + cw_ref[1:2] * xm_ref[...]
    u = u + cw_ref[0:1] * pltpu.roll(ext, 1, 0)[SUBLANES:SUBLANES + tc]
    u = u + cw_ref[2:3] * pltpu.roll(ext, n_ext - 1, 0)[SUBLANES:SUBLANES + tc]
    u = u + cw_ref[3:4] * pltpu.roll(ext, n_ext - 2, 0)[SUBLANES:SUBLANES + tc]

    gates = _dot(u.astype(BF16), wl_ref[0]) + bl_ref[0]
    r = jax.nn.sigmoid(gates[:, :LRU_W])
    ig = jax.nn.sigmoid(gates[:, LRU_W:])
    log_a = -LRU_C * r * _softplus(-lam_ref[0])
    a = jnp.exp(log_a)
    a_scr[...] = a
    b_scr[...] = jnp.sqrt(-jnp.tanh(log_a) * (a * a + 1.0)) * ig * u

    ntile = tc // SUBLANES
    row = lax.broadcasted_iota(jnp.int32, (SUBLANES, LRU_W), 0)

    def scan(forward):
        def tile(i, carry):
            j = i if forward else ntile - 1 - i
            r0 = pl.multiple_of(j * SUBLANES, SUBLANES)
            a = a_scr[pl.ds(r0, SUBLANES), :]
            bv = b_scr[pl.ds(r0, SUBLANES), :]
            for d in (1, 2, 4):
                shift = d if forward else SUBLANES - d
                msk = (row >= d) if forward else (row < SUBLANES - d)
                ap = pltpu.roll(a, shift, 0)
                bp = pltpu.roll(bv, shift, 0)
                bv = jnp.where(msk, a * bp + bv, bv)
                a = jnp.where(msk, a * ap, a)
            h = a * carry + bv
            g0 = pl.multiple_of(t0 + r0, SUBLANES)
            if forward:
                hf_scr[pl.ds(g0, SUBLANES), :] = h
                return h[SUBLANES - 1:SUBLANES, :]
            b_scr[pl.ds(r0, SUBLANES), :] = h
            return h[0:1, :]

        idx = 0 if forward else 1
        last = lax.fori_loop(0, ntile, tile, carry_scr[idx:idx + 1, :])
        carry_scr[idx:idx + 1, :] = last
        if emit_final:
            fin_ref[0, idx:idx + 1, :] = last
        if not forward:
            o_ref[...] = ((hf_scr[pl.ds(t0, tc), :] + b_scr[...])
                          * jax.nn.gelu(ya_ref[...])).astype(o_ref.dtype)

    pl.when(fwd)(lambda: scan(True))
    pl.when(jnp.logical_not(fwd))(lambda: scan(False))


def _lru_call(z, *, b, t, conv_w, conv_b, w_lru, b_lru, lam, h0, emit_final, name):
    tc = min(t, 512)
    nc = t // tc
    per8 = tc // SUBLANES
    nrow8 = b * t // SUBLANES

    def chunk(s):
        return jnp.where(s < nc, s, 2 * nc - 1 - s)

    def hold(s):
        return jnp.where(s < nc, nc - 1, 2 * nc - 1 - s)

    in_specs = [
        pl.BlockSpec((SUBLANES, LRU_W),
                     lambda bb, s: (jnp.maximum((bb * nc + chunk(s)) * per8 - 1, 0), 0)),
        pl.BlockSpec((tc, LRU_W), lambda bb, s: (bb * nc + chunk(s), 0)),
        pl.BlockSpec((SUBLANES, LRU_W),
                     lambda bb, s: (jnp.minimum((bb * nc + chunk(s) + 1) * per8, nrow8 - 1), 0)),
        pl.BlockSpec((tc, LRU_W), lambda bb, s: (bb * nc + hold(s), 1)),
        pl.BlockSpec((CONV_W, LRU_W), lambda bb, s: (0, 0)),
        pl.BlockSpec((1, LRU_W), lambda bb, s: (0, 0)),
        pl.BlockSpec((1, LRU_W, 2 * LRU_W), lambda bb, s: (s // nc, 0, 0)),
        pl.BlockSpec((1, 1, 2 * LRU_W), lambda bb, s: (s // nc, 0, 0)),
        pl.BlockSpec((1, 1, LRU_W), lambda bb, s: (s // nc, 0, 0)),
    ]
    args = [z, z, z, z, conv_w, conv_b.reshape(1, LRU_W), w_lru, b_lru, lam.reshape(2, 1, LRU_W)]
    if h0 is not None:
        in_specs.append(pl.BlockSpec((1, 2, LRU_W), lambda bb, s: (bb, 0, 0)))
        args.append(h0)
    out_shape = [jax.ShapeDtypeStruct((b * t, LRU_W), BF16)]
    out_specs = [pl.BlockSpec((tc, LRU_W), lambda bb, s: (bb * nc + hold(s), 0))]
    if emit_final:
        out_shape.append(jax.ShapeDtypeStruct((b, 2, LRU_W), F32))
        out_specs.append(pl.BlockSpec((1, 2, LRU_W), lambda bb, s: (bb, 0, 0)))
    res = pl.pallas_call(
        functools.partial(_lru_kernel, tc=tc, nc=nc, has_h0=h0 is not None, emit_final=emit_final),
        grid=(b, 2 * nc),
        in_specs=in_specs,
        out_specs=out_specs,
        out_shape=out_shape,
        scratch_shapes=[pltpu.VMEM((t, LRU_W), F32), pltpu.VMEM((tc, LRU_W), F32),
                        pltpu.VMEM((tc, LRU_W), F32), pltpu.VMEM((2, LRU_W), F32)],
        compiler_params=_cparams(),
        name=name,
    )(*args)
    return res if emit_final else (res[0], None)


def _merge_kernel(x_ref, gt_ref, a_ref, w_ref, g_ref, d_ref, wb_ref, wo_ref, mod_ref, o_ref, acc_scr):
    bidx = pl.program_id(1)

    def contrib(br_ref):
        return gt_ref[...].astype(F32) * _dot(br_ref[...], wb_ref[0])

    @pl.when(bidx == 0)
    def _():
        acc_scr[...] = contrib(a_ref)

    @pl.when(bidx == 1)
    def _():
        acc_scr[...] += contrib(w_ref)

    @pl.when(bidx == 2)
    def _():
        acc_scr[...] += contrib(g_ref)

    @pl.when(bidx == 3)
    def _():
        merged = (acc_scr[...] + contrib(d_ref)).astype(BF16)
        o_ref[...] = x_ref[...] + mod_ref[0][2:3] * _dot(merged, wo_ref[...])


def _merge_call(x, gates, branches, w_branch, w_o, mod, latent, name):
    m = x.shape[0]
    tm = 256
    row = lambda i, j: (i, 0)
    in_specs = [pl.BlockSpec((tm, D_MODEL), row),
                pl.BlockSpec((tm, D_MODEL), lambda i, j: (i, j))]
    in_specs += [pl.BlockSpec((tm, BRANCH_W), row)] * N_BRANCH
    in_specs += [pl.BlockSpec((1, BRANCH_W, D_MODEL), lambda i, j: (j, 0, 0)),
                 pl.BlockSpec((D_MODEL, D_MODEL), lambda i, j: (0, 0)),
                 pl.BlockSpec((1, N_MOD, D_MODEL), _mod_index(latent, tm))]
    return pl.pallas_call(
        _merge_kernel,
        grid=(m // tm, N_BRANCH),
        in_specs=in_specs,
        out_specs=pl.BlockSpec((tm, D_MODEL), row),
        out_shape=jax.ShapeDtypeStruct((m, D_MODEL), F32),
        scratch_shapes=[pltpu.VMEM((tm, D_MODEL), F32)],
        compiler_params=_cparams(),
        name=name,
    )(x, gates, *branches, w_branch, w_o, mod)


def _ffn_kernel(x_ref, mod_ref, g_ref, w1_ref, b1_ref, w2_ref, b2_ref, o_ref, h_scr, acc_scr):
    j = pl.program_id(1)

    @pl.when(j == 0)
    def _():
        h_scr[...] = _norm_mod(x_ref[...], g_ref[...], mod_ref[0], 3, 4).astype(BF16)
        acc_scr[...] = jnp.zeros_like(acc_scr)

    a = jnp.maximum(_dot(h_scr[...], w1_ref[...]) + b1_ref[...], 0.0)
    acc_scr[...] += _dot((a * a).astype(BF16), w2_ref[...])

    @pl.when(j == pl.num_programs(1) - 1)
    def _():
        o_ref[...] = x_ref[...] + mod_ref[0][5:6] * (acc_scr[...] + b2_ref[...])


def _ffn_call(x, mod, g, w1, b1, w2, b2, latent, name):
    m = x.shape[0]
    tm, tf = 512, 512
    return pl.pallas_call(
        _ffn_kernel,
        grid=(m // tm, D_FF // tf),
        in_specs=[pl.BlockSpec((tm, D_MODEL), lambda i, j: (i, 0)),
                  pl.BlockSpec((1, N_MOD, D_MODEL), _mod_index(latent, tm)),
                  pl.BlockSpec((1, D_MODEL), lambda i, j: (0, 0)),
                  pl.BlockSpec((D_MODEL, tf), lambda i, j: (0, j)),
                  pl.BlockSpec((1, tf), lambda i, j: (0, j)),
                  pl.BlockSpec((tf, D_MODEL), lambda i, j: (j, 0)),
                  pl.BlockSpec((1, D_MODEL), lambda i, j: (0, 0))],
        out_specs=pl.BlockSpec((tm, D_MODEL), lambda i, j: (i, 0)),
        out_shape=jax.ShapeDtypeStruct((m, D_MODEL), F32),
        scratch_shapes=[pltpu.VMEM((tm, D_MODEL), BF16), pltpu.VMEM((tm, D_MODEL), F32)],
        compiler_params=_cparams(),
        name=name,
    )(x, mod, g.reshape(1, D_MODEL), w1, b1.reshape(1, D_FF), w2, b2.reshape(1, D_MODEL))


def _block_diag(w):
    eye = jnp.eye(LRU_BLOCKS, dtype=w.dtype)
    return jnp.einsum('ncd,nm->ncmd', w, eye).reshape(LRU_W, LRU_W)


def _trunk_layer(x, mod, wts, layer, latent, rope, cached):
    b, t = (DEC_BATCH, DEC_SEQ) if latent else (BATCH, SEQ)
    tag = ('lat' if latent else 'ctx') + str(layer)
    z = _normmod_matmul(x, mod, wts['norm1_g'], wts['w_in'], None, None, F32, latent, 'inproj_' + tag)
    gates = _normmod_matmul(x, mod, wts['norm1_g'], wts['w_gate'], wts['b_gate'], 'sigmoid', BF16,
                            latent, 'gates_' + tag)
    a_out, lru_final = _lru_call(
        z, b=b, t=t, conv_w=wts['conv_w'], conv_b=wts['conv_b'], w_lru=wts['w_lru'], b_lru=wts['b_lru'],
        lam=wts['lru_lambda'], h0=cached[6] if latent else None, emit_final=not latent, name='lru_' + tag)
    w_out, k_w = _gqa_call(
        z, b=b, t=t, tq=BLOCK if latent else SEQ, qname='qw', kname='kw', vname='vw',
        qn=wts['win_qn'], kn=wts['win_kn'], sink=wts['win_sink'], rope=rope,
        cache=(cached[0], cached[1]) if latent else None, layer=layer, banded=latent,
        emit_k=not latent, name='win_' + tag)
    g_out, k_g = _gqa_call(
        z, b=b, t=t, tq=256, qname='qg', kname='kg', vname='vg',
        qn=wts['grid_qn'], kn=wts['grid_kn'], sink=None, rope=rope,
        cache=(cached[2], cached[3]) if latent else None, layer=layer, banded=False,
        emit_k=not latent, name='grid_' + tag)
    d_out, k_d = _diff_call(
        z, b=b, t=t, tq=256, qn=wts['diff_qn'], kn=wts['diff_kn'], lparams=wts['diff_lp'],
        out_g=wts['diff_out_g'], rope=rope, cache=(cached[4], cached[5]) if latent else None,
        layer=layer, emit_k=not latent, name='diff_' + tag)
    x = _merge_call(x, gates, (a_out, w_out, g_out, d_out), wts['w_branch'], wts['w_o'], mod, latent,
                    'merge_' + tag)
    x = _ffn_call(x, mod, wts['norm2_g'], wts['w_ff1'], wts['b_ff1'], wts['w_ff2'], wts['b_ff2'],
                  latent, 'ffn_' + tag)
    ctx_out = None
    if not latent:
        seg = lambda name, width: z[:, _COL[name]:_COL[name] + width]
        ctx_out = (k_w, seg('vw', 128), k_g, seg('vg', 128), k_d, seg('vd', 512), lru_final)
    return x, ctx_out


def kernel(x_prompt, x_sample, cache_win_k, cache_win_v, cache_grid_k, cache_grid_v, cache_diff_k, cache_diff_v, state_lru, c, c_ctx, w_ada, b_ada, norm1_g, norm2_g, w_in, conv_w, conv_b, lru_wr, lru_br, lru_wi, lru_bi, lru_lambda, win_qn, win_kn, win_sink, grid_qn, grid_kn, diff_qn, diff_kn, diff_lq1, diff_lk1, diff_lq2, diff_lk2, diff_out_g, w_branch, w_gate, b_gate, w_o, w_ff1, b_ff1, w_ff2, b_ff2):
    cond = jnp.zeros((MOD_ROWS, D_MODEL), F32).at[0].set(c_ctx).at[1:1 + DEC_BATCH].set(c)
    mod_all = _modulation(cond, w_ada, b_ada).reshape(DEPTH, MOD_ROWS, N_MOD, D_MODEL)
    rope = _rope_tables(DEC_SEQ)
    cached_all = (cache_win_k.reshape(DEC_BATCH, DEPTH, PAST_LEN, LANES),
                  cache_win_v.reshape(DEC_BATCH, DEPTH, PAST_LEN, LANES),
                  cache_grid_k.reshape(DEC_BATCH, DEPTH, PAST_LEN, LANES),
                  cache_grid_v.reshape(DEC_BATCH, DEPTH, PAST_LEN, LANES),
                  cache_diff_k.reshape(DEC_BATCH, DEPTH, PAST_LEN, 512),
                  cache_diff_v.reshape(DEC_BATCH, DEPTH, PAST_LEN, 512))
    perm = jnp.asarray(_PERM)
    y_p = x_prompt.reshape(BATCH * SEQ, D_MODEL)
    y_s = x_sample.reshape(DEC_BATCH * DEC_SEQ, D_MODEL)
    outs = [[] for _ in range(7)]
    for l in range(DEPTH):
        w_lru = jnp.stack([
            jnp.concatenate([_block_diag(lru_wr[l, k]), _block_diag(lru_wi[l, k])], axis=1)
            for k in range(2)]).astype(BF16)
        b_lru = jnp.concatenate([lru_br[l], lru_bi[l]], axis=-1).reshape(2, 1, 2 * LRU_W)
        wts = {
            'norm1_g': norm1_g[l], 'norm2_g': norm2_g[l],
            'w_in': w_in[l][:, perm].astype(BF16),
            'w_gate': w_gate[l].astype(BF16), 'b_gate': b_gate[l],
            'conv_w': conv_w[l], 'conv_b': conv_b[l], 'w_lru': w_lru, 'b_lru': b_lru,
            'lru_lambda': lru_lambda[l],
            'win_qn': win_qn[l], 'win_kn': win_kn[l], 'win_sink': win_sink[l],
            'grid_qn': grid_qn[l], 'grid_kn': grid_kn[l],
            'diff_qn': diff_qn[l], 'diff_kn': diff_kn[l],
            'diff_lp': jnp.stack([diff_lq1[l], diff_lk1[l], diff_lq2[l], diff_lk2[l]]),
            'diff_out_g': diff_out_g[l],
            'w_branch': w_branch[l].astype(BF16), 'w_o': w_o[l].astype(BF16),
            'w_ff1': w_ff1[l].astype(BF16), 'b_ff1': b_ff1[l],
            'w_ff2': w_ff2[l].astype(BF16), 'b_ff2': b_ff2[l],
        }
        y_p, ctx_l = _trunk_layer(y_p, mod_all[l], wts, l, False, None, None)
        for dst, val in zip(outs, ctx_l):
            dst.append(val)
        cached = cached_all + (state_lru[:, l],)
        y_s, _ = _trunk_layer(y_s, mod_all[l], wts, l, True, rope, cached)

    def stack(vals, shape):
        return jnp.stack([v.reshape((BATCH,) + shape) for v in vals], axis=1)

    return (y_p.reshape(BATCH, SEQ, D_MODEL),
            y_s.reshape(DEC_BATCH, DEC_SEQ, D_MODEL),
            stack(outs[0], (SEQ, WIN_KV, HEAD_DIM)),
            stack(outs[1], (SEQ, WIN_KV, HEAD_DIM)),
            stack(outs[2], (SEQ, WIN_KV, HEAD_DIM)),
            stack(outs[3], (SEQ, WIN_KV, HEAD_DIM)),
            stack(outs[4], (SEQ, DIFF_HEADS, 2, HEAD_DIM)),
            stack(outs[5], (SEQ, DIFF_HEADS, 2 * HEAD_DIM)),
            stack(outs[6], (2, LRU_W)))
```

```python
import functools
import math

import jax
import jax.numpy as jnp
import numpy as np
from jax import lax
from jax.experimental import pallas as pl
from jax.experimental.pallas import tpu as pltpu

F32 = jnp.float32
BF16 = jnp.bfloat16

D_MODEL = 2048
BATCH = 32
SEQ = 256
DEPTH = 4
DEC_BATCH = 4
DEC_SEQ = 4096
PAST_LEN = 512
GRID_W = 64
BLOCK = 128
HEAD_DIM = 64
N_FREQ = HEAD_DIM // 4
ROPE_BASE = 10000.0
ATTN_SCALE = HEAD_DIM ** -0.5
RMS_EPS = 1e-6
N_MOD = 6
N_BRANCH = 4
BRANCH_W = D_MODEL // 4
LRU_W = BRANCH_W
LRU_BLOCKS = 8
LRU_BW = LRU_W // LRU_BLOCKS
LRU_C = 8.0
CONV_W = 4
WIN_HEADS = 8
WIN_KV = 2
DIFF_HEADS = 4
D_FF = 4 * D_MODEL
D_IN = 4096

V7X_VMEM_BYTES = 64 * 1024 * 1024
VMEM_LIMIT = V7X_VMEM_BYTES - 8 * 1024 * 1024
LANES = 128
SUBLANES = 8
NEG = -1e30
LOG2E = math.log2(math.e)
ATTN_CHUNK = 1536
MOD_ROWS = 8

_ORIG = dict(xa=(0, 512), ya=(512, 512), qw=(1024, 512), kw=(1536, 128), vw=(1664, 128),
             qg=(1792, 512), kg=(2304, 128), vg=(2432, 128), qd=(2560, 512), kd=(3072, 512),
             vd=(3584, 512))
_ORDER = ('xa', 'ya', 'qw', 'qg', 'qd', 'kd', 'vd', 'kw', 'vw', 'kg', 'vg')
_COL = {}
_off = 0
for _n in _ORDER:
    _COL[_n] = _off
    _off += _ORIG[_n][1]
_PERM = np.concatenate([np.arange(_ORIG[n][0], _ORIG[n][0] + _ORIG[n][1]) for n in _ORDER])


def _cparams():
    return pltpu.CompilerParams(vmem_limit_bytes=VMEM_LIMIT)


def _dot(a, b):
    return jnp.dot(a, b, preferred_element_type=F32)


def _dot_nt(a, b):
    return lax.dot_general(a, b, (((1,), (1,)), ((), ())), preferred_element_type=F32)


def _mod_kernel(c_ref, w_ref, b_ref, o_ref):
    c = c_ref[...]
    s = (c * jax.nn.sigmoid(c)).astype(BF16)
    o_ref[0] = _dot(s, w_ref[0].astype(BF16)) + b_ref[0]


def _modulation(cond, w_ada, b_ada):
    tn = 1024
    n = N_MOD * D_MODEL
    return pl.pallas_call(
        _mod_kernel,
        grid=(DEPTH, n // tn),
        in_specs=[pl.BlockSpec((MOD_ROWS, D_MODEL), lambda l, j: (0, 0)),
                  pl.BlockSpec((1, D_MODEL, tn), lambda l, j: (l, 0, j)),
                  pl.BlockSpec((1, 1, tn), lambda l, j: (l, 0, j))],
        out_specs=pl.BlockSpec((1, MOD_ROWS, tn), lambda l, j: (l, 0, j)),
        out_shape=jax.ShapeDtypeStruct((DEPTH, MOD_ROWS, n), F32),
        compiler_params=_cparams(),
        name='modulation',
    )(cond, w_ada, b_ada.reshape(DEPTH, 1, n))


def _mod_index(latent, tm):
    if latent:
        per = DEC_SEQ // tm
        return lambda i, j: (1 + i // per, 0, 0)
    return lambda i, j: (0, 0, 0)


def _norm_mod(x, g, mod, shift_idx, scale_idx):
    var = jnp.mean(x * x, axis=-1, keepdims=True)
    y = x * lax.rsqrt(var + RMS_EPS) * g
    return y * (1.0 + mod[scale_idx:scale_idx + 1]) + mod[shift_idx:shift_idx + 1]


def _nm_kernel(*refs, has_bias, act):
    if has_bias:
        x_ref, mod_ref, g_ref, w_ref, b_ref, o_ref, h_scr = refs
    else:
        x_ref, mod_ref, g_ref, w_ref, o_ref, h_scr = refs

    @pl.when(pl.program_id(1) == 0)
    def _():
        h_scr[...] = _norm_mod(x_ref[...], g_ref[...], mod_ref[0], 0, 1).astype(BF16)

    acc = _dot(h_scr[...], w_ref[...])
    if has_bias:
        acc = acc + b_ref[...]
    if act == 'sigmoid':
        acc = jax.nn.sigmoid(acc)
    o_ref[...] = acc.astype(o_ref.dtype)


def _normmod_matmul(x, mod, g, w, bias, act, out_dtype, latent, name):
    m, n = x.shape[0], w.shape[1]
    tm, tn = 1024, 1024
    in_specs = [pl.BlockSpec((tm, D_MODEL), lambda i, j: (i, 0)),
                pl.BlockSpec((1, N_MOD, D_MODEL), _mod_index(latent, tm)),
                pl.BlockSpec((1, D_MODEL), lambda i, j: (0, 0)),
                pl.BlockSpec((D_MODEL, tn), lambda i, j: (0, j))]
    args = [x, mod, g.reshape(1, D_MODEL), w]
    if bias is not None:
        in_specs.append(pl.BlockSpec((1, tn), lambda i, j: (0, j)))
        args.append(bias.reshape(1, n))
    return pl.pallas_call(
        functools.partial(_nm_kernel, has_bias=bias is not None, act=act),
        grid=(m // tm, n // tn),
        in_specs=in_specs,
        out_specs=pl.BlockSpec((tm, tn), lambda i, j: (i, j)),
        out_shape=jax.ShapeDtypeStruct((m, n), out_dtype),
        scratch_shapes=[pltpu.VMEM((tm, D_MODEL), BF16)],
        compiler_params=_cparams(),
        name=name,
    )(*args)


def _lane_lo():
    return lax.broadcasted_iota(jnp.int32, (1, LANES), 1) < HEAD_DIM


def _seg_matrix():
    r = lax.broadcasted_iota(jnp.int32, (LANES, LANES), 0) // HEAD_DIM
    c = lax.broadcasted_iota(jnp.int32, (LANES, LANES), 1) // HEAD_DIM
    return jnp.where(r == c, 1.0, 0.0).astype(BF16)


def _head_rmsnorm(x, gain):
    x2 = x * x
    hi = x2.astype(BF16)
    lo = (x2 - hi.astype(F32)).astype(BF16)
    seg = _seg_matrix()
    ms = (_dot(hi, seg) + _dot(lo, seg)) * (1.0 / HEAD_DIM)
    return x * lax.rsqrt(ms + RMS_EPS) * gain


def _rope(x, cos, sin_signed):
    lane = lax.broadcasted_iota(jnp.int32, (1, LANES), 1)
    first = (lane & (2 * N_FREQ - 1)) < N_FREQ
    up = pltpu.roll(x, LANES - N_FREQ, 1)
    dn = pltpu.roll(x, N_FREQ, 1)
    return x * cos + jnp.where(first, up, dn) * sin_signed


def _rope_tables(t):
    pos = jnp.arange(t)
    row = (pos // GRID_W).astype(F32)
    col = (pos % GRID_W).astype(F32)
    inv = ROPE_BASE ** (-jnp.arange(N_FREQ, dtype=F32) / N_FREQ)
    ar, ac = row[:, None] * inv, col[:, None] * inv
    cos = jnp.concatenate([jnp.cos(ar), jnp.cos(ar), jnp.cos(ac), jnp.cos(ac)], axis=-1)
    sin = jnp.concatenate([-jnp.sin(ar), jnp.sin(ar), -jnp.sin(ac), jnp.sin(ac)], axis=-1)
    return jnp.tile(cos, (1, 2)), jnp.tile(sin, (1, 2))


def _gqa_kernel(*refs, t, tq, n_ctx, banded, has_sink, has_rope, emit_k, ck):
    it = iter(refs)
    q_ref, k_ref, v_ref, qn_ref, kn_ref = (next(it) for _ in range(5))
    if has_rope:
        cq_ref, sq_ref, cka_ref, ska_ref = (next(it) for _ in range(4))
    if n_ctx:
        kc_ref, vc_ref = next(it), next(it)
    if has_sink:
        sink_ref = next(it)
    o_ref = next(it)
    if emit_k:
        ko_ref = next(it)
    k_scr, v_scr = next(it), next(it)

    g = pl.program_id(1)
    qi = pl.program_id(2)
    lo = _lane_lo()
    lat0 = BLOCK if banded else 0
    ctx0 = t + 2 * lat0
    first_head = g == 0

    def put(dst, kx, vx):
        kr = pltpu.roll(kx, HEAD_DIM, 1)
        vr = pltpu.roll(vx, HEAD_DIM, 1)
        zero = jnp.zeros_like(kx)
        k_scr[0, dst, :] = jnp.where(lo, jnp.where(first_head, kx, kr), zero).astype(BF16)
        k_scr[1, dst, :] = jnp.where(lo, zero, jnp.where(first_head, kr, kx)).astype(BF16)
        v_scr[0, dst, :] = jnp.where(lo, jnp.where(first_head, vx, vr), zero).astype(BF16)
        v_scr[1, dst, :] = jnp.where(lo, zero, jnp.where(first_head, vr, vx)).astype(BF16)

    @pl.when(qi == 0)
    def _build():
        step = min(t, 512)
        for r in range(0, t, step):
            kx = _head_rmsnorm(k_ref[r:r + step, :], kn_ref[...])
            if emit_k:
                ko_ref[r:r + step, :] = kx
            if has_rope:
                kx = _rope(kx, cka_ref[r:r + step, :], ska_ref[r:r + step, :])
            put(slice(lat0 + r, lat0 + r + step), kx, v_ref[r:r + step, :])
        if banded:
            zpad = jnp.zeros((BLOCK, LANES), BF16)
            for scr in (k_scr, v_scr):
                for var in range(2):
                    scr[var, 0:BLOCK, :] = zpad
                    scr[var, lat0 + t:lat0 + t + BLOCK, :] = zpad
        if n_ctx:
            put(slice(ctx0, ctx0 + n_ctx), kc_ref[0, 0], vc_ref[0, 0])

    def step_fn(q2, start, size, mask, carry):
        m0, m1, l, acc = carry
        k0 = k_scr[0, pl.ds(start, size), :]
        k1 = k_scr[1, pl.ds(start, size), :]
        s0 = _dot_nt(q2, k0)
        s1 = _dot_nt(q2, k1)
        if mask is not None:
            s0 = jnp.where(mask, s0, NEG)
            s1 = jnp.where(mask, s1, NEG)
        n0 = jnp.maximum(m0, jnp.max(s0, axis=-1, keepdims=True))
        n1 = jnp.maximum(m1, jnp.max(s1, axis=-1, keepdims=True))
        p0 = jnp.exp2(s0 - n0)
        p1 = jnp.exp2(s1 - n1)
        alpha = jnp.where(lo, jnp.exp2(m0 - n0), jnp.exp2(m1 - n1))
        rs = jnp.where(lo, jnp.sum(p0, axis=-1, keepdims=True), jnp.sum(p1, axis=-1, keepdims=True))
        l = alpha * l + rs
        acc = (alpha * acc + _dot(p0.astype(BF16), v_scr[0, pl.ds(start, size), :])
               + _dot(p1.astype(BF16), v_scr[1, pl.ds(start, size), :]))
        return n0, n1, l, acc

    q = q_ref[...]
    for p in range(2):
        qp = _head_rmsnorm(q[:, p * LANES:(p + 1) * LANES], qn_ref[...])
        if has_rope:
            qp = _rope(qp, cq_ref[...], sq_ref[...])
        q2 = (qp * (ATTN_SCALE * LOG2E)).astype(BF16)
        if has_sink:
            h0 = g * 4 + 2 * p
            m0 = jnp.full((tq, 1), sink_ref[h0] * LOG2E, F32)
            m1 = jnp.full((tq, 1), sink_ref[h0 + 1] * LOG2E, F32)
            l = jnp.ones((tq, LANES), F32)
        else:
            m0 = jnp.full((tq, 1), NEG, F32)
            m1 = jnp.full((tq, 1), NEG, F32)
            l = jnp.zeros((tq, LANES), F32)
        carry = (m0, m1, l, jnp.zeros((tq, LANES), F32))
        if banded:
            span = tq + 2 * BLOCK
            rr = lax.broadcasted_iota(jnp.int32, (tq, span), 0)
            cc = lax.broadcasted_iota(jnp.int32, (tq, span), 1)
            kpos = qi * tq - BLOCK + cc
            mask = (cc >= rr) & (cc - rr <= 2 * BLOCK) & (kpos >= 0) & (kpos < t)
            carry = step_fn(q2, pl.multiple_of(qi * tq, tq), span, mask, carry)
            if n_ctx:
                carry = step_fn(q2, ctx0, n_ctx, None, carry)
        else:
            for c in range((t + n_ctx) // ck):
                carry = step_fn(q2, c * ck, ck, None, carry)
        _, _, l, acc = carry
        o_ref[:, p * LANES:(p + 1) * LANES] = (acc / l).astype(o_ref.dtype)


def _gqa_call(z, *, b, t, tq, qname, kname, vname, qn, kn, sink, rope, cache, layer, banded,
              emit_k, name):
    nq = t // tq
    n_ctx = PAST_LEN if cache is not None else 0
    rows = t + n_ctx + (2 * BLOCK if banded else 0)
    ck = min(ATTN_CHUNK, t + n_ctx)
    qblk, kblk, vblk = _COL[qname] // 256, _COL[kname] // LANES, _COL[vname] // LANES
    const = lambda bb, g, i: (0, 0)
    in_specs = [pl.BlockSpec((tq, 256), lambda bb, g, i: (bb * nq + i, qblk + g)),
                pl.BlockSpec((t, LANES), lambda bb, g, i: (bb, kblk)),
                pl.BlockSpec((t, LANES), lambda bb, g, i: (bb, vblk)),
                pl.BlockSpec((1, LANES), const),
                pl.BlockSpec((1, LANES), const)]
    args = [z, z, z, jnp.tile(qn, 2).reshape(1, LANES), jnp.tile(kn, 2).reshape(1, LANES)]
    if rope is not None:
        cos, sin = rope
        in_specs += [pl.BlockSpec((tq, LANES), lambda bb, g, i: (i, 0)),
                     pl.BlockSpec((tq, LANES), lambda bb, g, i: (i, 0)),
                     pl.BlockSpec((t, LANES), const),
                     pl.BlockSpec((t, LANES), const)]
        args += [cos, sin, cos, sin]
    if cache is not None:
        spec = pl.BlockSpec((1, 1, n_ctx, LANES), lambda bb, g, i: (bb, layer, 0, 0))
        in_specs += [spec, spec]
        args += [cache[0], cache[1]]
    if sink is not None:
        in_specs.append(pl.BlockSpec(memory_space=pltpu.SMEM))
        args.append(sink)
    out_shape = [jax.ShapeDtypeStruct((b * t, 512), BF16)]
    out_specs = [pl.BlockSpec((tq, 256), lambda bb, g, i: (bb * nq + i, g))]
    if emit_k:
        out_shape.append(jax.ShapeDtypeStruct((b * t, LANES), F32))
        out_specs.append(pl.BlockSpec((t, LANES), lambda bb, g, i: (bb, 0)))
    res = pl.pallas_call(
        functools.partial(_gqa_kernel, t=t, tq=tq, n_ctx=n_ctx, banded=banded,
                          has_sink=sink is not None, has_rope=rope is not None, emit_k=emit_k, ck=ck),
        grid=(b, 2, nq),
        in_specs=in_specs,
        out_specs=out_specs,
        out_shape=out_shape,
        scratch_shapes=[pltpu.VMEM((2, rows, LANES), BF16), pltpu.VMEM((2, rows, LANES), BF16)],
        compiler_params=_cparams(),
        name=name,
    )(*args)
    return res if emit_k else (res[0], None)


def _diff_kernel(*refs, t, tq, n_ctx, has_rope, emit_k, ck, lam_init):
    it = iter(refs)
    q_ref, k_ref, v_ref, qn_ref, kn_ref, lp_ref, og_ref = (next(it) for _ in range(7))
    if has_rope:
        cq_ref, sq_ref, cka_ref, ska_ref = (next(it) for _ in range(4))
    if n_ctx:
        kc_ref, vc_ref = next(it), next(it)
    o_ref = next(it)
    if emit_k:
        ko_ref = next(it)
    k_scr, v_scr = next(it), next(it)

    qi = pl.program_id(2)
    lo = _lane_lo()

    def put(dst, kx, vx):
        zero = jnp.zeros_like(kx)
        k_scr[0, dst, :] = jnp.where(lo, kx, zero).astype(BF16)
        k_scr[1, dst, :] = jnp.where(lo, zero, kx).astype(BF16)
        v_scr[dst, :] = vx.astype(BF16)

    @pl.when(qi == 0)
    def _build():
        step = min(t, 512)
        for r in range(0, t, step):
            kx = _head_rmsnorm(k_ref[r:r + step, :], kn_ref[...])
            if emit_k:
                ko_ref[r:r + step, :] = kx
            if has_rope:
                kx = _rope(kx, cka_ref[r:r + step, :], ska_ref[r:r + step, :])
            put(slice(r, r + step), kx, v_ref[r:r + step, :])
        if n_ctx:
            put(slice(t, t + n_ctx), kc_ref[0, 0], vc_ref[0, 0])

    qp = _head_rmsnorm(q_ref[...], qn_ref[...])
    if has_rope:
        qp = _rope(qp, cq_ref[...], sq_ref[...])
    q2 = (qp * (ATTN_SCALE * LOG2E)).astype(BF16)

    def step_fn(start, carry):
        m0, m1, l0, l1, a0, a1 = carry
        v = v_scr[pl.ds(start, ck), :]
        s0 = _dot_nt(q2, k_scr[0, pl.ds(start, ck), :])
        s1 = _dot_nt(q2, k_scr[1, pl.ds(start, ck), :])
        n0 = jnp.maximum(m0, jnp.max(s0, axis=-1, keepdims=True))
        n1 = jnp.maximum(m1, jnp.max(s1, axis=-1, keepdims=True))
        p0 = jnp.exp2(s0 - n0)
        p1 = jnp.exp2(s1 - n1)
        e0 = jnp.exp2(m0 - n0)
        e1 = jnp.exp2(m1 - n1)
        l0 = e0 * l0 + jnp.sum(p0, axis=-1, keepdims=True)
        l1 = e1 * l1 + jnp.sum(p1, axis=-1, keepdims=True)
        a0 = e0 * a0 + _dot(p0.astype(BF16), v)
        a1 = e1 * a1 + _dot(p1.astype(BF16), v)
        return n0, n1, l0, l1, a0, a1

    col = lambda val: jnp.full((tq, 1), val, F32)
    carry = (col(NEG), col(NEG), col(0.0), col(0.0),
             jnp.zeros((tq, LANES), F32), jnp.zeros((tq, LANES), F32))
    for c in range((t + n_ctx) // ck):
        carry = step_fn(c * ck, carry)
    _, _, l0, l1, a0, a1 = carry

    lp = lp_ref[...]
    lam = (jnp.exp(jnp.sum(lp[0:1] * lp[1:2], axis=-1, keepdims=True))
           - jnp.exp(jnp.sum(lp[2:3] * lp[3:4], axis=-1, keepdims=True)) + lam_init)
    o = a0 / l0 - lam * (a1 / l1)
    var = jnp.mean(o * o, axis=-1, keepdims=True)
    o = o * lax.rsqrt(var + RMS_EPS) * og_ref[...] * (1.0 - lam_init)
    o_ref[...] = o.astype(o_ref.dtype)


def _diff_call(z, *, b, t, tq, qn, kn, lparams, out_g, rope, cache, layer, emit_k, name):
    nq = t // tq
    n_ctx = PAST_LEN if cache is not None else 0
    ck = min(ATTN_CHUNK, t + n_ctx)
    qblk, kblk, vblk = _COL['qd'] // LANES, _COL['kd'] // LANES, _COL['vd'] // LANES
    const = lambda bb, h, i: (0, 0)
    in_specs = [pl.BlockSpec((tq, LANES), lambda bb, h, i: (bb * nq + i, qblk + h)),
                pl.BlockSpec((t, LANES), lambda bb, h, i: (bb, kblk + h)),
                pl.BlockSpec((t, LANES), lambda bb, h, i: (bb, vblk + h)),
                pl.BlockSpec((1, LANES), const),
                pl.BlockSpec((1, LANES), const),
                pl.BlockSpec((4, HEAD_DIM), const),
                pl.BlockSpec((1, LANES), const)]
    args = [z, z, z, jnp.tile(qn, 2).reshape(1, LANES), jnp.tile(kn, 2).reshape(1, LANES),
            lparams, out_g.reshape(1, LANES)]
    if rope is not None:
        cos, sin = rope
        in_specs += [pl.BlockSpec((tq, LANES), lambda bb, h, i: (i, 0)),
                     pl.BlockSpec((tq, LANES), lambda bb, h, i: (i, 0)),
                     pl.BlockSpec((t, LANES), const),
                     pl.BlockSpec((t, LANES), const)]
        args += [cos, sin, cos, sin]
    if cache is not None:
        spec = pl.BlockSpec((1, 1, n_ctx, LANES), lambda bb, h, i: (bb, layer, 0, h))
        in_specs += [spec, spec]
        args += [cache[0], cache[1]]
    out_shape = [jax.ShapeDtypeStruct((b * t, 512), BF16)]
    out_specs = [pl.BlockSpec((tq, LANES), lambda bb, h, i: (bb * nq + i, h))]
    if emit_k:
        out_shape.append(jax.ShapeDtypeStruct((b * t, 512), F32))
        out_specs.append(pl.BlockSpec((t, LANES), lambda bb, h, i: (bb, h)))
    lam_init = 0.8 - 0.6 * math.exp(-0.3 * layer)
    res = pl.pallas_call(
        functools.partial(_diff_kernel, t=t, tq=tq, n_ctx=n_ctx, has_rope=rope is not None,
                          emit_k=emit_k, ck=ck, lam_init=lam_init),
        grid=(b, DIFF_HEADS, nq),
        in_specs=in_specs,
        out_specs=out_specs,
        out_shape=out_shape,
        scratch_shapes=[pltpu.VMEM((2, t + n_ctx, LANES), BF16), pltpu.VMEM((t + n_ctx, LANES), BF16)],
        compiler_params=_cparams(),
        name=name,
    )(*args)
    return res if emit_k else (res[0], None)


def _softplus(x):
    return jnp.maximum(x, 0.0) + jnp.log1p(jnp.exp(-jnp.abs(x)))


def _lru_kernel(*refs, tc, nc, has_h0, emit_final):
    it = iter(refs)
    xp_ref, xm_ref, xn_ref, ya_ref, cw_ref, cb_ref, wl_ref, bl_ref, lam_ref = (next(it) for _ in range(9))
    if has_h0:
        h0_ref = next(it)
    o_ref = next(it)
    if emit_final:
        fin_ref = next(it)
    hf_scr, a_scr, b_scr, carry_scr = (next(it) for _ in range(4))

    s = pl.program_id(1)
    fwd = s < nc
    c = jnp.where(fwd, s, 2 * nc - 1 - s)
    t0 = pl.multiple_of(c * tc, tc)

    @pl.when(s == 0)
    def _():
        if has_h0:
            carry_scr[...] = h0_ref[0]
        else:
            carry_scr[...] = jnp.zeros_like(carry_scr)

    prev = jnp.where(c > 0, xp_ref[...], 0.0)
    nxt = jnp.where(c < nc - 1, xn_ref[...], 0.0)
    ext = jnp.concatenate([prev, xm_ref[...], nxt], axis=0)
    n_ext = tc + 2 * SUBLANES
    u = cb_ref[...] + cw_ref[1:2] * xm_ref[...]
    u = u + cw_ref[0:1] * pltpu.roll(ext, 1, 0)[SUBLANES:SUBLANES + tc]
    u = u + cw_ref[2:3] * pltpu.roll(ext, n_ext - 1, 0)[SUBLANES:SUBLANES + tc]
    u = u + cw_ref[3:4] * pltpu.roll(ext, n_ext - 2, 0)[SUBLANES:SUBLANES + tc]

    gates = _dot(u.astype(BF16), wl_ref[0]) + bl_ref[0]
    r = jax.nn.sigmoid(gates[:, :LRU_W])
    ig = jax.nn.sigmoid(gates[:, LRU_W:])
    log_a = -LRU_C * r * _softplus(-lam_ref[0])
    a = jnp.exp(log_a)
    a_scr[...] = a
    b_scr[...] = jnp.sqrt(-jnp.tanh(log_a) * (a * a + 1.0)) * ig * u

    ntile = tc // SUBLANES
    row = lax.broadcasted_iota(jnp.int32, (SUBLANES, LRU_W), 0)

    def scan(forward):
        def tile(i, carry):
            j = i if forward else ntile - 1 - i
            r0 = pl.multiple_of(j * SUBLANES, SUBLANES)
            a = a_scr[pl.ds(r0, SUBLANES), :]
            bv = b_scr[pl.ds(r0, SUBLANES), :]
            for d in (1, 2, 4):
                shift = d if forward else SUBLANES - d
                msk = (row >= d) if forward else (row < SUBLANES - d)
                ap = pltpu.roll(a, shift, 0)
                bp = pltpu.roll(bv, shift, 0)
                bv = jnp.where(msk, a * bp + bv, bv)
                a = jnp.where(msk, a * ap, a)
            h = a * carry + bv
            g0 = pl.multiple_of(t0 + r0, SUBLANES)
            if forward:
                hf_scr[pl.ds(g0, SUBLANES), :] = h
                return h[SUBLANES - 1:SUBLANES, :]
            b_scr[pl.ds(r0, SUBLANES), :] = h
            return h[0:1, :]

        idx = 0 if forward else 1
        last = lax.fori_loop(0, ntile, tile, carry_scr[idx:idx + 1, :])
        carry_scr[idx:idx + 1, :] = last
        if emit_final:
            fin_ref[0, idx:idx + 1, :] = last
        if not forward:
            o_ref[...] = ((hf_scr[pl.ds(t0, tc), :] + b_scr[...])
                          * jax.nn.gelu(ya_ref[...])).astype(o_ref.dtype)

    pl.when(fwd)(lambda: scan(True))
    pl.when(jnp.logical_not(fwd))(lambda: scan(False))


def _lru_call(z, *, b, t, conv_w, conv_b, w_lru, b_lru, lam, h0, emit_final, name):
    tc = min(t, 512)
    nc = t // tc
    per8 = tc // SUBLANES
    nrow8 = b * t // SUBLANES

    def chunk(s):
        return jnp.where(s < nc, s, 2 * nc - 1 - s)

    def hold(s):
        return jnp.where(s < nc, nc - 1, 2 * nc - 1 - s)

    in_specs = [
        pl.BlockSpec((SUBLANES, LRU_W),
                     lambda bb, s: (jnp.maximum((bb * nc + chunk(s)) * per8 - 1, 0), 0)),
        pl.BlockSpec((tc, LRU_W), lambda bb, s: (bb * nc + chunk(s), 0)),
        pl.BlockSpec((SUBLANES, LRU_W),
                     lambda bb, s: (jnp.minimum((bb * nc + chunk(s) + 1) * per8, nrow8 - 1), 0)),
        pl.BlockSpec((tc, LRU_W), lambda bb, s: (bb * nc + hold(s), 1)),
        pl.BlockSpec((CONV_W, LRU_W), lambda bb, s: (0, 0)),
        pl.BlockSpec((1, LRU_W), lambda bb, s: (0, 0)),
        pl.BlockSpec((1, LRU_W, 2 * LRU_W), lambda bb, s: (s // nc, 0, 0)),
        pl.BlockSpec((1, 1, 2 * LRU_W), lambda bb, s: (s // nc, 0, 0)),
        pl.BlockSpec((1, 1, LRU_W), lambda bb, s: (s // nc, 0, 0)),
    ]
    args = [z, z, z, z, conv_w, conv_b.reshape(1, LRU_W), w_lru, b_lru, lam.reshape(2, 1, LRU_W)]
    if h0 is not None:
        in_specs.append(pl.BlockSpec((1, 2, LRU_W), lambda bb, s: (bb, 0, 0)))
        args.append(h0)
    out_shape = [jax.ShapeDtypeStruct((b * t, LRU_W), BF16)]
    out_specs = [pl.BlockSpec((tc, LRU_W), lambda bb, s: (bb * nc + hold(s), 0))]
    if emit_final:
        out_shape.append(jax.ShapeDtypeStruct((b, 2, LRU_W), F32))
        out_specs.append(pl.BlockSpec((1, 2, LRU_W), lambda bb, s: (bb, 0, 0)))
    res = pl.pallas_call(
        functools.partial(_lru_kernel, tc=tc, nc=nc, has_h0=h0 is not None, emit_final=emit_final),
        grid=(b, 2 * nc),
        in_specs=in_specs,
        out_specs=out_specs,
        out_shape=out_shape,
        scratch_shapes=[pltpu.VMEM((t, LRU_W), F32), pltpu.VMEM((tc, LRU_W), F32),
                        pltpu.VMEM((tc, LRU_W), F32), pltpu.VMEM((2, LRU_W), F32)],
        compiler_params=_cparams(),
        name=name,
    )(*args)
    return res if emit_final else (res[0], None)


def _merge_kernel(x_ref, gt_ref, a_ref, w_ref, g_ref, d_ref, wb_ref, wo_ref, mod_ref, o_ref, acc_scr):
    bidx = pl.program_id(1)

    def contrib(br_ref):
        return gt_ref[...].astype(F32) * _dot(br_ref[...], wb_ref[0])

    @pl.when(bidx == 0)
    def _():
        acc_scr[...] = contrib(a_ref)

    @pl.when(bidx == 1)
    def _():
        acc_scr[...] += contrib(w_ref)

    @pl.when(bidx == 2)
    def _():
        acc_scr[...] += contrib(g_ref)

    @pl.when(bidx == 3)
    def _():
        merged = (acc_scr[...] + contrib(d_ref)).astype(BF16)
        o_ref[...] = x_ref[...] + mod_ref[0][2:3] * _dot(merged, wo_ref[...])


def _merge_call(x, gates, branches, w_branch, w_o, mod, latent, name):
    m = x.shape[0]
    tm = 512
    row = lambda i, j: (i, 0)
    in_specs = [pl.BlockSpec((tm, D_MODEL), row),
                pl.BlockSpec((tm, D_MODEL), lambda i, j: (i, j))]
    in_specs += [pl.BlockSpec((tm, BRANCH_W), row)] * N_BRANCH
    in_specs += [pl.BlockSpec((1, BRANCH_W, D_MODEL), lambda i, j: (j, 0, 0)),
                 pl.BlockSpec((D_MODEL, D_MODEL), lambda i, j: (0, 0)),
                 pl.BlockSpec((1, N_MOD, D_MODEL), _mod_index(latent, tm))]
    return pl.pallas_call(
        _merge_kernel,
        grid=(m // tm, N_BRANCH),
        in_specs=in_specs,
        out_specs=pl.BlockSpec((tm, D_MODEL), row),
        out_shape=jax.ShapeDtypeStruct((m, D_MODEL), F32),
        scratch_shapes=[pltpu.VMEM((tm, D_MODEL), F32)],
        compiler_params=_cparams(),
        name=name,
    )(x, gates, *branches, w_branch, w_o, mod)


def _ffn_kernel(x_ref, mod_ref, g_ref, w1_ref, b1_ref, w2_ref, b2_ref, o_ref, h_scr, acc_scr):
    j = pl.program_id(1)

    @pl.when(j == 0)
    def _():
        h_scr[...] = _norm_mod(x_ref[...], g_ref[...], mod_ref[0], 3, 4).astype(BF16)
        acc_scr[...] = jnp.zeros_like(acc_scr)

    a = jnp.maximum(_dot(h_scr[...], w1_ref[...]) + b1_ref[...], 0.0)
    acc_scr[...] += _dot((a * a).astype(BF16), w2_ref[...])

    @pl.when(j == pl.num_programs(1) - 1)
    def _():
        o_ref[...] = x_ref[...] + mod_ref[0][5:6] * (acc_scr[...] + b2_ref[...])


def _ffn_call(x, mod, g, w1, b1, w2, b2, latent, name):
    m = x.shape[0]
    tm, tf = 512, 1024
    return pl.pallas_call(
        _ffn_kernel,
        grid=(m // tm, D_FF // tf),
        in_specs=[pl.BlockSpec((tm, D_MODEL), lambda i, j: (i, 0)),
                  pl.BlockSpec((1, N_MOD, D_MODEL), _mod_index(latent, tm)),
                  pl.BlockSpec((1, D_MODEL), lambda i, j: (0, 0)),
                  pl.BlockSpec((D_MODEL, tf), lambda i, j: (0, j)),
                  pl.BlockSpec((1, tf), lambda i, j: (0, j)),
                  pl.BlockSpec((tf, D_MODEL), lambda i, j: (j, 0)),
                  pl.BlockSpec((1, D_MODEL), lambda i, j: (0, 0))],
        out_specs=pl.BlockSpec((tm, D_MODEL), lambda i, j: (i, 0)),
        out_shape=jax.ShapeDtypeStruct((m, D_MODEL), F32),
        scratch_shapes=[pltpu.VMEM((tm, D_MODEL), BF16), pltpu.VMEM((tm, D_MODEL), F32)],
        compiler_params=_cparams(),
        name=name,
    )(x, mod, g.reshape(1, D_MODEL), w1, b1.reshape(1, D_FF), w2, b2.reshape(1, D_MODEL))


def _block_diag(w):
    eye = jnp.eye(LRU_BLOCKS, dtype=w.dtype)
    return jnp.einsum('ncd,nm->ncmd', w, eye).reshape(LRU_W, LRU_W)


def _trunk_layer(x, mod, wts, layer, latent, rope, cached):
    b, t = (DEC_BATCH, DEC_SEQ) if latent else (BATCH, SEQ)
    tag = ('lat' if latent else 'ctx') + str(layer)
    z = _normmod_matmul(x, mod, wts['norm1_g'], wts['w_in'], None, None, F32, latent, 'inproj_' + tag)
    gates = _normmod_matmul(x, mod, wts['norm1_g'], wts['w_gate'], wts['b_gate'], 'sigmoid', BF16,
                            latent, 'gates_' + tag)
    a_out, lru_final = _lru_call(
        z, b=b, t=t, conv_w=wts['conv_w'], conv_b=wts['conv_b'], w_lru=wts['w_lru'], b_lru=wts['b_lru'],
        lam=wts['lru_lambda'], h0=cached[6] if latent else None, emit_final=not latent, name='lru_' + tag)
    w_out, k_w = _gqa_call(
        z, b=b, t=t, tq=256, qname='qw', kname='kw', vname='vw',
        qn=wts['win_qn'], kn=wts['win_kn'], sink=wts['win_sink'], rope=rope,
        cache=(cached[0], cached[1]) if latent else None, layer=layer, banded=latent,
        emit_k=not latent, name='win_' + tag)
    g_out, k_g = _gqa_call(
        z, b=b, t=t, tq=256, qname='qg', kname='kg', vname='vg',
        qn=wts['grid_qn'], kn=wts['grid_kn'], sink=None, rope=rope,
        cache=(cached[2], cached[3]) if latent else None, layer=layer, banded=False,
        emit_k=not latent, name='grid_' + tag)
    d_out, k_d = _diff_call(
        z, b=b, t=t, tq=256, qn=wts['diff_qn'], kn=wts['diff_kn'], lparams=wts['diff_lp'],
        out_g=wts['diff_out_g'], rope=rope, cache=(cached[4], cached[5]) if latent else None,
        layer=layer, emit_k=not latent, name='diff_' + tag)
    x = _merge_call(x, gates, (a_out, w_out, g_out, d_out), wts['w_branch'], wts['w_o'], mod, latent,
                    'merge_' + tag)
    x = _ffn_call(x, mod, wts['norm2_g'], wts['w_ff1'], wts['b_ff1'], wts['w_ff2'], wts['b_ff2'],
                  latent, 'ffn_' + tag)
    ctx_out = None
    if not latent:
        seg = lambda name, width: z[:, _COL[name]:_COL[name] + width]
        ctx_out = (k_w, seg('vw', 128), k_g, seg('vg', 128), k_d, seg('vd', 512), lru_final)
    return x, ctx_out


def kernel(x_prompt, x_sample, cache_win_k, cache_win_v, cache_grid_k, cache_grid_v, cache_diff_k, cache_diff_v, state_lru, c, c_ctx, w_ada, b_ada, norm1_g, norm2_g, w_in, conv_w, conv_b, lru_wr, lru_br, lru_wi, lru_bi, lru_lambda, win_qn, win_kn, win_sink, grid_qn, grid_kn, diff_qn, diff_kn, diff_lq1, diff_lk1, diff_lq2, diff_lk2, diff_out_g, w_branch, w_gate, b_gate, w_o, w_ff1, b_ff1, w_ff2, b_ff2):
    cond = jnp.zeros((MOD_ROWS, D_MODEL), F32).at[0].set(c_ctx).at[1:1 + DEC_BATCH].set(c)
    mod_all = _modulation(cond, w_ada, b_ada).reshape(DEPTH, MOD_ROWS, N_MOD, D_MODEL)
    rope = _rope_tables(DEC_SEQ)
    cached_all = (cache_win_k.reshape(DEC_BATCH, DEPTH, PAST_LEN, LANES),
                  cache_win_v.reshape(DEC_BATCH, DEPTH, PAST_LEN, LANES),
                  cache_grid_k.reshape(DEC_BATCH, DEPTH, PAST_LEN, LANES),
                  cache_grid_v.reshape(DEC_BATCH, DEPTH, PAST_LEN, LANES),
                  cache_diff_k.reshape(DEC_BATCH, DEPTH, PAST_LEN, 512),
                  cache_diff_v.reshape(DEC_BATCH, DEPTH, PAST_LEN, 512))
    perm = jnp.asarray(_PERM)
    y_p = x_prompt.reshape(BATCH * SEQ, D_MODEL)
    y_s = x_sample.reshape(DEC_BATCH * DEC_SEQ, D_MODEL)
    outs = [[] for _ in range(7)]
    for l in range(DEPTH):
        w_lru = jnp.stack([
            jnp.concatenate([_block_diag(lru_wr[l, k]), _block_diag(lru_wi[l, k])], axis=1)
            for k in range(2)]).astype(BF16)
        b_lru = jnp.concatenate([lru_br[l], lru_bi[l]], axis=-1).reshape(2, 1, 2 * LRU_W)
        wts = {
            'norm1_g': norm1_g[l], 'norm2_g': norm2_g[l],
            'w_in': w_in[l][:, perm].astype(BF16),
            'w_gate': w_gate[l].astype(BF16), 'b_gate': b_gate[l],
            'conv_w': conv_w[l], 'conv_b': conv_b[l], 'w_lru': w_lru, 'b_lru': b_lru,
            'lru_lambda': lru_lambda[l],
            'win_qn': win_qn[l], 'win_kn': win_kn[l], 'win_sink': win_sink[l],
            'grid_qn': grid_qn[l], 'grid_kn': grid_kn[l],
            'diff_qn': diff_qn[l], 'diff_kn': diff_kn[l],
            'diff_lp': jnp.stack([diff_lq1[l], diff_lk1[l], diff_lq2[l], diff_lk2[l]]),
            'diff_out_g': diff_out_g[l],
            'w_branch': w_branch[l].astype(BF16), 'w_o': w_o[l].astype(BF16),
            'w_ff1': w_ff1[l].astype(BF16), 'b_ff1': b_ff1[l],
            'w_ff2': w_ff2[l].astype(BF16), 'b_ff2': b_ff2[l],
        }
        y_p, ctx_l = _trunk_layer(y_p, mod_all[l], wts, l, False, None, None)
        for dst, val in zip(outs, ctx_l):
            dst.append(val)
        cached = cached_all + (state_lru[:, l],)
        y_s, _ = _trunk_layer(y_s, mod_all[l], wts, l, True, rope, cached)

    def stack(vals, shape):
        return jnp.stack([v.reshape((BATCH,) + shape) for v in vals], axis=1)

    return (y_p.reshape(BATCH, SEQ, D_MODEL),
            y_s.reshape(DEC_BATCH, DEC_SEQ, D_MODEL),
            stack(outs[0], (SEQ, WIN_KV, HEAD_DIM)),
            stack(outs[1], (SEQ, WIN_KV, HEAD_DIM)),
            stack(outs[2], (SEQ, WIN_KV, HEAD_DIM)),
            stack(outs[3], (SEQ, WIN_KV, HEAD_DIM)),
            stack(outs[4], (SEQ, DIFF_HEADS, 2, HEAD_DIM)),
            stack(outs[5], (SEQ, DIFF_HEADS, 2 * HEAD_DIM)),
            stack(outs[6], (2, LRU_W)))
```

```python
import functools
import math

import jax
import jax.numpy as jnp
import numpy as np
from jax import lax
from jax.experimental import pallas as pl
from jax.experimental.pallas import tpu as pltpu

F32 = jnp.float32
BF16 = jnp.bfloat16

D_MODEL = 2048
BATCH = 32
SEQ = 256
DEPTH = 4
DEC_BATCH = 4
DEC_SEQ = 4096
PAST_LEN = 512
GRID_W = 64
BLOCK = 128
HEAD_DIM = 64
N_FREQ = HEAD_DIM // 4
ROPE_BASE = 10000.0
ATTN_SCALE = HEAD_DIM ** -0.5
RMS_EPS = 1e-6
N_MOD = 6
N_BRANCH = 4
BRANCH_W = D_MODEL // 4
LRU_W = BRANCH_W
LRU_BLOCKS = 8
LRU_BW = LRU_W // LRU_BLOCKS
LRU_C = 8.0
CONV_W = 4
WIN_HEADS = 8
WIN_KV = 2
DIFF_HEADS = 4
D_FF = 4 * D_MODEL
D_IN = 4096

V7X_VMEM_BYTES = 64 * 1024 * 1024
VMEM_LIMIT = V7X_VMEM_BYTES - 8 * 1024 * 1024
LANES = 128
SUBLANES = 8
NEG = -1e30
LOG2E = math.log2(math.e)
ATTN_CHUNK = 1536
MOD_ROWS = 8

_COL = dict(xa=0, ya=512, qw=1024, kw=1536, vw=1664, qg=1792, kg=2304, vg=2432, qd=2560, kd=3072,
            vd=3584)


def _cparams():
    return pltpu.CompilerParams(vmem_limit_bytes=VMEM_LIMIT)


def _dot(a, b):
    return jnp.dot(a, b, preferred_element_type=F32)


def _dot_nt(a, b):
    return lax.dot_general(a, b, (((1,), (1,)), ((), ())), preferred_element_type=F32)


def _mod_kernel(c_ref, w_ref, b_ref, o_ref):
    c = c_ref[...]
    s = (c * jax.nn.sigmoid(c)).astype(BF16)
    o_ref[0] = _dot(s, w_ref[0].astype(BF16)) + b_ref[0]


def _modulation(cond, w_ada, b_ada):
    tn = 1024
    n = N_MOD * D_MODEL
    return pl.pallas_call(
        _mod_kernel,
        grid=(DEPTH, n // tn),
        in_specs=[pl.BlockSpec((MOD_ROWS, D_MODEL), lambda l, j: (0, 0)),
                  pl.BlockSpec((1, D_MODEL, tn), lambda l, j: (l, 0, j)),
                  pl.BlockSpec((1, 1, tn), lambda l, j: (l, 0, j))],
        out_specs=pl.BlockSpec((1, MOD_ROWS, tn), lambda l, j: (l, 0, j)),
        out_shape=jax.ShapeDtypeStruct((DEPTH, MOD_ROWS, n), F32),
        compiler_params=_cparams(),
        name='modulation',
    )(cond, w_ada, b_ada.reshape(DEPTH, 1, n))


def _mod_index(latent, tm):
    if latent:
        per = DEC_SEQ // tm
        return lambda i, *_: (1 + i // per, 0, 0)
    return lambda i, *_: (0, 0, 0)


def _norm_mod(x, g, mod, shift_idx, scale_idx):
    var = jnp.mean(x * x, axis=-1, keepdims=True)
    y = x * lax.rsqrt(var + RMS_EPS) * g
    return y * (1.0 + mod[scale_idx:scale_idx + 1]) + mod[shift_idx:shift_idx + 1]


def _nm_kernel(*refs, has_bias, act):
    if has_bias:
        x_ref, mod_ref, g_ref, w_ref, b_ref, o_ref, h_scr = refs
    else:
        x_ref, mod_ref, g_ref, w_ref, o_ref, h_scr = refs

    @pl.when(pl.program_id(1) == 0)
    def _():
        h_scr[...] = _norm_mod(x_ref[...], g_ref[...], mod_ref[0], 0, 1).astype(BF16)

    acc = _dot(h_scr[...], w_ref[0])
    if has_bias:
        acc = acc + b_ref[...]
    if act == 'sigmoid':
        acc = jax.nn.sigmoid(acc)
    o_ref[...] = acc.astype(o_ref.dtype)


def _normmod_matmul(x, mod, g, w, layer, bias, act, out_dtype, latent, name):
    m, n = x.shape[0], w.shape[2]
    tm, tn = 1024, 1024
    in_specs = [pl.BlockSpec((tm, D_MODEL), lambda i, j: (i, 0)),
                pl.BlockSpec((1, N_MOD, D_MODEL), _mod_index(latent, tm)),
                pl.BlockSpec((1, D_MODEL), lambda i, j: (0, 0)),
                pl.BlockSpec((1, D_MODEL, tn), lambda i, j: (layer, 0, j))]
    args = [x, mod, g.reshape(1, D_MODEL), w]
    if bias is not None:
        in_specs.append(pl.BlockSpec((1, tn), lambda i, j: (0, j)))
        args.append(bias.reshape(1, n))
    return pl.pallas_call(
        functools.partial(_nm_kernel, has_bias=bias is not None, act=act),
        grid=(m // tm, n // tn),
        in_specs=in_specs,
        out_specs=pl.BlockSpec((tm, tn), lambda i, j: (i, j)),
        out_shape=jax.ShapeDtypeStruct((m, n), out_dtype),
        scratch_shapes=[pltpu.VMEM((tm, D_MODEL), BF16)],
        compiler_params=_cparams(),
        name=name,
    )(*args)


def _lane_lo():
    return lax.broadcasted_iota(jnp.int32, (1, LANES), 1) < HEAD_DIM


def _seg_matrix():
    r = lax.broadcasted_iota(jnp.int32, (LANES, LANES), 0) // HEAD_DIM
    c = lax.broadcasted_iota(jnp.int32, (LANES, LANES), 1) // HEAD_DIM
    return jnp.where(r == c, 1.0, 0.0).astype(BF16)


def _head_rmsnorm(x, gain):
    x2 = x * x
    hi = x2.astype(BF16)
    lo = (x2 - hi.astype(F32)).astype(BF16)
    seg = _seg_matrix()
    ms = (_dot(hi, seg) + _dot(lo, seg)) * (1.0 / HEAD_DIM)
    return x * lax.rsqrt(ms + RMS_EPS) * gain


def _rope(x, cos, sin_signed):
    lane = lax.broadcasted_iota(jnp.int32, (1, LANES), 1)
    first = (lane & (2 * N_FREQ - 1)) < N_FREQ
    up = pltpu.roll(x, LANES - N_FREQ, 1)
    dn = pltpu.roll(x, N_FREQ, 1)
    return x * cos + jnp.where(first, up, dn) * sin_signed


def _rope_tables(t):
    pos = jnp.arange(t)
    row = (pos // GRID_W).astype(F32)
    col = (pos % GRID_W).astype(F32)
    inv = ROPE_BASE ** (-jnp.arange(N_FREQ, dtype=F32) / N_FREQ)
    ar, ac = row[:, None] * inv, col[:, None] * inv
    cos = jnp.concatenate([jnp.cos(ar), jnp.cos(ar), jnp.cos(ac), jnp.cos(ac)], axis=-1)
    sin = jnp.concatenate([-jnp.sin(ar), jnp.sin(ar), -jnp.sin(ac), jnp.sin(ac)], axis=-1)
    return jnp.tile(cos, (1, 2)), jnp.tile(sin, (1, 2))


def _gqa_kernel(*refs, t, tq, n_ctx, banded, has_sink, has_rope, emit_k, ck):
    it = iter(refs)
    q_ref, k_ref, v_ref, qn_ref, kn_ref = (next(it) for _ in range(5))
    if has_rope:
        cq_ref, sq_ref, cka_ref, ska_ref = (next(it) for _ in range(4))
    if n_ctx:
        kc_ref, vc_ref = next(it), next(it)
    if has_sink:
        sink_ref = next(it)
    if emit_k:
        next(it), next(it)
    o_ref = next(it)
    if emit_k:
        ko_ref, vo_ref = next(it), next(it)
    k_scr, v_scr = next(it), next(it)

    g = pl.program_id(1)
    qi = pl.program_id(2)
    lo = _lane_lo()
    lat0 = BLOCK if banded else 0
    ctx0 = t + 2 * lat0
    first_head = g == 0

    def put(dst, kx, vx):
        kr = pltpu.roll(kx, HEAD_DIM, 1)
        vr = pltpu.roll(vx, HEAD_DIM, 1)
        zero = jnp.zeros_like(kx)
        k_scr[0, dst, :] = jnp.where(lo, jnp.where(first_head, kx, kr), zero).astype(BF16)
        k_scr[1, dst, :] = jnp.where(lo, zero, jnp.where(first_head, kr, kx)).astype(BF16)
        v_scr[0, dst, :] = jnp.where(lo, jnp.where(first_head, vx, vr), zero).astype(BF16)
        v_scr[1, dst, :] = jnp.where(lo, zero, jnp.where(first_head, vr, vx)).astype(BF16)

    @pl.when(qi == 0)
    def _build():
        step = min(t, 512)
        for r in range(0, t, step):
            kx = _head_rmsnorm(k_ref[r:r + step, :], kn_ref[...])
            if emit_k:
                ko_ref[0, 0, r:r + step, :] = kx
                vo_ref[0, 0, r:r + step, :] = v_ref[r:r + step, :]
            if has_rope:
                kx = _rope(kx, cka_ref[r:r + step, :], ska_ref[r:r + step, :])
            put(slice(lat0 + r, lat0 + r + step), kx, v_ref[r:r + step, :])
        if banded:
            zpad = jnp.zeros((BLOCK, LANES), BF16)
            for scr in (k_scr, v_scr):
                for var in range(2):
                    scr[var, 0:BLOCK, :] = zpad
                    scr[var, lat0 + t:lat0 + t + BLOCK, :] = zpad
        if n_ctx:
            put(slice(ctx0, ctx0 + n_ctx), kc_ref[0, 0], vc_ref[0, 0])

    def step_fn(q2, segs, carry):
        m0, m1, l, acc = carry

        def scores(var):
            parts = []
            for start, size, mask in segs:
                s = _dot_nt(q2, k_scr[var, pl.ds(start, size), :])
                parts.append(s if mask is None else jnp.where(mask, s, NEG))
            return parts[0] if len(parts) == 1 else jnp.concatenate(parts, axis=1)

        s0, s1 = scores(0), scores(1)
        n0 = jnp.maximum(m0, jnp.max(s0, axis=-1, keepdims=True))
        n1 = jnp.maximum(m1, jnp.max(s1, axis=-1, keepdims=True))
        p0 = jnp.exp2(s0 - n0)
        p1 = jnp.exp2(s1 - n1)
        alpha = jnp.where(lo, jnp.exp2(m0 - n0), jnp.exp2(m1 - n1))
        rs = jnp.where(lo, jnp.sum(p0, axis=-1, keepdims=True), jnp.sum(p1, axis=-1, keepdims=True))
        l = alpha * l + rs
        acc = alpha * acc
        p0, p1 = p0.astype(BF16), p1.astype(BF16)
        off = 0
        for start, size, _ in segs:
            acc = (acc + _dot(p0[:, off:off + size], v_scr[0, pl.ds(start, size), :])
                   + _dot(p1[:, off:off + size], v_scr[1, pl.ds(start, size), :]))
            off += size
        return n0, n1, l, acc

    if banded:
        span = tq + 2 * BLOCK
        rr = lax.broadcasted_iota(jnp.int32, (tq, span), 0)
        cc = lax.broadcasted_iota(jnp.int32, (tq, span), 1)
        kpos = qi * tq - BLOCK + cc
        band_mask = (cc >= rr) & (cc - rr <= 2 * BLOCK) & (kpos >= 0) & (kpos < t)

    q = q_ref[...]
    for p in range(2):
        qp = _head_rmsnorm(q[:, p * LANES:(p + 1) * LANES], qn_ref[...])
        if has_rope:
            qp = _rope(qp, cq_ref[...], sq_ref[...])
        q2 = (qp * (ATTN_SCALE * LOG2E)).astype(BF16)
        if has_sink:
            h0 = g * 4 + 2 * p
            m0 = jnp.full((tq, 1), sink_ref[h0] * LOG2E, F32)
            m1 = jnp.full((tq, 1), sink_ref[h0 + 1] * LOG2E, F32)
            l = jnp.ones((tq, LANES), F32)
        else:
            m0 = jnp.full((tq, 1), NEG, F32)
            m1 = jnp.full((tq, 1), NEG, F32)
            l = jnp.zeros((tq, LANES), F32)
        carry = (m0, m1, l, jnp.zeros((tq, LANES), F32))
        if banded:
            segs = [(pl.multiple_of(qi * tq, tq), span, band_mask)]
            if n_ctx:
                segs.append((ctx0, n_ctx, None))
            carry = step_fn(q2, segs, carry)
        else:
            for c in range((t + n_ctx) // ck):
                carry = step_fn(q2, [(c * ck, ck, None)], carry)
        _, _, l, acc = carry
        o_ref[:, p * LANES:(p + 1) * LANES] = (acc / l).astype(o_ref.dtype)


def _gqa_call(z, *, b, t, tq, qname, kname, vname, qn, kn, sink, rope, cache, layer, banded,
              kv_out, name):
    nq = t // tq
    n_ctx = PAST_LEN if cache is not None else 0
    rows = t + n_ctx + (2 * BLOCK if banded else 0)
    ck = min(ATTN_CHUNK, t + n_ctx)
    qblk, kblk, vblk = _COL[qname] // 256, _COL[kname] // LANES, _COL[vname] // LANES
    const = lambda bb, g, i: (0, 0)
    in_specs = [pl.BlockSpec((tq, 256), lambda bb, g, i: (bb * nq + i, qblk + g)),
                pl.BlockSpec((t, LANES), lambda bb, g, i: (bb, kblk)),
                pl.BlockSpec((t, LANES), lambda bb, g, i: (bb, vblk)),
                pl.BlockSpec((1, LANES), const),
                pl.BlockSpec((1, LANES), const)]
    args = [z, z, z, jnp.tile(qn, 2).reshape(1, LANES), jnp.tile(kn, 2).reshape(1, LANES)]
    if rope is not None:
        cos, sin = rope
        in_specs += [pl.BlockSpec((tq, LANES), lambda bb, g, i: (i, 0)),
                     pl.BlockSpec((tq, LANES), lambda bb, g, i: (i, 0)),
                     pl.BlockSpec((t, LANES), const),
                     pl.BlockSpec((t, LANES), const)]
        args += [cos, sin, cos, sin]
    if cache is not None:
        spec = pl.BlockSpec((1, 1, n_ctx, LANES), lambda bb, g, i: (bb, layer, 0, 0))
        in_specs += [spec, spec]
        args += [cache[0], cache[1]]
    if sink is not None:
        in_specs.append(pl.BlockSpec(memory_space=pltpu.SMEM))
        args.append(sink)
    out_shape = [jax.ShapeDtypeStruct((b * t, 512), BF16)]
    out_specs = [pl.BlockSpec((tq, 256), lambda bb, g, i: (bb * nq + i, g))]
    aliases = {}
    if kv_out is not None:
        for n_out, arr in enumerate(kv_out):
            aliases[len(args)] = 1 + n_out
            in_specs.append(pl.BlockSpec(memory_space=pl.ANY))
            args.append(arr)
            out_shape.append(jax.ShapeDtypeStruct(arr.shape, arr.dtype))
            out_specs.append(pl.BlockSpec((1, 1, t, LANES), lambda bb, g, i: (bb, layer, 0, 0)))
    res = pl.pallas_call(
        functools.partial(_gqa_kernel, t=t, tq=tq, n_ctx=n_ctx, banded=banded,
                          has_sink=sink is not None, has_rope=rope is not None,
                          emit_k=kv_out is not None, ck=ck),
        grid=(b, 2, nq),
        in_specs=in_specs,
        out_specs=out_specs,
        out_shape=out_shape,
        input_output_aliases=aliases,
        scratch_shapes=[pltpu.VMEM((2, rows, LANES), BF16), pltpu.VMEM((2, rows, LANES), BF16)],
        compiler_params=_cparams(),
        name=name,
    )(*args)
    return res[0], tuple(res[1:])


def _diff_kernel(*refs, t, tq, n_ctx, has_rope, emit_k, ck, lam_init):
    it = iter(refs)
    q_ref, k_ref, v_ref, qn_ref, kn_ref, lp_ref, og_ref = (next(it) for _ in range(7))
    if has_rope:
        cq_ref, sq_ref, cka_ref, ska_ref = (next(it) for _ in range(4))
    if n_ctx:
        kc_ref, vc_ref = next(it), next(it)
    if emit_k:
        next(it), next(it)
    o_ref = next(it)
    if emit_k:
        ko_ref, vo_ref = next(it), next(it)
    k_scr, v_scr = next(it), next(it)

    qi = pl.program_id(2)
    lo = _lane_lo()

    def put(dst, kx, vx):
        zero = jnp.zeros_like(kx)
        k_scr[0, dst, :] = jnp.where(lo, kx, zero).astype(BF16)
        k_scr[1, dst, :] = jnp.where(lo, zero, kx).astype(BF16)
        v_scr[dst, :] = vx.astype(BF16)

    @pl.when(qi == 0)
    def _build():
        step = min(t, 512)
        for r in range(0, t, step):
            kx = _head_rmsnorm(k_ref[r:r + step, :], kn_ref[...])
            if emit_k:
                ko_ref[0, 0, r:r + step, :] = kx
                vo_ref[0, 0, r:r + step, :] = v_ref[r:r + step, :]
            if has_rope:
                kx = _rope(kx, cka_ref[r:r + step, :], ska_ref[r:r + step, :])
            put(slice(r, r + step), kx, v_ref[r:r + step, :])
        if n_ctx:
            put(slice(t, t + n_ctx), kc_ref[0, 0], vc_ref[0, 0])

    qp = _head_rmsnorm(q_ref[...], qn_ref[...])
    if has_rope:
        qp = _rope(qp, cq_ref[...], sq_ref[...])
    q2 = (qp * (ATTN_SCALE * LOG2E)).astype(BF16)

    def step_fn(start, carry):
        m0, m1, l0, l1, a0, a1 = carry
        v = v_scr[pl.ds(start, ck), :]
        s0 = _dot_nt(q2, k_scr[0, pl.ds(start, ck), :])
        s1 = _dot_nt(q2, k_scr[1, pl.ds(start, ck), :])
        n0 = jnp.maximum(m0, jnp.max(s0, axis=-1, keepdims=True))
        n1 = jnp.maximum(m1, jnp.max(s1, axis=-1, keepdims=True))
        p0 = jnp.exp2(s0 - n0)
        p1 = jnp.exp2(s1 - n1)
        e0 = jnp.exp2(m0 - n0)
        e1 = jnp.exp2(m1 - n1)
        l0 = e0 * l0 + jnp.sum(p0, axis=-1, keepdims=True)
        l1 = e1 * l1 + jnp.sum(p1, axis=-1, keepdims=True)
        a0 = e0 * a0 + _dot(p0.astype(BF16), v)
        a1 = e1 * a1 + _dot(p1.astype(BF16), v)
        return n0, n1, l0, l1, a0, a1

    col = lambda val: jnp.full((tq, 1), val, F32)
    carry = (col(NEG), col(NEG), col(0.0), col(0.0),
             jnp.zeros((tq, LANES), F32), jnp.zeros((tq, LANES), F32))
    for c in range((t + n_ctx) // ck):
        carry = step_fn(c * ck, carry)
    _, _, l0, l1, a0, a1 = carry

    lp = lp_ref[...]
    lam = (jnp.exp(jnp.sum(lp[0:1] * lp[1:2], axis=-1, keepdims=True))
           - jnp.exp(jnp.sum(lp[2:3] * lp[3:4], axis=-1, keepdims=True)) + lam_init)
    o = a0 / l0 - lam * (a1 / l1)
    var = jnp.mean(o * o, axis=-1, keepdims=True)
    o = o * lax.rsqrt(var + RMS_EPS) * og_ref[...] * (1.0 - lam_init)
    o_ref[...] = o.astype(o_ref.dtype)


def _diff_call(z, *, b, t, tq, qn, kn, lparams, out_g, rope, cache, layer, kv_out, name):
    nq = t // tq
    n_ctx = PAST_LEN if cache is not None else 0
    ck = min(ATTN_CHUNK, t + n_ctx)
    qblk, kblk, vblk = _COL['qd'] // LANES, _COL['kd'] // LANES, _COL['vd'] // LANES
    const = lambda bb, h, i: (0, 0)
    in_specs = [pl.BlockSpec((tq, LANES), lambda bb, h, i: (bb * nq + i, qblk + h)),
                pl.BlockSpec((t, LANES), lambda bb, h, i: (bb, kblk + h)),
                pl.BlockSpec((t, LANES), lambda bb, h, i: (bb, vblk + h)),
                pl.BlockSpec((1, LANES), const),
                pl.BlockSpec((1, LANES), const),
                pl.BlockSpec((4, HEAD_DIM), const),
                pl.BlockSpec((1, LANES), const)]
    args = [z, z, z, jnp.tile(qn, 2).reshape(1, LANES), jnp.tile(kn, 2).reshape(1, LANES),
            lparams, out_g.reshape(1, LANES)]
    if rope is not None:
        cos, sin = rope
        in_specs += [pl.BlockSpec((tq, LANES), lambda bb, h, i: (i, 0)),
                     pl.BlockSpec((tq, LANES), lambda bb, h, i: (i, 0)),
                     pl.BlockSpec((t, LANES), const),
                     pl.BlockSpec((t, LANES), const)]
        args += [cos, sin, cos, sin]
    if cache is not None:
        spec = pl.BlockSpec((1, 1, n_ctx, LANES), lambda bb, h, i: (bb, layer, 0, h))
        in_specs += [spec, spec]
        args += [cache[0], cache[1]]
    out_shape = [jax.ShapeDtypeStruct((b * t, 512), BF16)]
    out_specs = [pl.BlockSpec((tq, LANES), lambda bb, h, i: (bb * nq + i, h))]
    aliases = {}
    if kv_out is not None:
        for n_out, arr in enumerate(kv_out):
            aliases[len(args)] = 1 + n_out
            in_specs.append(pl.BlockSpec(memory_space=pl.ANY))
            args.append(arr)
            out_shape.append(jax.ShapeDtypeStruct(arr.shape, arr.dtype))
            out_specs.append(pl.BlockSpec((1, 1, t, LANES), lambda bb, h, i: (bb, layer, 0, h)))
    lam_init = 0.8 - 0.6 * math.exp(-0.3 * layer)
    res = pl.pallas_call(
        functools.partial(_diff_kernel, t=t, tq=tq, n_ctx=n_ctx, has_rope=rope is not None,
                          emit_k=kv_out is not None, ck=ck, lam_init=lam_init),
        grid=(b, DIFF_HEADS, nq),
        in_specs=in_specs,
        out_specs=out_specs,
        out_shape=out_shape,
        input_output_aliases=aliases,
        scratch_shapes=[pltpu.VMEM((2, t + n_ctx, LANES), BF16), pltpu.VMEM((t + n_ctx, LANES), BF16)],
        compiler_params=_cparams(),
        name=name,
    )(*args)
    return res[0], tuple(res[1:])


def _softplus(x):
    return jnp.maximum(x, 0.0) + jnp.log1p(jnp.exp(-jnp.abs(x)))


def _lru_kernel(*refs, tc, nc, has_h0, emit_final):
    it = iter(refs)
    xp_ref, xm_ref, xn_ref, ya_ref, cw_ref, cb_ref, wl_ref, bl_ref, lam_ref = (next(it) for _ in range(9))
    if has_h0:
        h0_ref = next(it)
    if emit_final:
        next(it)
    o_ref = next(it)
    if emit_final:
        fin_ref = next(it)
    hf_scr, a_scr, b_scr, carry_scr = (next(it) for _ in range(4))

    s = pl.program_id(1)
    fwd = s < nc
    c = jnp.where(fwd, s, 2 * nc - 1 - s)
    t0 = pl.multiple_of(c * tc, tc)

    @pl.when(s == 0)
    def _():
        if has_h0:
            carry_scr[...] = h0_ref[0, 0]
        else:
            carry_scr[...] = jnp.zeros_like(carry_scr)

    prev = jnp.where(c > 0, xp_ref[...], 0.0)
    nxt = jnp.where(c < nc - 1, xn_ref[...], 0.0)
    ext = jnp.concatenate([prev, xm_ref[...], nxt], axis=0)
    n_ext = tc + 2 * SUBLANES
    u = cb_ref[...] + cw_ref[1:2] * xm_ref[...]
    u = u + cw_ref[0:1] * pltpu.roll(ext, 1, 0)[SUBLANES:SUBLANES + tc]
    u = u + cw_ref[2:3] * pltpu.roll(ext, n_ext - 1, 0)[SUBLANES:SUBLANES + tc]
    u = u + cw_ref[3:4] * pltpu.roll(ext, n_ext - 2, 0)[SUBLANES:SUBLANES + tc]

    gates = _dot(u.astype(BF16), wl_ref[0]) + bl_ref[0]
    r = jax.nn.sigmoid(gates[:, :LRU_W])
    ig = jax.nn.sigmoid(gates[:, LRU_W:])
    log_a = -LRU_C * r * _softplus(-lam_ref[0])
    a = jnp.exp(log_a)
    a_scr[...] = a
    b_scr[...] = jnp.sqrt(-jnp.tanh(log_a) * (a * a + 1.0)) * ig * u

    ntile = tc // SUBLANES
    row = lax.broadcasted_iota(jnp.int32, (SUBLANES, LRU_W), 0)

    def scan(forward):
        def tile(i, carry):
            j = i if forward else ntile - 1 - i
            r0 = pl.multiple_of(j * SUBLANES, SUBLANES)
            a = a_scr[pl.ds(r0, SUBLANES), :]
            bv = b_scr[pl.ds(r0, SUBLANES), :]
            for d in (1, 2, 4):
                shift = d if forward else SUBLANES - d
                msk = (row >= d) if forward else (row < SUBLANES - d)
                ap = pltpu.roll(a, shift, 0)
                bp = pltpu.roll(bv, shift, 0)
                bv = jnp.where(msk, a * bp + bv, bv)
                a = jnp.where(msk, a * ap, a)
            h = a * carry + bv
            g0 = pl.multiple_of(t0 + r0, SUBLANES)
            if forward:
                hf_scr[pl.ds(g0, SUBLANES), :] = h
                return h[SUBLANES - 1:SUBLANES, :]
            b_scr[pl.ds(r0, SUBLANES), :] = h
            return h[0:1, :]

        idx = 0 if forward else 1
        last = lax.fori_loop(0, ntile, tile, carry_scr[idx:idx + 1, :])
        carry_scr[idx:idx + 1, :] = last
        if emit_final:
            fin_ref[0, 0, idx:idx + 1, :] = last
        if not forward:
            o_ref[...] = ((hf_scr[pl.ds(t0, tc), :] + b_scr[...])
                          * jax.nn.gelu(ya_ref[...])).astype(o_ref.dtype)

    pl.when(fwd)(lambda: scan(True))
    pl.when(jnp.logical_not(fwd))(lambda: scan(False))


def _lru_call(z, *, b, t, conv_w, conv_b, w_lru, b_lru, lam, h0, state_out, layer, name):
    tc = min(t, 512)
    nc = t // tc
    per8 = tc // SUBLANES
    nrow8 = b * t // SUBLANES

    def chunk(s):
        return jnp.where(s < nc, s, 2 * nc - 1 - s)

    def hold(s):
        return jnp.where(s < nc, nc - 1, 2 * nc - 1 - s)

    in_specs = [
        pl.BlockSpec((SUBLANES, LRU_W),
                     lambda bb, s: (jnp.maximum((bb * nc + chunk(s)) * per8 - 1, 0), 0)),
        pl.BlockSpec((tc, LRU_W), lambda bb, s: (bb * nc + chunk(s), 0)),
        pl.BlockSpec((SUBLANES, LRU_W),
                     lambda bb, s: (jnp.minimum((bb * nc + chunk(s) + 1) * per8, nrow8 - 1), 0)),
        pl.BlockSpec((tc, LRU_W), lambda bb, s: (bb * nc + hold(s), 1)),
        pl.BlockSpec((CONV_W, LRU_W), lambda bb, s: (0, 0)),
        pl.BlockSpec((1, LRU_W), lambda bb, s: (0, 0)),
        pl.BlockSpec((1, LRU_W, 2 * LRU_W), lambda bb, s: (s // nc, 0, 0)),
        pl.BlockSpec((1, 1, 2 * LRU_W), lambda bb, s: (s // nc, 0, 0)),
        pl.BlockSpec((1, 1, LRU_W), lambda bb, s: (s // nc, 0, 0)),
    ]
    args = [z, z, z, z, conv_w, conv_b.reshape(1, LRU_W), w_lru, b_lru, lam.reshape(2, 1, LRU_W)]
    if h0 is not None:
        in_specs.append(pl.BlockSpec((1, 1, 2, LRU_W), lambda bb, s: (bb, layer, 0, 0)))
        args.append(h0)
    out_shape = [jax.ShapeDtypeStruct((b * t, LRU_W), BF16)]
    out_specs = [pl.BlockSpec((tc, LRU_W), lambda bb, s: (bb * nc + hold(s), 0))]
    aliases = {}
    if state_out is not None:
        aliases[len(args)] = 1
        in_specs.append(pl.BlockSpec(memory_space=pl.ANY))
        args.append(state_out)
        out_shape.append(jax.ShapeDtypeStruct(state_out.shape, state_out.dtype))
        out_specs.append(pl.BlockSpec((1, 1, 2, LRU_W), lambda bb, s: (bb, layer, 0, 0)))
    res = pl.pallas_call(
        functools.partial(_lru_kernel, tc=tc, nc=nc, has_h0=h0 is not None,
                          emit_final=state_out is not None),
        grid=(b, 2 * nc),
        in_specs=in_specs,
        out_specs=out_specs,
        out_shape=out_shape,
        input_output_aliases=aliases,
        scratch_shapes=[pltpu.VMEM((t, LRU_W), F32), pltpu.VMEM((tc, LRU_W), F32),
                        pltpu.VMEM((tc, LRU_W), F32), pltpu.VMEM((2, LRU_W), F32)],
        compiler_params=_cparams(),
        name=name,
    )(*args)
    return res[0], (res[1] if state_out is not None else None)


def _merge_kernel(x_ref, gt_ref, a_ref, w_ref, g_ref, d_ref, wb_ref, wo_ref, mod_ref, o_ref):
    merged = None
    for bi, br_ref in enumerate((a_ref, w_ref, g_ref, d_ref)):
        gate = gt_ref[:, bi * D_MODEL:(bi + 1) * D_MODEL].astype(F32)
        term = gate * _dot(br_ref[...], wb_ref[0, bi])
        merged = term if merged is None else merged + term
    o_ref[...] = x_ref[...] + mod_ref[0][2:3] * _dot(merged.astype(BF16), wo_ref[0])


def _merge_call(x, gates, branches, w_branch, w_o, layer, mod, latent, name):
    m = x.shape[0]
    tm = 256
    row = lambda i: (i, 0)
    resident = pl.Buffered(1)
    in_specs = [pl.BlockSpec((tm, D_MODEL), row),
                pl.BlockSpec((tm, N_BRANCH * D_MODEL), row)]
    in_specs += [pl.BlockSpec((tm, BRANCH_W), row)] * N_BRANCH
    in_specs += [pl.BlockSpec((1, N_BRANCH, BRANCH_W, D_MODEL), lambda i: (layer, 0, 0, 0),
                              pipeline_mode=resident),
                 pl.BlockSpec((1, D_MODEL, D_MODEL), lambda i: (layer, 0, 0), pipeline_mode=resident),
                 pl.BlockSpec((1, N_MOD, D_MODEL), _mod_index(latent, tm))]
    return pl.pallas_call(
        _merge_kernel,
        grid=(m // tm,),
        in_specs=in_specs,
        out_specs=pl.BlockSpec((tm, D_MODEL), row),
        out_shape=jax.ShapeDtypeStruct((m, D_MODEL), F32),
        compiler_params=_cparams(),
        name=name,
    )(x, gates, *branches, w_branch, w_o, mod)


def _ffn_kernel(x_ref, mod_ref, g_ref, w1_ref, b1_ref, w2_ref, b2_ref, o_ref, h_scr, acc_scr):
    j = pl.program_id(1)

    @pl.when(j == 0)
    def _():
        h_scr[...] = _norm_mod(x_ref[...], g_ref[...], mod_ref[0], 3, 4).astype(BF16)
        acc_scr[...] = jnp.zeros_like(acc_scr)

    a = jnp.maximum(_dot(h_scr[...], w1_ref[0]) + b1_ref[...], 0.0)
    acc_scr[...] += _dot((a * a).astype(BF16), w2_ref[0])

    @pl.when(j == pl.num_programs(1) - 1)
    def _():
        o_ref[...] = x_ref[...] + mod_ref[0][5:6] * (acc_scr[...] + b2_ref[...])


def _ffn_call(x, mod, g, w1, b1, w2, b2, layer, latent, name):
    m = x.shape[0]
    tm, tf = 512, 1024
    return pl.pallas_call(
        _ffn_kernel,
        grid=(m // tm, D_FF // tf),
        in_specs=[pl.BlockSpec((tm, D_MODEL), lambda i, j: (i, 0)),
                  pl.BlockSpec((1, N_MOD, D_MODEL), _mod_index(latent, tm)),
                  pl.BlockSpec((1, D_MODEL), lambda i, j: (0, 0)),
                  pl.BlockSpec((1, D_MODEL, tf), lambda i, j: (layer, 0, j)),
                  pl.BlockSpec((1, tf), lambda i, j: (0, j)),
                  pl.BlockSpec((1, tf, D_MODEL), lambda i, j: (layer, j, 0)),
                  pl.BlockSpec((1, D_MODEL), lambda i, j: (0, 0))],
        out_specs=pl.BlockSpec((tm, D_MODEL), lambda i, j: (i, 0)),
        out_shape=jax.ShapeDtypeStruct((m, D_MODEL), F32),
        scratch_shapes=[pltpu.VMEM((tm, D_MODEL), BF16), pltpu.VMEM((tm, D_MODEL), F32)],
        compiler_params=_cparams(),
        name=name,
    )(x, mod, g.reshape(1, D_MODEL), w1, b1.reshape(1, D_FF), w2, b2.reshape(1, D_MODEL))


def _block_diag(w):
    eye = jnp.eye(LRU_BLOCKS, dtype=w.dtype)
    return jnp.einsum('ncd,nm->ncmd', w, eye).reshape(LRU_W, LRU_W)


def _trunk_layer(x, mod, wts, layer, latent, rope, cached, new):
    b, t = (DEC_BATCH, DEC_SEQ) if latent else (BATCH, SEQ)
    tag = ('lat' if latent else 'ctx') + str(layer)
    z = _normmod_matmul(x, mod, wts['norm1_g'], wts['w_in'], layer, None, None, F32, latent,
                        'inproj_' + tag)
    gates = _normmod_matmul(x, mod, wts['norm1_g'], wts['w_gate'], layer, wts['b_gate'], 'sigmoid', BF16,
                            latent, 'gates_' + tag)
    a_out, new_state = _lru_call(
        z, b=b, t=t, conv_w=wts['conv_w'], conv_b=wts['conv_b'], w_lru=wts['w_lru'], b_lru=wts['b_lru'],
        lam=wts['lru_lambda'], h0=cached[6] if latent else None, state_out=None if latent else new[6],
        layer=layer, name='lru_' + tag)
    w_out, new_win = _gqa_call(
        z, b=b, t=t, tq=256, qname='qw', kname='kw', vname='vw',
        qn=wts['win_qn'], kn=wts['win_kn'], sink=wts['win_sink'], rope=rope,
        cache=(cached[0], cached[1]) if latent else None, layer=layer, banded=latent,
        kv_out=None if latent else new[0:2], name='win_' + tag)
    g_out, new_grid = _gqa_call(
        z, b=b, t=t, tq=256, qname='qg', kname='kg', vname='vg',
        qn=wts['grid_qn'], kn=wts['grid_kn'], sink=None, rope=rope,
        cache=(cached[2], cached[3]) if latent else None, layer=layer, banded=False,
        kv_out=None if latent else new[2:4], name='grid_' + tag)
    d_out, new_diff = _diff_call(
        z, b=b, t=t, tq=256, qn=wts['diff_qn'], kn=wts['diff_kn'], lparams=wts['diff_lp'],
        out_g=wts['diff_out_g'], rope=rope, cache=(cached[4], cached[5]) if latent else None,
        layer=layer, kv_out=None if latent else new[4:6], name='diff_' + tag)
    x = _merge_call(x, gates, (a_out, w_out, g_out, d_out), wts['w_branch'], wts['w_o'], layer, mod,
                    latent, 'merge_' + tag)
    x = _ffn_call(x, mod, wts['norm2_g'], wts['w_ff1'], wts['b_ff1'], wts['w_ff2'], wts['b_ff2'],
                  layer, latent, 'ffn_' + tag)
    if latent:
        return x, None
    return x, new_win + new_grid + new_diff + (new_state,)


def kernel(x_prompt, x_sample, cache_win_k, cache_win_v, cache_grid_k, cache_grid_v, cache_diff_k, cache_diff_v, state_lru, c, c_ctx, w_ada, b_ada, norm1_g, norm2_g, w_in, conv_w, conv_b, lru_wr, lru_br, lru_wi, lru_bi, lru_lambda, win_qn, win_kn, win_sink, grid_qn, grid_kn, diff_qn, diff_kn, diff_lq1, diff_lk1, diff_lq2, diff_lk2, diff_out_g, w_branch, w_gate, b_gate, w_o, w_ff1, b_ff1, w_ff2, b_ff2):
    cond = jnp.zeros((MOD_ROWS, D_MODEL), F32).at[0].set(c_ctx).at[1:1 + DEC_BATCH].set(c)
    mod_all = _modulation(cond, w_ada, b_ada).reshape(DEPTH, MOD_ROWS, N_MOD, D_MODEL)
    rope = _rope_tables(DEC_SEQ)
    cached_all = (cache_win_k.reshape(DEC_BATCH, DEPTH, PAST_LEN, LANES),
                  cache_win_v.reshape(DEC_BATCH, DEPTH, PAST_LEN, LANES),
                  cache_grid_k.reshape(DEC_BATCH, DEPTH, PAST_LEN, LANES),
                  cache_grid_v.reshape(DEC_BATCH, DEPTH, PAST_LEN, LANES),
                  cache_diff_k.reshape(DEC_BATCH, DEPTH, PAST_LEN, 512),
                  cache_diff_v.reshape(DEC_BATCH, DEPTH, PAST_LEN, 512))
    w_in_b, w_gate_b, w_branch_b, w_o_b, w_ff1_b, w_ff2_b = (
        w.astype(BF16) for w in (w_in, w_gate, w_branch, w_o, w_ff1, w_ff2))
    y_p = x_prompt.reshape(BATCH * SEQ, D_MODEL)
    y_s = x_sample.reshape(DEC_BATCH * DEC_SEQ, D_MODEL)
    new = tuple(jnp.zeros((BATCH, DEPTH, SEQ, w), F32) for w in (LANES,) * 4 + (512,) * 2)
    new += (jnp.zeros((BATCH, DEPTH, 2, LRU_W), F32),)
    for l in range(DEPTH):
        w_lru = jnp.stack([
            jnp.concatenate([_block_diag(lru_wr[l, k]), _block_diag(lru_wi[l, k])], axis=1)
            for k in range(2)]).astype(BF16)
        b_lru = jnp.concatenate([lru_br[l], lru_bi[l]], axis=-1).reshape(2, 1, 2 * LRU_W)
        wts = {
            'norm1_g': norm1_g[l], 'norm2_g': norm2_g[l],
            'w_in': w_in_b, 'w_gate': w_gate_b, 'b_gate': b_gate[l],
            'conv_w': conv_w[l], 'conv_b': conv_b[l], 'w_lru': w_lru, 'b_lru': b_lru,
            'lru_lambda': lru_lambda[l],
            'win_qn': win_qn[l], 'win_kn': win_kn[l], 'win_sink': win_sink[l],
            'grid_qn': grid_qn[l], 'grid_kn': grid_kn[l],
            'diff_qn': diff_qn[l], 'diff_kn': diff_kn[l],
            'diff_lp': jnp.stack([diff_lq1[l], diff_lk1[l], diff_lq2[l], diff_lk2[l]]),
            'diff_out_g': diff_out_g[l],
            'w_branch': w_branch_b, 'w_o': w_o_b,
            'w_ff1': w_ff1_b, 'b_ff1': b_ff1[l], 'w_ff2': w_ff2_b, 'b_ff2': b_ff2[l],
        }
        y_p, new = _trunk_layer(y_p, mod_all[l], wts, l, False, None, None, new)
        y_s, _ = _trunk_layer(y_s, mod_all[l], wts, l, True, rope, cached_all + (state_lru,), None)

    kv_shape = (BATCH, DEPTH, SEQ, WIN_KV, HEAD_DIM)
    return (y_p.reshape(BATCH, SEQ, D_MODEL),
            y_s.reshape(DEC_BATCH, DEC_SEQ, D_MODEL),
            new[0].reshape(kv_shape), new[1].reshape(kv_shape),
            new[2].reshape(kv_shape), new[3].reshape(kv_shape),
            new[4].reshape(BATCH, DEPTH, SEQ, DIFF_HEADS, 2, HEAD_DIM),
            new[5].reshape(BATCH, DEPTH, SEQ, DIFF_HEADS, 2 * HEAD_DIM),
            new[6])
```

```python
import functools
import math

import jax
import jax.numpy as jnp
import numpy as np
from jax import lax
from jax.experimental import pallas as pl
from jax.experimental.pallas import tpu as pltpu

F32 = jnp.float32
BF16 = jnp.bfloat16

D_MODEL = 2048
BATCH = 32
SEQ = 256
DEPTH = 4
DEC_BATCH = 4
DEC_SEQ = 4096
PAST_LEN = 512
GRID_W = 64
BLOCK = 128
HEAD_DIM = 64
N_FREQ = HEAD_DIM // 4
ROPE_BASE = 10000.0
ATTN_SCALE = HEAD_DIM ** -0.5
RMS_EPS = 1e-6
N_MOD = 6
N_BRANCH = 4
BRANCH_W = D_MODEL // 4
LRU_W = BRANCH_W
LRU_BLOCKS = 8
LRU_BW = LRU_W // LRU_BLOCKS
LRU_C = 8.0
CONV_W = 4
WIN_HEADS = 8
WIN_KV = 2
DIFF_HEADS = 4
D_FF = 4 * D_MODEL
D_IN = 4096

V7X_VMEM_BYTES = 64 * 1024 * 1024
VMEM_LIMIT = V7X_VMEM_BYTES - 8 * 1024 * 1024
LANES = 128
SUBLANES = 8
NEG = -1e30
LOG2E = math.log2(math.e)
ATTN_CHUNK = 1536
FAST_SOFTMAX_BOUND = 40.0
MOD_ROWS = 8

_COL = dict(xa=0, ya=512, qw=1024, kw=1536, vw=1664, qg=1792, kg=2304, vg=2432, qd=2560, kd=3072,
            vd=3584)


def _cparams():
    return pltpu.CompilerParams(vmem_limit_bytes=VMEM_LIMIT)


def _dot(a, b):
    return jnp.dot(a, b, preferred_element_type=F32)


def _dot_nt(a, b):
    return lax.dot_general(a, b, (((1,), (1,)), ((), ())), preferred_element_type=F32)


def _mod_kernel(c_ref, w_ref, b_ref, o_ref):
    c = c_ref[...]
    s = (c * jax.nn.sigmoid(c)).astype(BF16)
    o_ref[0] = _dot(s, w_ref[0].astype(BF16)) + b_ref[0]


def _modulation(cond, w_ada, b_ada):
    tn = 1024
    n = N_MOD * D_MODEL
    return pl.pallas_call(
        _mod_kernel,
        grid=(DEPTH, n // tn),
        in_specs=[pl.BlockSpec((MOD_ROWS, D_MODEL), lambda l, j: (0, 0)),
                  pl.BlockSpec((1, D_MODEL, tn), lambda l, j: (l, 0, j)),
                  pl.BlockSpec((1, 1, tn), lambda l, j: (l, 0, j))],
        out_specs=pl.BlockSpec((1, MOD_ROWS, tn), lambda l, j: (l, 0, j)),
        out_shape=jax.ShapeDtypeStruct((DEPTH, MOD_ROWS, n), F32),
        compiler_params=_cparams(),
        name='modulation',
    )(cond, w_ada, b_ada.reshape(DEPTH, 1, n))


def _mod_index(latent, tm):
    if latent:
        per = DEC_SEQ // tm
        return lambda i, *_: (1 + i // per, 0, 0)
    return lambda i, *_: (0, 0, 0)


def _norm_mod(x, g, mod, shift_idx, scale_idx):
    var = jnp.mean(x * x, axis=-1, keepdims=True)
    y = x * lax.rsqrt(var + RMS_EPS) * g
    return y * (1.0 + mod[scale_idx:scale_idx + 1]) + mod[shift_idx:shift_idx + 1]


def _nm_kernel(*refs, has_bias, act):
    if has_bias:
        x_ref, mod_ref, g_ref, w_ref, b_ref, o_ref, h_scr = refs
    else:
        x_ref, mod_ref, g_ref, w_ref, o_ref, h_scr = refs

    @pl.when(pl.program_id(1) == 0)
    def _():
        h_scr[...] = _norm_mod(x_ref[...], g_ref[...], mod_ref[0], 0, 1).astype(BF16)

    acc = _dot(h_scr[...], w_ref[0])
    if has_bias:
        acc = acc + b_ref[...]
    if act == 'sigmoid':
        acc = jax.nn.sigmoid(acc)
    o_ref[...] = acc.astype(o_ref.dtype)


def _normmod_matmul(x, mod, g, w, layer, bias, act, out_dtype, latent, name):
    m, n = x.shape[0], w.shape[2]
    tm, tn = 1024, 1024
    in_specs = [pl.BlockSpec((tm, D_MODEL), lambda i, j: (i, 0)),
                pl.BlockSpec((1, N_MOD, D_MODEL), _mod_index(latent, tm)),
                pl.BlockSpec((1, D_MODEL), lambda i, j: (0, 0)),
                pl.BlockSpec((1, D_MODEL, tn), lambda i, j: (layer, 0, j))]
    args = [x, mod, g.reshape(1, D_MODEL), w]
    if bias is not None:
        in_specs.append(pl.BlockSpec((1, tn), lambda i, j: (0, j)))
        args.append(bias.reshape(1, n))
    return pl.pallas_call(
        functools.partial(_nm_kernel, has_bias=bias is not None, act=act),
        grid=(m // tm, n // tn),
        in_specs=in_specs,
        out_specs=pl.BlockSpec((tm, tn), lambda i, j: (i, j)),
        out_shape=jax.ShapeDtypeStruct((m, n), out_dtype),
        scratch_shapes=[pltpu.VMEM((tm, D_MODEL), BF16)],
        compiler_params=_cparams(),
        name=name,
    )(*args)


def _lane_lo():
    return lax.broadcasted_iota(jnp.int32, (1, LANES), 1) < HEAD_DIM


def _seg_matrix():
    r = lax.broadcasted_iota(jnp.int32, (LANES, LANES), 0) // HEAD_DIM
    c = lax.broadcasted_iota(jnp.int32, (LANES, LANES), 1) // HEAD_DIM
    return jnp.where(r == c, 1.0, 0.0).astype(BF16)


def _head_rmsnorm(x, gain):
    x2 = x * x
    hi = x2.astype(BF16)
    lo = (x2 - hi.astype(F32)).astype(BF16)
    seg = _seg_matrix()
    ms = (_dot(hi, seg) + _dot(lo, seg)) * (1.0 / HEAD_DIM)
    return x * lax.rsqrt(ms + RMS_EPS) * gain


def _rope(x, cos, sin_signed):
    lane = lax.broadcasted_iota(jnp.int32, (1, LANES), 1)
    first = (lane & (2 * N_FREQ - 1)) < N_FREQ
    up = pltpu.roll(x, LANES - N_FREQ, 1)
    dn = pltpu.roll(x, N_FREQ, 1)
    return x * cos + jnp.where(first, up, dn) * sin_signed


def _score_bound(qn_ref, kn_ref, cached_k):
    root = math.sqrt(HEAD_DIM)
    qmax = root * jnp.max(jnp.abs(qn_ref[...]))
    kmax = root * jnp.max(jnp.abs(kn_ref[...]))
    if cached_k is not None:
        kmax = jnp.maximum(kmax, jnp.sqrt(jnp.max(jnp.sum(cached_k * cached_k, axis=-1, keepdims=True))))
    return qmax * kmax * (ATTN_SCALE * LOG2E)


def _lane_tile_sum(p):
    acc = p[:, 0:LANES]
    for j in range(1, p.shape[1] // LANES):
        acc = acc + p[:, j * LANES:(j + 1) * LANES]
    return acc


def _rope_tables(t):
    pos = jnp.arange(t)
    row = (pos // GRID_W).astype(F32)
    col = (pos % GRID_W).astype(F32)
    inv = ROPE_BASE ** (-jnp.arange(N_FREQ, dtype=F32) / N_FREQ)
    ar, ac = row[:, None] * inv, col[:, None] * inv
    cos = jnp.concatenate([jnp.cos(ar), jnp.cos(ar), jnp.cos(ac), jnp.cos(ac)], axis=-1)
    sin = jnp.concatenate([-jnp.sin(ar), jnp.sin(ar), -jnp.sin(ac), jnp.sin(ac)], axis=-1)
    return jnp.tile(cos, (1, 2)), jnp.tile(sin, (1, 2))


def _gqa_kernel(*refs, t, tq, n_ctx, banded, has_sink, has_rope, emit_k, ck):
    it = iter(refs)
    q_ref, k_ref, v_ref, qn_ref, kn_ref = (next(it) for _ in range(5))
    if has_rope:
        cq_ref, sq_ref, cka_ref, ska_ref = (next(it) for _ in range(4))
    if n_ctx:
        kc_ref, vc_ref = next(it), next(it)
    if has_sink:
        sink_ref = next(it)
    if emit_k:
        next(it), next(it)
    o_ref = next(it)
    if emit_k:
        ko_ref, vo_ref = next(it), next(it)
    k_scr, v_scr, bound_scr = next(it), next(it), next(it)

    g = pl.program_id(1)
    qi = pl.program_id(2)
    lo = _lane_lo()
    lat0 = BLOCK if banded else 0
    ctx0 = t + 2 * lat0
    first_head = g == 0

    def put(dst, kx, vx):
        kr = pltpu.roll(kx, HEAD_DIM, 1)
        vr = pltpu.roll(vx, HEAD_DIM, 1)
        zero = jnp.zeros_like(kx)
        k_scr[0, dst, :] = jnp.where(lo, jnp.where(first_head, kx, kr), zero).astype(BF16)
        k_scr[1, dst, :] = jnp.where(lo, zero, jnp.where(first_head, kr, kx)).astype(BF16)
        v_scr[0, dst, :] = jnp.where(lo, jnp.where(first_head, vx, vr), zero).astype(BF16)
        v_scr[1, dst, :] = jnp.where(lo, zero, jnp.where(first_head, vr, vx)).astype(BF16)

    @pl.when(qi == 0)
    def _build():
        step = min(t, 512)
        for r in range(0, t, step):
            kx = _head_rmsnorm(k_ref[r:r + step, :], kn_ref[...])
            if emit_k:
                ko_ref[0, 0, r:r + step, :] = kx
                vo_ref[0, 0, r:r + step, :] = v_ref[r:r + step, :]
            if has_rope:
                kx = _rope(kx, cka_ref[r:r + step, :], ska_ref[r:r + step, :])
            put(slice(lat0 + r, lat0 + r + step), kx, v_ref[r:r + step, :])
        if banded:
            zpad = jnp.zeros((BLOCK, LANES), BF16)
            for scr in (k_scr, v_scr):
                for var in range(2):
                    scr[var, 0:BLOCK, :] = zpad
                    scr[var, lat0 + t:lat0 + t + BLOCK, :] = zpad
        if n_ctx:
            put(slice(ctx0, ctx0 + n_ctx), kc_ref[0, 0], vc_ref[0, 0])
        bound = _score_bound(qn_ref, kn_ref, kc_ref[0, 0] if n_ctx else None)
        if has_sink:
            for h in range(WIN_HEADS):
                bound = jnp.maximum(bound, jnp.abs(sink_ref[h]) * LOG2E)
        bound_scr[0] = bound

    def step_fn(q2, segs, carry):
        m0, m1, l, acc = carry

        def scores(var):
            parts = []
            for start, size, mask in segs:
                s = _dot_nt(q2, k_scr[var, pl.ds(start, size), :])
                parts.append(s if mask is None else jnp.where(mask, s, NEG))
            return parts[0] if len(parts) == 1 else jnp.concatenate(parts, axis=1)

        s0, s1 = scores(0), scores(1)
        n0 = jnp.maximum(m0, jnp.max(s0, axis=-1, keepdims=True))
        n1 = jnp.maximum(m1, jnp.max(s1, axis=-1, keepdims=True))
        p0 = jnp.exp2(s0 - n0)
        p1 = jnp.exp2(s1 - n1)
        alpha = jnp.where(lo, jnp.exp2(m0 - n0), jnp.exp2(m1 - n1))
        rs = jnp.where(lo, jnp.sum(p0, axis=-1, keepdims=True), jnp.sum(p1, axis=-1, keepdims=True))
        l = alpha * l + rs
        acc = alpha * acc
        p0, p1 = p0.astype(BF16), p1.astype(BF16)
        off = 0
        for start, size, _ in segs:
            acc = (acc + _dot(p0[:, off:off + size], v_scr[0, pl.ds(start, size), :])
                   + _dot(p1[:, off:off + size], v_scr[1, pl.ds(start, size), :]))
            off += size
        return n0, n1, l, acc

    def fast_step(q2, segs, carry):
        ls0, ls1, acc = carry
        for start, size, mask in segs:
            s0 = _dot_nt(q2, k_scr[0, pl.ds(start, size), :])
            s1 = _dot_nt(q2, k_scr[1, pl.ds(start, size), :])
            if mask is not None:
                s0 = jnp.where(mask, s0, NEG)
                s1 = jnp.where(mask, s1, NEG)
            p0 = jnp.exp2(s0)
            p1 = jnp.exp2(s1)
            ls0 = ls0 + _lane_tile_sum(p0)
            ls1 = ls1 + _lane_tile_sum(p1)
            acc = (acc + _dot(p0.astype(BF16), v_scr[0, pl.ds(start, size), :])
                   + _dot(p1.astype(BF16), v_scr[1, pl.ds(start, size), :]))
        return ls0, ls1, acc

    if banded:
        span = tq + 2 * BLOCK
        rr = lax.broadcasted_iota(jnp.int32, (tq, span), 0)
        cc = lax.broadcasted_iota(jnp.int32, (tq, span), 1)
        kpos = qi * tq - BLOCK + cc
        band_mask = (cc >= rr) & (cc - rr <= 2 * BLOCK) & (kpos >= 0) & (kpos < t)

    def attend(fast):
        q = q_ref[...]
        for p in range(2):
            qp = _head_rmsnorm(q[:, p * LANES:(p + 1) * LANES], qn_ref[...])
            if has_rope:
                qp = _rope(qp, cq_ref[...], sq_ref[...])
            q2 = (qp * (ATTN_SCALE * LOG2E)).astype(BF16)
            if banded:
                chunks = [[(pl.multiple_of(qi * tq, tq), span, band_mask)]]
                if n_ctx:
                    chunks[0].append((ctx0, n_ctx, None))
            else:
                chunks = [[(c * ck, ck, None)] for c in range((t + n_ctx) // ck)]
            if has_sink:
                h0 = g * 4 + 2 * p
                sink0 = jnp.full((1, 1), sink_ref[h0] * LOG2E, F32)
                sink1 = jnp.full((1, 1), sink_ref[h0 + 1] * LOG2E, F32)
            zeros = jnp.zeros((tq, LANES), F32)
            if fast:
                carry = (zeros, zeros, zeros)
                for segs in chunks:
                    carry = fast_step(q2, segs, carry)
                ls0, ls1, acc = carry
                l = jnp.where(lo, jnp.sum(ls0, axis=-1, keepdims=True),
                              jnp.sum(ls1, axis=-1, keepdims=True))
                if has_sink:
                    l = l + jnp.where(lo, jnp.exp2(sink0), jnp.exp2(sink1))
            else:
                if has_sink:
                    m0 = jnp.broadcast_to(sink0, (tq, 1))
                    m1 = jnp.broadcast_to(sink1, (tq, 1))
                    l = jnp.ones((tq, LANES), F32)
                else:
                    m0 = jnp.full((tq, 1), NEG, F32)
                    m1 = jnp.full((tq, 1), NEG, F32)
                    l = zeros
                carry = (m0, m1, l, zeros)
                for segs in chunks:
                    carry = step_fn(q2, segs, carry)
                _, _, l, acc = carry
            o_ref[:, p * LANES:(p + 1) * LANES] = (acc / l).astype(o_ref.dtype)

    fast_ok = bound_scr[0] <= FAST_SOFTMAX_BOUND
    pl.when(fast_ok)(lambda: attend(True))
    pl.when(jnp.logical_not(fast_ok))(lambda: attend(False))


def _gqa_call(z, *, b, t, tq, qname, kname, vname, qn, kn, sink, rope, cache, layer, banded,
              kv_out, name):
    nq = t // tq
    n_ctx = PAST_LEN if cache is not None else 0
    rows = t + n_ctx + (2 * BLOCK if banded else 0)
    ck = min(ATTN_CHUNK, t + n_ctx)
    qblk, kblk, vblk = _COL[qname] // 256, _COL[kname] // LANES, _COL[vname] // LANES
    const = lambda bb, g, i: (0, 0)
    in_specs = [pl.BlockSpec((tq, 256), lambda bb, g, i: (bb * nq + i, qblk + g)),
                pl.BlockSpec((t, LANES), lambda bb, g, i: (bb, kblk)),
                pl.BlockSpec((t, LANES), lambda bb, g, i: (bb, vblk)),
                pl.BlockSpec((1, LANES), const),
                pl.BlockSpec((1, LANES), const)]
    args = [z, z, z, jnp.tile(qn, 2).reshape(1, LANES), jnp.tile(kn, 2).reshape(1, LANES)]
    if rope is not None:
        cos, sin = rope
        in_specs += [pl.BlockSpec((tq, LANES), lambda bb, g, i: (i, 0)),
                     pl.BlockSpec((tq, LANES), lambda bb, g, i: (i, 0)),
                     pl.BlockSpec((t, LANES), const),
                     pl.BlockSpec((t, LANES), const)]
        args += [cos, sin, cos, sin]
    if cache is not None:
        spec = pl.BlockSpec((1, 1, n_ctx, LANES), lambda bb, g, i: (bb, layer, 0, 0))
        in_specs += [spec, spec]
        args += [cache[0], cache[1]]
    if sink is not None:
        in_specs.append(pl.BlockSpec(memory_space=pltpu.SMEM))
        args.append(sink)
    out_shape = [jax.ShapeDtypeStruct((b * t, 512), BF16)]
    out_specs = [pl.BlockSpec((tq, 256), lambda bb, g, i: (bb * nq + i, g))]
    aliases = {}
    if kv_out is not None:
        for n_out, arr in enumerate(kv_out):
            aliases[len(args)] = 1 + n_out
            in_specs.append(pl.BlockSpec(memory_space=pl.ANY))
            args.append(arr)
            out_shape.append(jax.ShapeDtypeStruct(arr.shape, arr.dtype))
            out_specs.append(pl.BlockSpec((1, 1, t, LANES), lambda bb, g, i: (bb, layer, 0, 0)))
    res = pl.pallas_call(
        functools.partial(_gqa_kernel, t=t, tq=tq, n_ctx=n_ctx, banded=banded,
                          has_sink=sink is not None, has_rope=rope is not None,
                          emit_k=kv_out is not None, ck=ck),
        grid=(b, 2, nq),
        in_specs=in_specs,
        out_specs=out_specs,
        out_shape=out_shape,
        input_output_aliases=aliases,
        scratch_shapes=[pltpu.VMEM((2, rows, LANES), BF16), pltpu.VMEM((2, rows, LANES), BF16),
                        pltpu.SMEM((1,), F32)],
        compiler_params=_cparams(),
        name=name,
    )(*args)
    return res[0], tuple(res[1:])


def _diff_kernel(*refs, t, tq, n_ctx, has_rope, emit_k, ck, lam_init):
    it = iter(refs)
    q_ref, k_ref, v_ref, qn_ref, kn_ref, lp_ref, og_ref = (next(it) for _ in range(7))
    if has_rope:
        cq_ref, sq_ref, cka_ref, ska_ref = (next(it) for _ in range(4))
    if n_ctx:
        kc_ref, vc_ref = next(it), next(it)
    if emit_k:
        next(it), next(it)
    o_ref = next(it)
    if emit_k:
        ko_ref, vo_ref = next(it), next(it)
    k_scr, v_scr, bound_scr = next(it), next(it), next(it)

    qi = pl.program_id(2)
    lo = _lane_lo()

    def put(dst, kx, vx):
        zero = jnp.zeros_like(kx)
        k_scr[0, dst, :] = jnp.where(lo, kx, zero).astype(BF16)
        k_scr[1, dst, :] = jnp.where(lo, zero, kx).astype(BF16)
        v_scr[dst, :] = vx.astype(BF16)

    @pl.when(qi == 0)
    def _build():
        step = min(t, 512)
        for r in range(0, t, step):
            kx = _head_rmsnorm(k_ref[r:r + step, :], kn_ref[...])
            if emit_k:
                ko_ref[0, 0, r:r + step, :] = kx
                vo_ref[0, 0, r:r + step, :] = v_ref[r:r + step, :]
            if has_rope:
                kx = _rope(kx, cka_ref[r:r + step, :], ska_ref[r:r + step, :])
            put(slice(r, r + step), kx, v_ref[r:r + step, :])
        if n_ctx:
            put(slice(t, t + n_ctx), kc_ref[0, 0], vc_ref[0, 0])
        bound_scr[0] = _score_bound(qn_ref, kn_ref, kc_ref[0, 0] if n_ctx else None)

    qp = _head_rmsnorm(q_ref[...], qn_ref[...])
    if has_rope:
        qp = _rope(qp, cq_ref[...], sq_ref[...])
    q2 = (qp * (ATTN_SCALE * LOG2E)).astype(BF16)

    def step_fn(start, carry):
        m0, m1, l0, l1, a0, a1 = carry
        v = v_scr[pl.ds(start, ck), :]
        s0 = _dot_nt(q2, k_scr[0, pl.ds(start, ck), :])
        s1 = _dot_nt(q2, k_scr[1, pl.ds(start, ck), :])
        n0 = jnp.maximum(m0, jnp.max(s0, axis=-1, keepdims=True))
        n1 = jnp.maximum(m1, jnp.max(s1, axis=-1, keepdims=True))
        p0 = jnp.exp2(s0 - n0)
        p1 = jnp.exp2(s1 - n1)
        e0 = jnp.exp2(m0 - n0)
        e1 = jnp.exp2(m1 - n1)
        l0 = e0 * l0 + jnp.sum(p0, axis=-1, keepdims=True)
        l1 = e1 * l1 + jnp.sum(p1, axis=-1, keepdims=True)
        a0 = e0 * a0 + _dot(p0.astype(BF16), v)
        a1 = e1 * a1 + _dot(p1.astype(BF16), v)
        return n0, n1, l0, l1, a0, a1

    def fast_step(start, carry):
        ls0, ls1, a0, a1 = carry
        v = v_scr[pl.ds(start, ck), :]
        p0 = jnp.exp2(_dot_nt(q2, k_scr[0, pl.ds(start, ck), :]))
        p1 = jnp.exp2(_dot_nt(q2, k_scr[1, pl.ds(start, ck), :]))
        return (ls0 + _lane_tile_sum(p0), ls1 + _lane_tile_sum(p1),
                a0 + _dot(p0.astype(BF16), v), a1 + _dot(p1.astype(BF16), v))

    def finish(l0, l1, a0, a1):
        lp = lp_ref[...]
        lam = (jnp.exp(jnp.sum(lp[0:1] * lp[1:2], axis=-1, keepdims=True))
               - jnp.exp(jnp.sum(lp[2:3] * lp[3:4], axis=-1, keepdims=True)) + lam_init)
        o = a0 / l0 - lam * (a1 / l1)
        var = jnp.mean(o * o, axis=-1, keepdims=True)
        o = o * lax.rsqrt(var + RMS_EPS) * og_ref[...] * (1.0 - lam_init)
        o_ref[...] = o.astype(o_ref.dtype)

    zeros = jnp.zeros((tq, LANES), F32)
    starts = [c * ck for c in range((t + n_ctx) // ck)]

    def attend_fast():
        carry = (zeros, zeros, zeros, zeros)
        for start in starts:
            carry = fast_step(start, carry)
        ls0, ls1, a0, a1 = carry
        finish(jnp.sum(ls0, axis=-1, keepdims=True), jnp.sum(ls1, axis=-1, keepdims=True), a0, a1)

    def attend_online():
        col = lambda val: jnp.full((tq, 1), val, F32)
        carry = (col(NEG), col(NEG), col(0.0), col(0.0), zeros, zeros)
        for start in starts:
            carry = step_fn(start, carry)
        finish(*carry[2:])

    fast_ok = bound_scr[0] <= FAST_SOFTMAX_BOUND
    pl.when(fast_ok)(attend_fast)
    pl.when(jnp.logical_not(fast_ok))(attend_online)


def _diff_call(z, *, b, t, tq, qn, kn, lparams, out_g, rope, cache, layer, kv_out, name):
    nq = t // tq
    n_ctx = PAST_LEN if cache is not None else 0
    ck = min(ATTN_CHUNK, t + n_ctx)
    qblk, kblk, vblk = _COL['qd'] // LANES, _COL['kd'] // LANES, _COL['vd'] // LANES
    const = lambda bb, h, i: (0, 0)
    in_specs = [pl.BlockSpec((tq, LANES), lambda bb, h, i: (bb * nq + i, qblk + h)),
                pl.BlockSpec((t, LANES), lambda bb, h, i: (bb, kblk + h)),
                pl.BlockSpec((t, LANES), lambda bb, h, i: (bb, vblk + h)),
                pl.BlockSpec((1, LANES), const),
                pl.BlockSpec((1, LANES), const),
                pl.BlockSpec((4, HEAD_DIM), const),
                pl.BlockSpec((1, LANES), const)]
    args = [z, z, z, jnp.tile(qn, 2).reshape(1, LANES), jnp.tile(kn, 2).reshape(1, LANES),
            lparams, out_g.reshape(1, LANES)]
    if rope is not None:
        cos, sin = rope
        in_specs += [pl.BlockSpec((tq, LANES), lambda bb, h, i: (i, 0)),
                     pl.BlockSpec((tq, LANES), lambda bb, h, i: (i, 0)),
                     pl.BlockSpec((t, LANES), const),
                     pl.BlockSpec((t, LANES), const)]
        args += [cos, sin, cos, sin]
    if cache is not None:
        spec = pl.BlockSpec((1, 1, n_ctx, LANES), lambda bb, h, i: (bb, layer, 0, h))
        in_specs += [spec, spec]
        args += [cache[0], cache[1]]
    out_shape = [jax.ShapeDtypeStruct((b * t, 512), BF16)]
    out_specs = [pl.BlockSpec((tq, LANES), lambda bb, h, i: (bb * nq + i, h))]
    aliases = {}
    if kv_out is not None:
        for n_out, arr in enumerate(kv_out):
            aliases[len(args)] = 1 + n_out
            in_specs.append(pl.BlockSpec(memory_space=pl.ANY))
            args.append(arr)
            out_shape.append(jax.ShapeDtypeStruct(arr.shape, arr.dtype))
            out_specs.append(pl.BlockSpec((1, 1, t, LANES), lambda bb, h, i: (bb, layer, 0, h)))
    lam_init = 0.8 - 0.6 * math.exp(-0.3 * layer)
    res = pl.pallas_call(
        functools.partial(_diff_kernel, t=t, tq=tq, n_ctx=n_ctx, has_rope=rope is not None,
                          emit_k=kv_out is not None, ck=ck, lam_init=lam_init),
        grid=(b, DIFF_HEADS, nq),
        in_specs=in_specs,
        out_specs=out_specs,
        out_shape=out_shape,
        input_output_aliases=aliases,
        scratch_shapes=[pltpu.VMEM((2, t + n_ctx, LANES), BF16), pltpu.VMEM((t + n_ctx, LANES), BF16),
                        pltpu.SMEM((1,), F32)],
        compiler_params=_cparams(),
        name=name,
    )(*args)
    return res[0], tuple(res[1:])


def _softplus(x):
    return jnp.maximum(x, 0.0) + jnp.log1p(jnp.exp(-jnp.abs(x)))


def _lru_kernel(*refs, tc, nc, has_h0, emit_final):
    it = iter(refs)
    xp_ref, xm_ref, xn_ref, ya_ref, cw_ref, cb_ref, wl_ref, bl_ref, lam_ref = (next(it) for _ in range(9))
    if has_h0:
        h0_ref = next(it)
    if emit_final:
        next(it)
    o_ref = next(it)
    if emit_final:
        fin_ref = next(it)
    hf_scr, a_scr, b_scr, carry_scr = (next(it) for _ in range(4))

    s = pl.program_id(1)
    fwd = s < nc
    c = jnp.where(fwd, s, 2 * nc - 1 - s)
    t0 = pl.multiple_of(c * tc, tc)

    @pl.when(s == 0)
    def _():
        if has_h0:
            carry_scr[...] = h0_ref[0, 0]
        else:
            carry_scr[...] = jnp.zeros_like(carry_scr)

    prev = jnp.where(c > 0, xp_ref[...], 0.0)
    nxt = jnp.where(c < nc - 1, xn_ref[...], 0.0)
    ext = jnp.concatenate([prev, xm_ref[...], nxt], axis=0)
    n_ext = tc + 2 * SUBLANES
    u = cb_ref[...] + cw_ref[1:2] * xm_ref[...]
    u = u + cw_ref[0:1] * pltpu.roll(ext, 1, 0)[SUBLANES:SUBLANES + tc]
    u = u + cw_ref[2:3] * pltpu.roll(ext, n_ext - 1, 0)[SUBLANES:SUBLANES + tc]
    u = u + cw_ref[3:4] * pltpu.roll(ext, n_ext - 2, 0)[SUBLANES:SUBLANES + tc]

    gates = _dot(u.astype(BF16), wl_ref[0]) + bl_ref[0]
    r = jax.nn.sigmoid(gates[:, :LRU_W])
    ig = jax.nn.sigmoid(gates[:, LRU_W:])
    log_a = -LRU_C * r * _softplus(-lam_ref[0])
    a = jnp.exp(log_a)
    a_scr[...] = a
    b_scr[...] = jnp.sqrt(-jnp.tanh(log_a) * (a * a + 1.0)) * ig * u

    ntile = tc // SUBLANES
    row = lax.broadcasted_iota(jnp.int32, (SUBLANES, LRU_W), 0)

    def scan(forward):
        def tile(i, carry):
            j = i if forward else ntile - 1 - i
            r0 = pl.multiple_of(j * SUBLANES, SUBLANES)
            a = a_scr[pl.ds(r0, SUBLANES), :]
            bv = b_scr[pl.ds(r0, SUBLANES), :]
            for d in (1, 2, 4):
                shift = d if forward else SUBLANES - d
                msk = (row >= d) if forward else (row < SUBLANES - d)
                ap = pltpu.roll(a, shift, 0)
                bp = pltpu.roll(bv, shift, 0)
                bv = jnp.where(msk, a * bp + bv, bv)
                a = jnp.where(msk, a * ap, a)
            h = a * carry + bv
            g0 = pl.multiple_of(t0 + r0, SUBLANES)
            if forward:
                hf_scr[pl.ds(g0, SUBLANES), :] = h
                return h[SUBLANES - 1:SUBLANES, :]
            b_scr[pl.ds(r0, SUBLANES), :] = h
            return h[0:1, :]

        idx = 0 if forward else 1
        last = lax.fori_loop(0, ntile, tile, carry_scr[idx:idx + 1, :])
        carry_scr[idx:idx + 1, :] = last
        if emit_final:
            fin_ref[0, 0, idx:idx + 1, :] = last
        if not forward:
            o_ref[...] = ((hf_scr[pl.ds(t0, tc), :] + b_scr[...])
                          * jax.nn.gelu(ya_ref[...])).astype(o_ref.dtype)

    pl.when(fwd)(lambda: scan(True))
    pl.when(jnp.logical_not(fwd))(lambda: scan(False))


def _lru_call(z, *, b, t, conv_w, conv_b, w_lru, b_lru, lam, h0, state_out, layer, name):
    tc = min(t, 512)
    nc = t // tc
    per8 = tc // SUBLANES
    nrow8 = b * t // SUBLANES

    def chunk(s):
        return jnp.where(s < nc, s, 2 * nc - 1 - s)

    def hold(s):
        return jnp.where(s < nc, nc - 1, 2 * nc - 1 - s)

    in_specs = [
        pl.BlockSpec((SUBLANES, LRU_W),
                     lambda bb, s: (jnp.maximum((bb * nc + chunk(s)) * per8 - 1, 0), 0)),
        pl.BlockSpec((tc, LRU_W), lambda bb, s: (bb * nc + chunk(s), 0)),
        pl.BlockSpec((SUBLANES, LRU_W),
                     lambda bb, s: (jnp.minimum((bb * nc + chunk(s) + 1) * per8, nrow8 - 1), 0)),
        pl.BlockSpec((tc, LRU_W), lambda bb, s: (bb * nc + hold(s), 1)),
        pl.BlockSpec((CONV_W, LRU_W), lambda bb, s: (0, 0)),
        pl.BlockSpec((1, LRU_W), lambda bb, s: (0, 0)),
        pl.BlockSpec((1, LRU_W, 2 * LRU_W), lambda bb, s: (s // nc, 0, 0)),
        pl.BlockSpec((1, 1, 2 * LRU_W), lambda bb, s: (s // nc, 0, 0)),
        pl.BlockSpec((1, 1, LRU_W), lambda bb, s: (s // nc, 0, 0)),
    ]
    args = [z, z, z, z, conv_w, conv_b.reshape(1, LRU_W), w_lru, b_lru, lam.reshape(2, 1, LRU_W)]
    if h0 is not None:
        in_specs.append(pl.BlockSpec((1, 1, 2, LRU_W), lambda bb, s: (bb, layer, 0, 0)))
        args.append(h0)
    out_shape = [jax.ShapeDtypeStruct((b * t, LRU_W), BF16)]
    out_specs = [pl.BlockSpec((tc, LRU_W), lambda bb, s: (bb * nc + hold(s), 0))]
    aliases = {}
    if state_out is not None:
        aliases[len(args)] = 1
        in_specs.append(pl.BlockSpec(memory_space=pl.ANY))
        args.append(state_out)
        out_shape.append(jax.ShapeDtypeStruct(state_out.shape, state_out.dtype))
        out_specs.append(pl.BlockSpec((1, 1, 2, LRU_W), lambda bb, s: (bb, layer, 0, 0)))
    res = pl.pallas_call(
        functools.partial(_lru_kernel, tc=tc, nc=nc, has_h0=h0 is not None,
                          emit_final=state_out is not None),
        grid=(b, 2 * nc),
        in_specs=in_specs,
        out_specs=out_specs,
        out_shape=out_shape,
        input_output_aliases=aliases,
        scratch_shapes=[pltpu.VMEM((t, LRU_W), F32), pltpu.VMEM((tc, LRU_W), F32),
                        pltpu.VMEM((tc, LRU_W), F32), pltpu.VMEM((2, LRU_W), F32)],
        compiler_params=_cparams(),
        name=name,
    )(*args)
    return res[0], (res[1] if state_out is not None else None)


def _merge_kernel(x_ref, gt_ref, a_ref, w_ref, g_ref, d_ref, wb_ref, wo_ref, mod_ref, o_ref):
    merged = None
    for bi, br_ref in enumerate((a_ref, w_ref, g_ref, d_ref)):
        gate = gt_ref[:, bi * D_MODEL:(bi + 1) * D_MODEL].astype(F32)
        term = gate * _dot(br_ref[...], wb_ref[0, bi])
        merged = term if merged is None else merged + term
    o_ref[...] = x_ref[...] + mod_ref[0][2:3] * _dot(merged.astype(BF16), wo_ref[0])


def _merge_call(x, gates, branches, w_branch, w_o, layer, mod, latent, name):
    m = x.shape[0]
    tm = 256
    row = lambda i: (i, 0)
    resident = pl.Buffered(1)
    in_specs = [pl.BlockSpec((tm, D_MODEL), row),
                pl.BlockSpec((tm, N_BRANCH * D_MODEL), row)]
    in_specs += [pl.BlockSpec((tm, BRANCH_W), row)] * N_BRANCH
    in_specs += [pl.BlockSpec((1, N_BRANCH, BRANCH_W, D_MODEL), lambda i: (layer, 0, 0, 0),
                              pipeline_mode=resident),
                 pl.BlockSpec((1, D_MODEL, D_MODEL), lambda i: (layer, 0, 0), pipeline_mode=resident),
                 pl.BlockSpec((1, N_MOD, D_MODEL), _mod_index(latent, tm))]
    return pl.pallas_call(
        _merge_kernel,
        grid=(m // tm,),
        in_specs=in_specs,
        out_specs=pl.BlockSpec((tm, D_MODEL), row),
        out_shape=jax.ShapeDtypeStruct((m, D_MODEL), F32),
        compiler_params=_cparams(),
        name=name,
    )(x, gates, *branches, w_branch, w_o, mod)


def _ffn_kernel(x_ref, mod_ref, g_ref, w1_ref, b1_ref, w2_ref, b2_ref, o_ref, h_scr, acc_scr):
    j = pl.program_id(1)

    @pl.when(j == 0)
    def _():
        h_scr[...] = _norm_mod(x_ref[...], g_ref[...], mod_ref[0], 3, 4).astype(BF16)
        acc_scr[...] = jnp.zeros_like(acc_scr)

    a = jnp.maximum(_dot(h_scr[...], w1_ref[0]) + b1_ref[...], 0.0)
    acc_scr[...] += _dot((a * a).astype(BF16), w2_ref[0])

    @pl.when(j == pl.num_programs(1) - 1)
    def _():
        o_ref[...] = x_ref[...] + mod_ref[0][5:6] * (acc_scr[...] + b2_ref[...])


def _ffn_call(x, mod, g, w1, b1, w2, b2, layer, latent, name):
    m = x.shape[0]
    tm, tf = 512, 1024
    return pl.pallas_call(
        _ffn_kernel,
        grid=(m // tm, D_FF // tf),
        in_specs=[pl.BlockSpec((tm, D_MODEL), lambda i, j: (i, 0)),
                  pl.BlockSpec((1, N_MOD, D_MODEL), _mod_index(latent, tm)),
                  pl.BlockSpec((1, D_MODEL), lambda i, j: (0, 0)),
                  pl.BlockSpec((1, D_MODEL, tf), lambda i, j: (layer, 0, j)),
                  pl.BlockSpec((1, tf), lambda i, j: (0, j)),
                  pl.BlockSpec((1, tf, D_MODEL), lambda i, j: (layer, j, 0)),
                  pl.BlockSpec((1, D_MODEL), lambda i, j: (0, 0))],
        out_specs=pl.BlockSpec((tm, D_MODEL), lambda i, j: (i, 0)),
        out_shape=jax.ShapeDtypeStruct((m, D_MODEL), F32),
        scratch_shapes=[pltpu.VMEM((tm, D_MODEL), BF16), pltpu.VMEM((tm, D_MODEL), F32)],
        compiler_params=_cparams(),
        name=name,
    )(x, mod, g.reshape(1, D_MODEL), w1, b1.reshape(1, D_FF), w2, b2.reshape(1, D_MODEL))


def _block_diag(w):
    eye = jnp.eye(LRU_BLOCKS, dtype=w.dtype)
    return jnp.einsum('ncd,nm->ncmd', w, eye).reshape(LRU_W, LRU_W)


def _trunk_layer(x, mod, wts, layer, latent, rope, cached, new):
    b, t = (DEC_BATCH, DEC_SEQ) if latent else (BATCH, SEQ)
    tag = ('lat' if latent else 'ctx') + str(layer)
    z = _normmod_matmul(x, mod, wts['norm1_g'], wts['w_in'], layer, None, None, F32, latent,
                        'inproj_' + tag)
    gates = _normmod_matmul(x, mod, wts['norm1_g'], wts['w_gate'], layer, wts['b_gate'], 'sigmoid', BF16,
                            latent, 'gates_' + tag)
    a_out, new_state = _lru_call(
        z, b=b, t=t, conv_w=wts['conv_w'], conv_b=wts['conv_b'], w_lru=wts['w_lru'], b_lru=wts['b_lru'],
        lam=wts['lru_lambda'], h0=cached[6] if latent else None, state_out=None if latent else new[6],
        layer=layer, name='lru_' + tag)
    w_out, new_win = _gqa_call(
        z, b=b, t=t, tq=256, qname='qw', kname='kw', vname='vw',
        qn=wts['win_qn'], kn=wts['win_kn'], sink=wts['win_sink'], rope=rope,
        cache=(cached[0], cached[1]) if latent else None, layer=layer, banded=latent,
        kv_out=None if latent else new[0:2], name='win_' + tag)
    g_out, new_grid = _gqa_call(
        z, b=b, t=t, tq=256, qname='qg', kname='kg', vname='vg',
        qn=wts['grid_qn'], kn=wts['grid_kn'], sink=None, rope=rope,
        cache=(cached[2], cached[3]) if latent else None, layer=layer, banded=False,
        kv_out=None if latent else new[2:4], name='grid_' + tag)
    d_out, new_diff = _diff_call(
        z, b=b, t=t, tq=256, qn=wts['diff_qn'], kn=wts['diff_kn'], lparams=wts['diff_lp'],
        out_g=wts['diff_out_g'], rope=rope, cache=(cached[4], cached[5]) if latent else None,
        layer=layer, kv_out=None if latent else new[4:6], name='diff_' + tag)
    x = _merge_call(x, gates, (a_out, w_out, g_out, d_out), wts['w_branch'], wts['w_o'], layer, mod,
                    latent, 'merge_' + tag)
    x = _ffn_call(x, mod, wts['norm2_g'], wts['w_ff1'], wts['b_ff1'], wts['w_ff2'], wts['b_ff2'],
                  layer, latent, 'ffn_' + tag)
    if latent:
        return x, None
    return x, new_win + new_grid + new_diff + (new_state,)


def kernel(x_prompt, x_sample, cache_win_k, cache_win_v, cache_grid_k, cache_grid_v, cache_diff_k, cache_diff_v, state_lru, c, c_ctx, w_ada, b_ada, norm1_g, norm2_g, w_in, conv_w, conv_b, lru_wr, lru_br, lru_wi, lru_bi, lru_lambda, win_qn, win_kn, win_sink, grid_qn, grid_kn, diff_qn, diff_kn, diff_lq1, diff_lk1, diff_lq2, diff_lk2, diff_out_g, w_branch, w_gate, b_gate, w_o, w_ff1, b_ff1, w_ff2, b_ff2):
    cond = jnp.zeros((MOD_ROWS, D_MODEL), F32).at[0].set(c_ctx).at[1:1 + DEC_BATCH].set(c)
    mod_all = _modulation(cond, w_ada, b_ada).reshape(DEPTH, MOD_ROWS, N_MOD, D_MODEL)
    rope = _rope_tables(DEC_SEQ)
    cached_all = (cache_win_k.reshape(DEC_BATCH, DEPTH, PAST_LEN, LANES),
                  cache_win_v.reshape(DEC_BATCH, DEPTH, PAST_LEN, LANES),
                  cache_grid_k.reshape(DEC_BATCH, DEPTH, PAST_LEN, LANES),
                  cache_grid_v.reshape(DEC_BATCH, DEPTH, PAST_LEN, LANES),
                  cache_diff_k.reshape(DEC_BATCH, DEPTH, PAST_LEN, 512),
                  cache_diff_v.reshape(DEC_BATCH, DEPTH, PAST_LEN, 512))
    w_in_b, w_gate_b, w_branch_b, w_o_b, w_ff1_b, w_ff2_b = (
        w.astype(BF16) for w in (w_in, w_gate, w_branch, w_o, w_ff1, w_ff2))
    y_p = x_prompt.reshape(BATCH * SEQ, D_MODEL)
    y_s = x_sample.reshape(DEC_BATCH * DEC_SEQ, D_MODEL)
    new = tuple(jnp.zeros((BATCH, DEPTH, SEQ, w), F32) for w in (LANES,) * 4 + (512,) * 2)
    new += (jnp.zeros((BATCH, DEPTH, 2, LRU_W), F32),)
    for l in range(DEPTH):
        w_lru = jnp.stack([
            jnp.concatenate([_block_diag(lru_wr[l, k]), _block_diag(lru_wi[l, k])], axis=1)
            for k in range(2)]).astype(BF16)
        b_lru = jnp.concatenate([lru_br[l], lru_bi[l]], axis=-1).reshape(2, 1, 2 * LRU_W)
        wts = {
            'norm1_g': norm1_g[l], 'norm2_g': norm2_g[l],
            'w_in': w_in_b, 'w_gate': w_gate_b, 'b_gate': b_gate[l],
            'conv_w': conv_w[l], 'conv_b': conv_b[l], 'w_lru': w_lru, 'b_lru': b_lru,
            'lru_lambda': lru_lambda[l],
            'win_qn': win_qn[l], 'win_kn': win_kn[l], 'win_sink': win_sink[l],
            'grid_qn': grid_qn[l], 'grid_kn': grid_kn[l],
            'diff_qn': diff_qn[l], 'diff_kn': diff_kn[l],
            'diff_lp': jnp.stack([diff_lq1[l], diff_lk1[l], diff_lq2[l], diff_lk2[l]]),
            'diff_out_g': diff_out_g[l],
            'w_branch': w_branch_b, 'w_o': w_o_b,
            'w_ff1': w_ff1_b, 'b_ff1': b_ff1[l], 'w_ff2': w_ff2_b, 'b_ff2': b_ff2[l],
        }
        y_p, new = _trunk_layer(y_p, mod_all[l], wts, l, False, None, None, new)
        y_s, _ = _trunk_layer(y_s, mod_all[l], wts, l, True, rope, cached_all + (state_lru,), None)

    kv_shape = (BATCH, DEPTH, SEQ, WIN_KV, HEAD_DIM)
    return (y_p.reshape(BATCH, SEQ, D_MODEL),
            y_s.reshape(DEC_BATCH, DEC_SEQ, D_MODEL),
            new[0].reshape(kv_shape), new[1].reshape(kv_shape),
            new[2].reshape(kv_shape), new[3].reshape(kv_shape),
            new[4].reshape(BATCH, DEPTH, SEQ, DIFF_HEADS, 2, HEAD_DIM),
            new[5].reshape(BATCH, DEPTH, SEQ, DIFF_HEADS, 2 * HEAD_DIM),
            new[6])
```

```python
import functools
import math

import jax
import jax.numpy as jnp
import numpy as np
from jax import lax
from jax.experimental import pallas as pl
from jax.experimental.pallas import tpu as pltpu

F32 = jnp.float32
BF16 = jnp.bfloat16

D_MODEL = 2048
BATCH = 32
SEQ = 256
DEPTH = 4
DEC_BATCH = 4
DEC_SEQ = 4096
PAST_LEN = 512
GRID_W = 64
BLOCK = 128
HEAD_DIM = 64
N_FREQ = HEAD_DIM // 4
ROPE_BASE = 10000.0
ATTN_SCALE = HEAD_DIM ** -0.5
RMS_EPS = 1e-6
N_MOD = 6
N_BRANCH = 4
BRANCH_W = D_MODEL // 4
LRU_W = BRANCH_W
LRU_BLOCKS = 8
LRU_BW = LRU_W // LRU_BLOCKS
LRU_C = 8.0
CONV_W = 4
WIN_HEADS = 8
WIN_KV = 2
DIFF_HEADS = 4
D_FF = 4 * D_MODEL
D_IN = 4096

V7X_VMEM_BYTES = 64 * 1024 * 1024
VMEM_LIMIT = V7X_VMEM_BYTES - 8 * 1024 * 1024
LANES = 128
SUBLANES = 8
NEG = -1e30
LOG2E = math.log2(math.e)
ATTN_CHUNK = 1536
FAST_SOFTMAX_BOUND = 40.0
MOD_ROWS = 8

_COL = dict(xa=0, ya=512, qw=1024, kw=1536, vw=1664, qg=1792, kg=2304, vg=2432, qd=2560, kd=3072,
            vd=3584)


def _cparams():
    return pltpu.CompilerParams(vmem_limit_bytes=VMEM_LIMIT)


def _dot(a, b):
    return jnp.dot(a, b, preferred_element_type=F32)


def _dot_nt(a, b):
    return lax.dot_general(a, b, (((1,), (1,)), ((), ())), preferred_element_type=F32)


def _mod_kernel(c_ref, w_ref, b_ref, o_ref):
    c = c_ref[...]
    s = (c * jax.nn.sigmoid(c)).astype(BF16)
    o_ref[0] = _dot(s, w_ref[0].astype(BF16)) + b_ref[0]


def _modulation(cond, w_ada, b_ada):
    tn = 1024
    n = N_MOD * D_MODEL
    return pl.pallas_call(
        _mod_kernel,
        grid=(DEPTH, n // tn),
        in_specs=[pl.BlockSpec((MOD_ROWS, D_MODEL), lambda l, j: (0, 0)),
                  pl.BlockSpec((1, D_MODEL, tn), lambda l, j: (l, 0, j)),
                  pl.BlockSpec((1, 1, tn), lambda l, j: (l, 0, j))],
        out_specs=pl.BlockSpec((1, MOD_ROWS, tn), lambda l, j: (l, 0, j)),
        out_shape=jax.ShapeDtypeStruct((DEPTH, MOD_ROWS, n), F32),
        compiler_params=_cparams(),
        name='modulation',
    )(cond, w_ada, b_ada.reshape(DEPTH, 1, n))


def _mod_index(latent, tm):
    if latent:
        per = DEC_SEQ // tm
        return lambda i, *_: (1 + i // per, 0, 0)
    return lambda i, *_: (0, 0, 0)


def _norm_mod(x, g, mod, shift_idx, scale_idx):
    var = jnp.mean(x * x, axis=-1, keepdims=True)
    y = x * lax.rsqrt(var + RMS_EPS) * g
    return y * (1.0 + mod[scale_idx:scale_idx + 1]) + mod[shift_idx:shift_idx + 1]


def _gates_kernel(x_ref, mod_ref, g_ref, w_ref, b_ref, o_ref, h_ref):
    @pl.when(pl.program_id(1) == 0)
    def _():
        h_ref[...] = _norm_mod(x_ref[...], g_ref[...], mod_ref[0], 0, 1).astype(BF16)

    o_ref[...] = jax.nn.sigmoid(_dot(h_ref[...], w_ref[0]) + b_ref[...]).astype(o_ref.dtype)


def _gates_call(x, mod, g, w, layer, bias, latent, name):
    m, n = x.shape[0], w.shape[2]
    tm, tn = 1024, 1024
    return pl.pallas_call(
        _gates_kernel,
        grid=(m // tm, n // tn),
        in_specs=[pl.BlockSpec((tm, D_MODEL), lambda i, j: (i, 0)),
                  pl.BlockSpec((1, N_MOD, D_MODEL), _mod_index(latent, tm)),
                  pl.BlockSpec((1, D_MODEL), lambda i, j: (0, 0)),
                  pl.BlockSpec((1, D_MODEL, tn), lambda i, j: (layer, 0, j)),
                  pl.BlockSpec((1, tn), lambda i, j: (0, j))],
        out_specs=[pl.BlockSpec((tm, tn), lambda i, j: (i, j)),
                   pl.BlockSpec((tm, D_MODEL), lambda i, j: (i, 0))],
        out_shape=[jax.ShapeDtypeStruct((m, n), BF16), jax.ShapeDtypeStruct((m, D_MODEL), BF16)],
        compiler_params=_cparams(),
        name=name,
    )(x, mod, g.reshape(1, D_MODEL), w, bias.reshape(1, n))


def _inproj_kernel(h_ref, w_ref, o_ref):
    o_ref[...] = _dot(h_ref[...], w_ref[0])


def _inproj_call(h, w, layer, name):
    m, n = h.shape[0], w.shape[2]
    tm, tn = 1024, 2048
    return pl.pallas_call(
        _inproj_kernel,
        grid=(m // tm, n // tn),
        in_specs=[pl.BlockSpec((tm, D_MODEL), lambda i, j: (i, 0)),
                  pl.BlockSpec((1, D_MODEL, tn), lambda i, j: (layer, 0, j))],
        out_specs=pl.BlockSpec((tm, tn), lambda i, j: (i, j)),
        out_shape=jax.ShapeDtypeStruct((m, n), F32),
        compiler_params=_cparams(),
        name=name,
    )(h, w)


def _lane_lo():
    return lax.broadcasted_iota(jnp.int32, (1, LANES), 1) < HEAD_DIM


def _seg_matrix():
    r = lax.broadcasted_iota(jnp.int32, (LANES, LANES), 0) // HEAD_DIM
    c = lax.broadcasted_iota(jnp.int32, (LANES, LANES), 1) // HEAD_DIM
    return jnp.where(r == c, 1.0, 0.0).astype(BF16)


def _head_rmsnorm(x, gain):
    x2 = x * x
    hi = x2.astype(BF16)
    lo = (x2 - hi.astype(F32)).astype(BF16)
    seg = _seg_matrix()
    ms = (_dot(hi, seg) + _dot(lo, seg)) * (1.0 / HEAD_DIM)
    return x * lax.rsqrt(ms + RMS_EPS) * gain


def _rope(x, cos, sin_signed):
    lane = lax.broadcasted_iota(jnp.int32, (1, LANES), 1)
    first = (lane & (2 * N_FREQ - 1)) < N_FREQ
    up = pltpu.roll(x, LANES - N_FREQ, 1)
    dn = pltpu.roll(x, N_FREQ, 1)
    return x * cos + jnp.where(first, up, dn) * sin_signed


def _score_bound(qn_ref, kn_ref, cached_k):
    root = math.sqrt(HEAD_DIM)
    qmax = root * jnp.max(jnp.abs(qn_ref[...]))
    kmax = root * jnp.max(jnp.abs(kn_ref[...]))
    if cached_k is not None:
        kmax = jnp.maximum(kmax, jnp.sqrt(jnp.max(jnp.sum(cached_k * cached_k, axis=-1, keepdims=True))))
    return qmax * kmax * (ATTN_SCALE * LOG2E)


def _lane_tile_sum(p):
    acc = p[:, 0:LANES]
    for j in range(1, p.shape[1] // LANES):
        acc = acc + p[:, j * LANES:(j + 1) * LANES]
    return acc


def _rope_tables(t):
    pos = jnp.arange(t)
    row = (pos // GRID_W).astype(F32)
    col = (pos % GRID_W).astype(F32)
    inv = ROPE_BASE ** (-jnp.arange(N_FREQ, dtype=F32) / N_FREQ)
    ar, ac = row[:, None] * inv, col[:, None] * inv
    cos = jnp.concatenate([jnp.cos(ar), jnp.cos(ar), jnp.cos(ac), jnp.cos(ac)], axis=-1)
    sin = jnp.concatenate([-jnp.sin(ar), jnp.sin(ar), -jnp.sin(ac), jnp.sin(ac)], axis=-1)
    return jnp.tile(cos, (1, 2)), jnp.tile(sin, (1, 2))


def _gqa_kernel(*refs, t, tq, n_ctx, banded, has_sink, has_rope, emit_k, ck):
    it = iter(refs)
    q_ref, k_ref, v_ref, qn_ref, kn_ref = (next(it) for _ in range(5))
    if has_rope:
        cq_ref, sq_ref, cka_ref, ska_ref = (next(it) for _ in range(4))
    if n_ctx:
        kc_ref, vc_ref = next(it), next(it)
    if has_sink:
        sink_ref = next(it)
    if emit_k:
        next(it), next(it)
    o_ref = next(it)
    if emit_k:
        ko_ref, vo_ref = next(it), next(it)
    k_scr, v_scr, bound_scr = next(it), next(it), next(it)

    g = pl.program_id(1)
    qi = pl.program_id(2)
    lo = _lane_lo()
    lat0 = BLOCK if banded else 0
    ctx0 = t + 2 * lat0
    first_head = g == 0

    def put(dst, kx, vx):
        kr = pltpu.roll(kx, HEAD_DIM, 1)
        vr = pltpu.roll(vx, HEAD_DIM, 1)
        zero = jnp.zeros_like(kx)
        k_scr[0, dst, :] = jnp.where(lo, jnp.where(first_head, kx, kr), zero).astype(BF16)
        k_scr[1, dst, :] = jnp.where(lo, zero, jnp.where(first_head, kr, kx)).astype(BF16)
        v_scr[0, dst, :] = jnp.where(lo, jnp.where(first_head, vx, vr), zero).astype(BF16)
        v_scr[1, dst, :] = jnp.where(lo, zero, jnp.where(first_head, vr, vx)).astype(BF16)

    @pl.when(qi == 0)
    def _build():
        step = min(t, 512)
        for r in range(0, t, step):
            kx = _head_rmsnorm(k_ref[r:r + step, :], kn_ref[...])
            if emit_k:
                ko_ref[0, 0, r:r + step, :] = kx
                vo_ref[0, 0, r:r + step, :] = v_ref[r:r + step, :]
            if has_rope:
                kx = _rope(kx, cka_ref[r:r + step, :], ska_ref[r:r + step, :])
            put(slice(lat0 + r, lat0 + r + step), kx, v_ref[r:r + step, :])
        if banded:
            zpad = jnp.zeros((BLOCK, LANES), BF16)
            for scr in (k_scr, v_scr):
                for var in range(2):
                    scr[var, 0:BLOCK, :] = zpad
                    scr[var, lat0 + t:lat0 + t + BLOCK, :] = zpad
        if n_ctx:
            put(slice(ctx0, ctx0 + n_ctx), kc_ref[0, 0], vc_ref[0, 0])
        bound = _score_bound(qn_ref, kn_ref, kc_ref[0, 0] if n_ctx else None)
        if has_sink:
            for h in range(WIN_HEADS):
                bound = jnp.maximum(bound, jnp.abs(sink_ref[h]) * LOG2E)
        bound_scr[0] = bound

    def step_fn(q2, segs, carry):
        m0, m1, l, acc = carry

        def scores(var):
            parts = []
            for start, size, mask in segs:
                s = _dot_nt(q2, k_scr[var, pl.ds(start, size), :])
                parts.append(s if mask is None else jnp.where(mask, s, NEG))
            return parts[0] if len(parts) == 1 else jnp.concatenate(parts, axis=1)

        s0, s1 = scores(0), scores(1)
        n0 = jnp.maximum(m0, jnp.max(s0, axis=-1, keepdims=True))
        n1 = jnp.maximum(m1, jnp.max(s1, axis=-1, keepdims=True))
        p0 = jnp.exp2(s0 - n0)
        p1 = jnp.exp2(s1 - n1)
        alpha = jnp.where(lo, jnp.exp2(m0 - n0), jnp.exp2(m1 - n1))
        rs = jnp.where(lo, jnp.sum(p0, axis=-1, keepdims=True), jnp.sum(p1, axis=-1, keepdims=True))
        l = alpha * l + rs
        acc = alpha * acc
        p0, p1 = p0.astype(BF16), p1.astype(BF16)
        off = 0
        for start, size, _ in segs:
            acc = (acc + _dot(p0[:, off:off + size], v_scr[0, pl.ds(start, size), :])
                   + _dot(p1[:, off:off + size], v_scr[1, pl.ds(start, size), :]))
            off += size
        return n0, n1, l, acc

    def fast_step(q2, segs, carry):
        ls0, ls1, acc = carry
        for start, size, mask in segs:
            s0 = _dot_nt(q2, k_scr[0, pl.ds(start, size), :])
            s1 = _dot_nt(q2, k_scr[1, pl.ds(start, size), :])
            if mask is not None:
                s0 = jnp.where(mask, s0, NEG)
                s1 = jnp.where(mask, s1, NEG)
            p0 = jnp.exp2(s0)
            p1 = jnp.exp2(s1)
            ls0 = ls0 + _lane_tile_sum(p0)
            ls1 = ls1 + _lane_tile_sum(p1)
            acc = (acc + _dot(p0.astype(BF16), v_scr[0, pl.ds(start, size), :])
                   + _dot(p1.astype(BF16), v_scr[1, pl.ds(start, size), :]))
        return ls0, ls1, acc

    if banded:
        span = tq + 2 * BLOCK
        rr = lax.broadcasted_iota(jnp.int32, (tq, span), 0)
        cc = lax.broadcasted_iota(jnp.int32, (tq, span), 1)
        kpos = qi * tq - BLOCK + cc
        band_mask = (cc >= rr) & (cc - rr <= 2 * BLOCK) & (kpos >= 0) & (kpos < t)

    def attend(fast):
        q = q_ref[...]
        for p in range(2):
            qp = _head_rmsnorm(q[:, p * LANES:(p + 1) * LANES], qn_ref[...])
            if has_rope:
                qp = _rope(qp, cq_ref[...], sq_ref[...])
            q2 = (qp * (ATTN_SCALE * LOG2E)).astype(BF16)
            if banded:
                chunks = [[(pl.multiple_of(qi * tq, tq), span, band_mask)]]
                if n_ctx:
                    chunks[0].append((ctx0, n_ctx, None))
            else:
                chunks = [[(c * ck, ck, None)] for c in range((t + n_ctx) // ck)]
            if has_sink:
                h0 = g * 4 + 2 * p
                sink0 = jnp.full((1, 1), sink_ref[h0] * LOG2E, F32)
                sink1 = jnp.full((1, 1), sink_ref[h0 + 1] * LOG2E, F32)
            zeros = jnp.zeros((tq, LANES), F32)
            if fast:
                carry = (zeros, zeros, zeros)
                for segs in chunks:
                    carry = fast_step(q2, segs, carry)
                ls0, ls1, acc = carry
                l = jnp.where(lo, jnp.sum(ls0, axis=-1, keepdims=True),
                              jnp.sum(ls1, axis=-1, keepdims=True))
                if has_sink:
                    l = l + jnp.where(lo, jnp.exp2(sink0), jnp.exp2(sink1))
            else:
                if has_sink:
                    m0 = jnp.broadcast_to(sink0, (tq, 1))
                    m1 = jnp.broadcast_to(sink1, (tq, 1))
                    l = jnp.ones((tq, LANES), F32)
                else:
                    m0 = jnp.full((tq, 1), NEG, F32)
                    m1 = jnp.full((tq, 1), NEG, F32)
                    l = zeros
                carry = (m0, m1, l, zeros)
                for segs in chunks:
                    carry = step_fn(q2, segs, carry)
                _, _, l, acc = carry
            o_ref[:, p * LANES:(p + 1) * LANES] = (acc / l).astype(o_ref.dtype)

    fast_ok = bound_scr[0] <= FAST_SOFTMAX_BOUND
    pl.when(fast_ok)(lambda: attend(True))
    pl.when(jnp.logical_not(fast_ok))(lambda: attend(False))


def _gqa_call(z, *, b, t, tq, qname, kname, vname, qn, kn, sink, rope, cache, layer, banded,
              kv_out, name):
    nq = t // tq
    n_ctx = PAST_LEN if cache is not None else 0
    rows = t + n_ctx + (2 * BLOCK if banded else 0)
    ck = min(ATTN_CHUNK, t + n_ctx)
    qblk, kblk, vblk = _COL[qname] // 256, _COL[kname] // LANES, _COL[vname] // LANES
    const = lambda bb, g, i: (0, 0)
    in_specs = [pl.BlockSpec((tq, 256), lambda bb, g, i: (bb * nq + i, qblk + g)),
                pl.BlockSpec((t, LANES), lambda bb, g, i: (bb, kblk)),
                pl.BlockSpec((t, LANES), lambda bb, g, i: (bb, vblk)),
                pl.BlockSpec((1, LANES), const),
                pl.BlockSpec((1, LANES), const)]
    args = [z, z, z, jnp.tile(qn, 2).reshape(1, LANES), jnp.tile(kn, 2).reshape(1, LANES)]
    if rope is not None:
        cos, sin = rope
        in_specs += [pl.BlockSpec((tq, LANES), lambda bb, g, i: (i, 0)),
                     pl.BlockSpec((tq, LANES), lambda bb, g, i: (i, 0)),
                     pl.BlockSpec((t, LANES), const),
                     pl.BlockSpec((t, LANES), const)]
        args += [cos, sin, cos, sin]
    if cache is not None:
        spec = pl.BlockSpec((1, 1, n_ctx, LANES), lambda bb, g, i: (bb, layer, 0, 0))
        in_specs += [spec, spec]
        args += [cache[0], cache[1]]
    if sink is not None:
        in_specs.append(pl.BlockSpec(memory_space=pltpu.SMEM))
        args.append(sink)
    out_shape = [jax.ShapeDtypeStruct((b * t, 512), BF16)]
    out_specs = [pl.BlockSpec((tq, 256), lambda bb, g, i: (bb * nq + i, g))]
    aliases = {}
    if kv_out is not None:
        for n_out, arr in enumerate(kv_out):
            aliases[len(args)] = 1 + n_out
            in_specs.append(pl.BlockSpec(memory_space=pl.ANY))
            args.append(arr)
            out_shape.append(jax.ShapeDtypeStruct(arr.shape, arr.dtype))
            out_specs.append(pl.BlockSpec((1, 1, t, LANES), lambda bb, g, i: (bb, layer, 0, 0)))
    res = pl.pallas_call(
        functools.partial(_gqa_kernel, t=t, tq=tq, n_ctx=n_ctx, banded=banded,
                          has_sink=sink is not None, has_rope=rope is not None,
                          emit_k=kv_out is not None, ck=ck),
        grid=(b, 2, nq),
        in_specs=in_specs,
        out_specs=out_specs,
        out_shape=out_shape,
        input_output_aliases=aliases,
        scratch_shapes=[pltpu.VMEM((2, rows, LANES), BF16), pltpu.VMEM((2, rows, LANES), BF16),
                        pltpu.SMEM((1,), F32)],
        compiler_params=_cparams(),
        name=name,
    )(*args)
    return res[0], tuple(res[1:])


def _diff_kernel(*refs, t, tq, n_ctx, has_rope, emit_k, ck, lam_init):
    it = iter(refs)
    q_ref, k_ref, v_ref, qn_ref, kn_ref, lp_ref, og_ref = (next(it) for _ in range(7))
    if has_rope:
        cq_ref, sq_ref, cka_ref, ska_ref = (next(it) for _ in range(4))
    if n_ctx:
        kc_ref, vc_ref = next(it), next(it)
    if emit_k:
        next(it), next(it)
    o_ref = next(it)
    if emit_k:
        ko_ref, vo_ref = next(it), next(it)
    k_scr, v_scr, bound_scr = next(it), next(it), next(it)

    qi = pl.program_id(2)
    lo = _lane_lo()

    def put(dst, kx, vx):
        zero = jnp.zeros_like(kx)
        k_scr[0, dst, :] = jnp.where(lo, kx, zero).astype(BF16)
        k_scr[1, dst, :] = jnp.where(lo, zero, kx).astype(BF16)
        v_scr[dst, :] = vx.astype(BF16)

    @pl.when(qi == 0)
    def _build():
        step = min(t, 512)
        for r in range(0, t, step):
            kx = _head_rmsnorm(k_ref[r:r + step, :], kn_ref[...])
            if emit_k:
                ko_ref[0, 0, r:r + step, :] = kx
                vo_ref[0, 0, r:r + step, :] = v_ref[r:r + step, :]
            if has_rope:
                kx = _rope(kx, cka_ref[r:r + step, :], ska_ref[r:r + step, :])
            put(slice(r, r + step), kx, v_ref[r:r + step, :])
        if n_ctx:
            put(slice(t, t + n_ctx), kc_ref[0, 0], vc_ref[0, 0])
        bound_scr[0] = _score_bound(qn_ref, kn_ref, kc_ref[0, 0] if n_ctx else None)

    qp = _head_rmsnorm(q_ref[...], qn_ref[...])
    if has_rope:
        qp = _rope(qp, cq_ref[...], sq_ref[...])
    q2 = (qp * (ATTN_SCALE * LOG2E)).astype(BF16)

    def step_fn(start, carry):
        m0, m1, l0, l1, a0, a1 = carry
        v = v_scr[pl.ds(start, ck), :]
        s0 = _dot_nt(q2, k_scr[0, pl.ds(start, ck), :])
        s1 = _dot_nt(q2, k_scr[1, pl.ds(start, ck), :])
        n0 = jnp.maximum(m0, jnp.max(s0, axis=-1, keepdims=True))
        n1 = jnp.maximum(m1, jnp.max(s1, axis=-1, keepdims=True))
        p0 = jnp.exp2(s0 - n0)
        p1 = jnp.exp2(s1 - n1)
        e0 = jnp.exp2(m0 - n0)
        e1 = jnp.exp2(m1 - n1)
        l0 = e0 * l0 + jnp.sum(p0, axis=-1, keepdims=True)
        l1 = e1 * l1 + jnp.sum(p1, axis=-1, keepdims=True)
        a0 = e0 * a0 + _dot(p0.astype(BF16), v)
        a1 = e1 * a1 + _dot(p1.astype(BF16), v)
        return n0, n1, l0, l1, a0, a1

    def fast_step(start, carry):
        ls0, ls1, a0, a1 = carry
        v = v_scr[pl.ds(start, ck), :]
        p0 = jnp.exp2(_dot_nt(q2, k_scr[0, pl.ds(start, ck), :]))
        p1 = jnp.exp2(_dot_nt(q2, k_scr[1, pl.ds(start, ck), :]))
        return (ls0 + _lane_tile_sum(p0), ls1 + _lane_tile_sum(p1),
                a0 + _dot(p0.astype(BF16), v), a1 + _dot(p1.astype(BF16), v))

    def finish(l0, l1, a0, a1):
        lp = lp_ref[...]
        lam = (jnp.exp(jnp.sum(lp[0:1] * lp[1:2], axis=-1, keepdims=True))
               - jnp.exp(jnp.sum(lp[2:3] * lp[3:4], axis=-1, keepdims=True)) + lam_init)
        o = a0 / l0 - lam * (a1 / l1)
        var = jnp.mean(o * o, axis=-1, keepdims=True)
        o = o * lax.rsqrt(var + RMS_EPS) * og_ref[...] * (1.0 - lam_init)
        o_ref[...] = o.astype(o_ref.dtype)

    zeros = jnp.zeros((tq, LANES), F32)
    starts = [c * ck for c in range((t + n_ctx) // ck)]

    def attend_fast():
        carry = (zeros, zeros, zeros, zeros)
        for start in starts:
            carry = fast_step(start, carry)
        ls0, ls1, a0, a1 = carry
        finish(jnp.sum(ls0, axis=-1, keepdims=True), jnp.sum(ls1, axis=-1, keepdims=True), a0, a1)

    def attend_online():
        col = lambda val: jnp.full((tq, 1), val, F32)
        carry = (col(NEG), col(NEG), col(0.0), col(0.0), zeros, zeros)
        for start in starts:
            carry = step_fn(start, carry)
        finish(*carry[2:])

    fast_ok = bound_scr[0] <= FAST_SOFTMAX_BOUND
    pl.when(fast_ok)(attend_fast)
    pl.when(jnp.logical_not(fast_ok))(attend_online)


def _diff_call(z, *, b, t, tq, qn, kn, lparams, out_g, rope, cache, layer, kv_out, name):
    nq = t // tq
    n_ctx = PAST_LEN if cache is not None else 0
    ck = min(ATTN_CHUNK, t + n_ctx)
    qblk, kblk, vblk = _COL['qd'] // LANES, _COL['kd'] // LANES, _COL['vd'] // LANES
    const = lambda bb, h, i: (0, 0)
    in_specs = [pl.BlockSpec((tq, LANES), lambda bb, h, i: (bb * nq + i, qblk + h)),
                pl.BlockSpec((t, LANES), lambda bb, h, i: (bb, kblk + h)),
                pl.BlockSpec((t, LANES), lambda bb, h, i: (bb, vblk + h)),
                pl.BlockSpec((1, LANES), const),
                pl.BlockSpec((1, LANES), const),
                pl.BlockSpec((4, HEAD_DIM), const),
                pl.BlockSpec((1, LANES), const)]
    args = [z, z, z, jnp.tile(qn, 2).reshape(1, LANES), jnp.tile(kn, 2).reshape(1, LANES),
            lparams, out_g.reshape(1, LANES)]
    if rope is not None:
        cos, sin = rope
        in_specs += [pl.BlockSpec((tq, LANES), lambda bb, h, i: (i, 0)),
                     pl.BlockSpec((tq, LANES), lambda bb, h, i: (i, 0)),
                     pl.BlockSpec((t, LANES), const),
                     pl.BlockSpec((t, LANES), const)]
        args += [cos, sin, cos, sin]
    if cache is not None:
        spec = pl.BlockSpec((1, 1, n_ctx, LANES), lambda bb, h, i: (bb, layer, 0, h))
        in_specs += [spec, spec]
        args += [cache[0], cache[1]]
    out_shape = [jax.ShapeDtypeStruct((b * t, 512), BF16)]
    out_specs = [pl.BlockSpec((tq, LANES), lambda bb, h, i: (bb * nq + i, h))]
    aliases = {}
    if kv_out is not None:
        for n_out, arr in enumerate(kv_out):
            aliases[len(args)] = 1 + n_out
            in_specs.append(pl.BlockSpec(memory_space=pl.ANY))
            args.append(arr)
            out_shape.append(jax.ShapeDtypeStruct(arr.shape, arr.dtype))
            out_specs.append(pl.BlockSpec((1, 1, t, LANES), lambda bb, h, i: (bb, layer, 0, h)))
    lam_init = 0.8 - 0.6 * math.exp(-0.3 * layer)
    res = pl.pallas_call(
        functools.partial(_diff_kernel, t=t, tq=tq, n_ctx=n_ctx, has_rope=rope is not None,
                          emit_k=kv_out is not None, ck=ck, lam_init=lam_init),
        grid=(b, DIFF_HEADS, nq),
        in_specs=in_specs,
        out_specs=out_specs,
        out_shape=out_shape,
        input_output_aliases=aliases,
        scratch_shapes=[pltpu.VMEM((2, t + n_ctx, LANES), BF16), pltpu.VMEM((t + n_ctx, LANES), BF16),
                        pltpu.SMEM((1,), F32)],
        compiler_params=_cparams(),
        name=name,
    )(*args)
    return res[0], tuple(res[1:])


def _softplus(x):
    return jnp.maximum(x, 0.0) + jnp.log1p(jnp.exp(-jnp.abs(x)))


def _lru_kernel(*refs, tc, nc, has_h0, emit_final):
    it = iter(refs)
    xp_ref, xm_ref, xn_ref, ya_ref, cw_ref, cb_ref, wl_ref, bl_ref, lam_ref = (next(it) for _ in range(9))
    if has_h0:
        h0_ref = next(it)
    if emit_final:
        next(it)
    o_ref = next(it)
    if emit_final:
        fin_ref = next(it)
    hf_scr, a_scr, b_scr, carry_scr = (next(it) for _ in range(4))

    s = pl.program_id(1)
    fwd = s < nc
    c = jnp.where(fwd, s, 2 * nc - 1 - s)
    t0 = pl.multiple_of(c * tc, tc)

    @pl.when(s == 0)
    def _():
        if has_h0:
            carry_scr[...] = h0_ref[0, 0]
        else:
            carry_scr[...] = jnp.zeros_like(carry_scr)

    prev = jnp.where(c > 0, xp_ref[...], 0.0)
    nxt = jnp.where(c < nc - 1, xn_ref[...], 0.0)
    ext = jnp.concatenate([prev, xm_ref[...], nxt], axis=0)
    n_ext = tc + 2 * SUBLANES
    u = cb_ref[...] + cw_ref[1:2] * xm_ref[...]
    u = u + cw_ref[0:1] * pltpu.roll(ext, 1, 0)[SUBLANES:SUBLANES + tc]
    u = u + cw_ref[2:3] * pltpu.roll(ext, n_ext - 1, 0)[SUBLANES:SUBLANES + tc]
    u = u + cw_ref[3:4] * pltpu.roll(ext, n_ext - 2, 0)[SUBLANES:SUBLANES + tc]

    gates = _dot(u.astype(BF16), wl_ref[0]) + bl_ref[0]
    r = jax.nn.sigmoid(gates[:, :LRU_W])
    ig = jax.nn.sigmoid(gates[:, LRU_W:])
    log_a = -LRU_C * r * _softplus(-lam_ref[0])
    a = jnp.exp(log_a)
    a_scr[...] = a
    b_scr[...] = jnp.sqrt(-jnp.tanh(log_a) * (a * a + 1.0)) * ig * u

    ntile = tc // SUBLANES
    row = lax.broadcasted_iota(jnp.int32, (SUBLANES, LRU_W), 0)

    def scan(forward):
        def tile(i, carry):
            j = i if forward else ntile - 1 - i
            r0 = pl.multiple_of(j * SUBLANES, SUBLANES)
            a = a_scr[pl.ds(r0, SUBLANES), :]
            bv = b_scr[pl.ds(r0, SUBLANES), :]
            for d in (1, 2, 4):
                shift = d if forward else SUBLANES - d
                msk = (row >= d) if forward else (row < SUBLANES - d)
                ap = pltpu.roll(a, shift, 0)
                bp = pltpu.roll(bv, shift, 0)
                bv = jnp.where(msk, a * bp + bv, bv)
                a = jnp.where(msk, a * ap, a)
            h = a * carry + bv
            g0 = pl.multiple_of(t0 + r0, SUBLANES)
            if forward:
                hf_scr[pl.ds(g0, SUBLANES), :] = h
                return h[SUBLANES - 1:SUBLANES, :]
            b_scr[pl.ds(r0, SUBLANES), :] = h
            return h[0:1, :]

        idx = 0 if forward else 1
        last = lax.fori_loop(0, ntile, tile, carry_scr[idx:idx + 1, :])
        carry_scr[idx:idx + 1, :] = last
        if emit_final:
            fin_ref[0, 0, idx:idx + 1, :] = last
        if not forward:
            o_ref[...] = ((hf_scr[pl.ds(t0, tc), :] + b_scr[...])
                          * jax.nn.gelu(ya_ref[...])).astype(o_ref.dtype)

    pl.when(fwd)(lambda: scan(True))
    pl.when(jnp.logical_not(fwd))(lambda: scan(False))


def _lru_call(z, *, b, t, conv_w, conv_b, w_lru, b_lru, lam, h0, state_out, layer, name):
    tc = min(t, 512)
    nc = t // tc
    per8 = tc // SUBLANES
    nrow8 = b * t // SUBLANES

    def chunk(s):
        return jnp.where(s < nc, s, 2 * nc - 1 - s)

    def hold(s):
        return jnp.where(s < nc, nc - 1, 2 * nc - 1 - s)

    in_specs = [
        pl.BlockSpec((SUBLANES, LRU_W),
                     lambda bb, s: (jnp.maximum((bb * nc + chunk(s)) * per8 - 1, 0), 0)),
        pl.BlockSpec((tc, LRU_W), lambda bb, s: (bb * nc + chunk(s), 0)),
        pl.BlockSpec((SUBLANES, LRU_W),
                     lambda bb, s: (jnp.minimum((bb * nc + chunk(s) + 1) * per8, nrow8 - 1), 0)),
        pl.BlockSpec((tc, LRU_W), lambda bb, s: (bb * nc + hold(s), 1)),
        pl.BlockSpec((CONV_W, LRU_W), lambda bb, s: (0, 0)),
        pl.BlockSpec((1, LRU_W), lambda bb, s: (0, 0)),
        pl.BlockSpec((1, LRU_W, 2 * LRU_W), lambda bb, s: (s // nc, 0, 0)),
        pl.BlockSpec((1, 1, 2 * LRU_W), lambda bb, s: (s // nc, 0, 0)),
        pl.BlockSpec((1, 1, LRU_W), lambda bb, s: (s // nc, 0, 0)),
    ]
    args = [z, z, z, z, conv_w, conv_b.reshape(1, LRU_W), w_lru, b_lru, lam.reshape(2, 1, LRU_W)]
    if h0 is not None:
        in_specs.append(pl.BlockSpec((1, 1, 2, LRU_W), lambda bb, s: (bb, layer, 0, 0)))
        args.append(h0)
    out_shape = [jax.ShapeDtypeStruct((b * t, LRU_W), BF16)]
    out_specs = [pl.BlockSpec((tc, LRU_W), lambda bb, s: (bb * nc + hold(s), 0))]
    aliases = {}
    if state_out is not None:
        aliases[len(args)] = 1
        in_specs.append(pl.BlockSpec(memory_space=pl.ANY))
        args.append(state_out)
        out_shape.append(jax.ShapeDtypeStruct(state_out.shape, state_out.dtype))
        out_specs.append(pl.BlockSpec((1, 1, 2, LRU_W), lambda bb, s: (bb, layer, 0, 0)))
    res = pl.pallas_call(
        functools.partial(_lru_kernel, tc=tc, nc=nc, has_h0=h0 is not None,
                          emit_final=state_out is not None),
        grid=(b, 2 * nc),
        in_specs=in_specs,
        out_specs=out_specs,
        out_shape=out_shape,
        input_output_aliases=aliases,
        scratch_shapes=[pltpu.VMEM((t, LRU_W), F32), pltpu.VMEM((tc, LRU_W), F32),
                        pltpu.VMEM((tc, LRU_W), F32), pltpu.VMEM((2, LRU_W), F32)],
        compiler_params=_cparams(),
        name=name,
    )(*args)
    return res[0], (res[1] if state_out is not None else None)


def _merge_kernel(x_ref, gt_ref, a_ref, w_ref, g_ref, d_ref, wb_ref, wo_ref, mod_ref, o_ref):
    merged = None
    for bi, br_ref in enumerate((a_ref, w_ref, g_ref, d_ref)):
        gate = gt_ref[:, bi * D_MODEL:(bi + 1) * D_MODEL].astype(F32)
        term = gate * _dot(br_ref[...], wb_ref[0, bi])
        merged = term if merged is None else merged + term
    o_ref[...] = x_ref[...] + mod_ref[0][2:3] * _dot(merged.astype(BF16), wo_ref[0])


def _merge_call(x, gates, branches, w_branch, w_o, layer, mod, latent, name):
    m = x.shape[0]
    tm = 256
    row = lambda i: (i, 0)
    resident = pl.Buffered(1)
    in_specs = [pl.BlockSpec((tm, D_MODEL), row),
                pl.BlockSpec((tm, N_BRANCH * D_MODEL), row)]
    in_specs += [pl.BlockSpec((tm, BRANCH_W), row)] * N_BRANCH
    in_specs += [pl.BlockSpec((1, N_BRANCH, BRANCH_W, D_MODEL), lambda i: (layer, 0, 0, 0),
                              pipeline_mode=resident),
                 pl.BlockSpec((1, D_MODEL, D_MODEL), lambda i: (layer, 0, 0), pipeline_mode=resident),
                 pl.BlockSpec((1, N_MOD, D_MODEL), _mod_index(latent, tm))]
    return pl.pallas_call(
        _merge_kernel,
        grid=(m // tm,),
        in_specs=in_specs,
        out_specs=pl.BlockSpec((tm, D_MODEL), row),
        out_shape=jax.ShapeDtypeStruct((m, D_MODEL), F32),
        compiler_params=_cparams(),
        name=name,
    )(x, gates, *branches, w_branch, w_o, mod)


def _ffn_kernel(x_ref, mod_ref, g_ref, w1_ref, b1_ref, w2_ref, b2_ref, o_ref, h_scr, acc_scr):
    j = pl.program_id(1)

    @pl.when(j == 0)
    def _():
        h_scr[...] = _norm_mod(x_ref[...], g_ref[...], mod_ref[0], 3, 4).astype(BF16)
        acc_scr[...] = jnp.zeros_like(acc_scr)

    a = jnp.maximum(_dot(h_scr[...], w1_ref[0]) + b1_ref[...], 0.0)
    acc_scr[...] += _dot((a * a).astype(BF16), w2_ref[0])

    @pl.when(j == pl.num_programs(1) - 1)
    def _():
        o_ref[...] = x_ref[...] + mod_ref[0][5:6] * (acc_scr[...] + b2_ref[...])


def _ffn_call(x, mod, g, w1, b1, w2, b2, layer, latent, name):
    m = x.shape[0]
    tm, tf = 512, 1024
    return pl.pallas_call(
        _ffn_kernel,
        grid=(m // tm, D_FF // tf),
        in_specs=[pl.BlockSpec((tm, D_MODEL), lambda i, j: (i, 0)),
                  pl.BlockSpec((1, N_MOD, D_MODEL), _mod_index(latent, tm)),
                  pl.BlockSpec((1, D_MODEL), lambda i, j: (0, 0)),
                  pl.BlockSpec((1, D_MODEL, tf), lambda i, j: (layer, 0, j)),
                  pl.BlockSpec((1, tf), lambda i, j: (0, j)),
                  pl.BlockSpec((1, tf, D_MODEL), lambda i, j: (layer, j, 0)),
                  pl.BlockSpec((1, D_MODEL), lambda i, j: (0, 0))],
        out_specs=pl.BlockSpec((tm, D_MODEL), lambda i, j: (i, 0)),
        out_shape=jax.ShapeDtypeStruct((m, D_MODEL), F32),
        scratch_shapes=[pltpu.VMEM((tm, D_MODEL), BF16), pltpu.VMEM((tm, D_MODEL), F32)],
        compiler_params=_cparams(),
        name=name,
    )(x, mod, g.reshape(1, D_MODEL), w1, b1.reshape(1, D_FF), w2, b2.reshape(1, D_MODEL))


def _block_diag(w):
    eye = jnp.eye(LRU_BLOCKS, dtype=w.dtype)
    return jnp.einsum('ncd,nm->ncmd', w, eye).reshape(LRU_W, LRU_W)


def _trunk_layer(x, mod, wts, layer, latent, rope, cached, new):
    b, t = (DEC_BATCH, DEC_SEQ) if latent else (BATCH, SEQ)
    tag = ('lat' if latent else 'ctx') + str(layer)
    gates, h = _gates_call(x, mod, wts['norm1_g'], wts['w_gate'], layer, wts['b_gate'], latent,
                           'gates_' + tag)
    z = _inproj_call(h, wts['w_in'], layer, 'inproj_' + tag)
    a_out, new_state = _lru_call(
        z, b=b, t=t, conv_w=wts['conv_w'], conv_b=wts['conv_b'], w_lru=wts['w_lru'], b_lru=wts['b_lru'],
        lam=wts['lru_lambda'], h0=cached[6] if latent else None, state_out=None if latent else new[6],
        layer=layer, name='lru_' + tag)
    w_out, new_win = _gqa_call(
        z, b=b, t=t, tq=256, qname='qw', kname='kw', vname='vw',
        qn=wts['win_qn'], kn=wts['win_kn'], sink=wts['win_sink'], rope=rope,
        cache=(cached[0], cached[1]) if latent else None, layer=layer, banded=latent,
        kv_out=None if latent else new[0:2], name='win_' + tag)
    g_out, new_grid = _gqa_call(
        z, b=b, t=t, tq=256, qname='qg', kname='kg', vname='vg',
        qn=wts['grid_qn'], kn=wts['grid_kn'], sink=None, rope=rope,
        cache=(cached[2], cached[3]) if latent else None, layer=layer, banded=False,
        kv_out=None if latent else new[2:4], name='grid_' + tag)
    d_out, new_diff = _diff_call(
        z, b=b, t=t, tq=256, qn=wts['diff_qn'], kn=wts['diff_kn'], lparams=wts['diff_lp'],
        out_g=wts['diff_out_g'], rope=rope, cache=(cached[4], cached[5]) if latent else None,
        layer=layer, kv_out=None if latent else new[4:6], name='diff_' + tag)
    x = _merge_call(x, gates, (a_out, w_out, g_out, d_out), wts['w_branch'], wts['w_o'], layer, mod,
                    latent, 'merge_' + tag)
    x = _ffn_call(x, mod, wts['norm2_g'], wts['w_ff1'], wts['b_ff1'], wts['w_ff2'], wts['b_ff2'],
                  layer, latent, 'ffn_' + tag)
    if latent:
        return x, None
    return x, new_win + new_grid + new_diff + (new_state,)


def kernel(x_prompt, x_sample, cache_win_k, cache_win_v, cache_grid_k, cache_grid_v, cache_diff_k, cache_diff_v, state_lru, c, c_ctx, w_ada, b_ada, norm1_g, norm2_g, w_in, conv_w, conv_b, lru_wr, lru_br, lru_wi, lru_bi, lru_lambda, win_qn, win_kn, win_sink, grid_qn, grid_kn, diff_qn, diff_kn, diff_lq1, diff_lk1, diff_lq2, diff_lk2, diff_out_g, w_branch, w_gate, b_gate, w_o, w_ff1, b_ff1, w_ff2, b_ff2):
    cond = jnp.zeros((MOD_ROWS, D_MODEL), F32).at[0].set(c_ctx).at[1:1 + DEC_BATCH].set(c)
    mod_all = _modulation(cond, w_ada, b_ada).reshape(DEPTH, MOD_ROWS, N_MOD, D_MODEL)
    rope = _rope_tables(DEC_SEQ)
    cached_all = (cache_win_k.reshape(DEC_BATCH, DEPTH, PAST_LEN, LANES),
                  cache_win_v.reshape(DEC_BATCH, DEPTH, PAST_LEN, LANES),
                  cache_grid_k.reshape(DEC_BATCH, DEPTH, PAST_LEN, LANES),
                  cache_grid_v.reshape(DEC_BATCH, DEPTH, PAST_LEN, LANES),
                  cache_diff_k.reshape(DEC_BATCH, DEPTH, PAST_LEN, 512),
                  cache_diff_v.reshape(DEC_BATCH, DEPTH, PAST_LEN, 512))
    w_in_b, w_gate_b, w_branch_b, w_o_b, w_ff1_b, w_ff2_b = (
        w.astype(BF16) for w in (w_in, w_gate, w_branch, w_o, w_ff1, w_ff2))
    y_p = x_prompt.reshape(BATCH * SEQ, D_MODEL)
    y_s = x_sample.reshape(DEC_BATCH * DEC_SEQ, D_MODEL)
    new = tuple(jnp.zeros((BATCH, DEPTH, SEQ, w), F32) for w in (LANES,) * 4 + (512,) * 2)
    new += (jnp.zeros((BATCH, DEPTH, 2, LRU_W), F32),)
    for l in range(DEPTH):
        w_lru = jnp.stack([
            jnp.concatenate([_block_diag(lru_wr[l, k]), _block_diag(lru_wi[l, k])], axis=1)
            for k in range(2)]).astype(BF16)
        b_lru = jnp.concatenate([lru_br[l], lru_bi[l]], axis=-1).reshape(2, 1, 2 * LRU_W)
        wts = {
            'norm1_g': norm1_g[l], 'norm2_g': norm2_g[l],
            'w_in': w_in_b, 'w_gate': w_gate_b, 'b_gate': b_gate[l],
            'conv_w': conv_w[l], 'conv_b': conv_b[l], 'w_lru': w_lru, 'b_lru': b_lru,
            'lru_lambda': lru_lambda[l],
            'win_qn': win_qn[l], 'win_kn': win_kn[l], 'win_sink': win_sink[l],
            'grid_qn': grid_qn[l], 'grid_kn': grid_kn[l],
            'diff_qn': diff_qn[l], 'diff_kn': diff_kn[l],
            'diff_lp': jnp.stack([diff_lq1[l], diff_lk1[l], diff_lq2[l], diff_lk2[l]]),
            'diff_out_g': diff_out_g[l],
            'w_branch': w_branch_b, 'w_o': w_o_b,
            'w_ff1': w_ff1_b, 'b_ff1': b_ff1[l], 'w_ff2': w_ff2_b, 'b_ff2': b_ff2[l],
        }
        y_p, new = _trunk_layer(y_p, mod_all[l], wts, l, False, None, None, new)
        y_s, _ = _trunk_layer(y_s, mod_all[l], wts, l, True, rope, cached_all + (state_lru,), None)

    kv_shape = (BATCH, DEPTH, SEQ, WIN_KV, HEAD_DIM)
    return (y_p.reshape(BATCH, SEQ, D_MODEL),
            y_s.reshape(DEC_BATCH, DEC_SEQ, D_MODEL),
            new[0].reshape(kv_shape), new[1].reshape(kv_shape),
            new[2].reshape(kv_shape), new[3].reshape(kv_shape),
            new[4].reshape(BATCH, DEPTH, SEQ, DIFF_HEADS, 2, HEAD_DIM),
            new[5].reshape(BATCH, DEPTH, SEQ, DIFF_HEADS, 2 * HEAD_DIM),
            new[6])
```

```python
import functools
import math

import jax
import jax.numpy as jnp
import numpy as np
from jax import lax
from jax.experimental import pallas as pl
from jax.experimental.pallas import tpu as pltpu

F32 = jnp.float32
BF16 = jnp.bfloat16

D_MODEL = 2048
BATCH = 32
SEQ = 256
DEPTH = 4
DEC_BATCH = 4
DEC_SEQ = 4096
PAST_LEN = 512
GRID_W = 64
BLOCK = 128
HEAD_DIM = 64
N_FREQ = HEAD_DIM // 4
ROPE_BASE = 10000.0
ATTN_SCALE = HEAD_DIM ** -0.5
RMS_EPS = 1e-6
N_MOD = 6
N_BRANCH = 4
BRANCH_W = D_MODEL // 4
LRU_W = BRANCH_W
LRU_BLOCKS = 8
LRU_BW = LRU_W // LRU_BLOCKS
LRU_C = 8.0
CONV_W = 4
WIN_HEADS = 8
WIN_KV = 2
DIFF_HEADS = 4
D_FF = 4 * D_MODEL
D_IN = 4096

V7X_VMEM_BYTES = 64 * 1024 * 1024
VMEM_LIMIT = V7X_VMEM_BYTES - 8 * 1024 * 1024
LANES = 128
SUBLANES = 8
NEG = -1e30
LOG2E = math.log2(math.e)
ATTN_CHUNK = 1536
FAST_SOFTMAX_BOUND = 40.0
NORM_PIECE = 256
MOD_ROWS = 8

_COL = dict(xa=0, ya=512, qw=1024, kw=1536, vw=1664, qg=1792, kg=2304, vg=2432, qd=2560, kd=3072,
            vd=3584)


def _cparams():
    return pltpu.CompilerParams(vmem_limit_bytes=VMEM_LIMIT)


def _dot(a, b):
    return jnp.dot(a, b, preferred_element_type=F32)


def _dot_nt(a, b):
    return lax.dot_general(a, b, (((1,), (1,)), ((), ())), preferred_element_type=F32)


def _mod_kernel(c_ref, w_ref, b_ref, o_ref):
    c = c_ref[...]
    s = (c * jax.nn.sigmoid(c)).astype(BF16)
    o_ref[0] = _dot(s, w_ref[0].astype(BF16)) + b_ref[0]


def _modulation(cond, w_ada, b_ada):
    tn = 1024
    n = N_MOD * D_MODEL
    return pl.pallas_call(
        _mod_kernel,
        grid=(DEPTH, n // tn),
        in_specs=[pl.BlockSpec((MOD_ROWS, D_MODEL), lambda l, j: (0, 0)),
                  pl.BlockSpec((1, D_MODEL, tn), lambda l, j: (l, 0, j)),
                  pl.BlockSpec((1, 1, tn), lambda l, j: (l, 0, j))],
        out_specs=pl.BlockSpec((1, MOD_ROWS, tn), lambda l, j: (l, 0, j)),
        out_shape=jax.ShapeDtypeStruct((DEPTH, MOD_ROWS, n), F32),
        compiler_params=_cparams(),
        name='modulation',
    )(cond, w_ada, b_ada.reshape(DEPTH, 1, n))


def _mod_index(latent, tm):
    if latent:
        per = DEC_SEQ // tm
        return lambda i, *_: (1 + i // per, 0, 0)
    return lambda i, *_: (0, 0, 0)


def _norm_mod(x, g, mod, shift_idx, scale_idx):
    var = jnp.mean(x * x, axis=-1, keepdims=True)
    y = x * lax.rsqrt(var + RMS_EPS) * g
    return y * (1.0 + mod[scale_idx:scale_idx + 1]) + mod[shift_idx:shift_idx + 1]


def _gates_kernel(x_ref, mod_ref, g_ref, w_ref, b_ref, o_ref, h_ref):
    def gate(h):
        return jax.nn.sigmoid(_dot(h, w_ref[0]) + b_ref[...]).astype(o_ref.dtype)

    @pl.when(pl.program_id(1) == 0)
    def _():
        for r in range(0, x_ref.shape[0], NORM_PIECE):
            rows = slice(r, r + NORM_PIECE)
            h = _norm_mod(x_ref[rows, :], g_ref[...], mod_ref[0], 0, 1).astype(BF16)
            h_ref[rows, :] = h
            o_ref[rows, :] = gate(h)

    @pl.when(pl.program_id(1) > 0)
    def _():
        o_ref[...] = gate(h_ref[...])


def _gates_call(x, mod, g, w, layer, bias, latent, name):
    m, n = x.shape[0], w.shape[2]
    tm, tn = 1024, 1024
    return pl.pallas_call(
        _gates_kernel,
        grid=(m // tm, n // tn),
        in_specs=[pl.BlockSpec((tm, D_MODEL), lambda i, j: (i, 0)),
                  pl.BlockSpec((1, N_MOD, D_MODEL), _mod_index(latent, tm)),
                  pl.BlockSpec((1, D_MODEL), lambda i, j: (0, 0)),
                  pl.BlockSpec((1, D_MODEL, tn), lambda i, j: (layer, 0, j)),
                  pl.BlockSpec((1, tn), lambda i, j: (0, j))],
        out_specs=[pl.BlockSpec((tm, tn), lambda i, j: (i, j)),
                   pl.BlockSpec((tm, D_MODEL), lambda i, j: (i, 0))],
        out_shape=[jax.ShapeDtypeStruct((m, n), BF16), jax.ShapeDtypeStruct((m, D_MODEL), BF16)],
        compiler_params=_cparams(),
        name=name,
    )(x, mod, g.reshape(1, D_MODEL), w, bias.reshape(1, n))


def _inproj_kernel(h_ref, w_ref, o_ref):
    o_ref[...] = _dot(h_ref[...], w_ref[0])


def _inproj_call(h, w, layer, name):
    m, n = h.shape[0], w.shape[2]
    tm, tn = 1024, 2048
    return pl.pallas_call(
        _inproj_kernel,
        grid=(m // tm, n // tn),
        in_specs=[pl.BlockSpec((tm, D_MODEL), lambda i, j: (i, 0)),
                  pl.BlockSpec((1, D_MODEL, tn), lambda i, j: (layer, 0, j))],
        out_specs=pl.BlockSpec((tm, tn), lambda i, j: (i, j)),
        out_shape=jax.ShapeDtypeStruct((m, n), F32),
        compiler_params=_cparams(),
        name=name,
    )(h, w)


def _lane_lo():
    return lax.broadcasted_iota(jnp.int32, (1, LANES), 1) < HEAD_DIM


def _seg_matrix():
    r = lax.broadcasted_iota(jnp.int32, (LANES, LANES), 0) // HEAD_DIM
    c = lax.broadcasted_iota(jnp.int32, (LANES, LANES), 1) // HEAD_DIM
    return jnp.where(r == c, 1.0, 0.0).astype(BF16)


def _head_rmsnorm(x, gain):
    x2 = x * x
    hi = x2.astype(BF16)
    lo = (x2 - hi.astype(F32)).astype(BF16)
    seg = _seg_matrix()
    ms = (_dot(hi, seg) + _dot(lo, seg)) * (1.0 / HEAD_DIM)
    return x * lax.rsqrt(ms + RMS_EPS) * gain


def _rope(x, cos, sin_signed):
    lane = lax.broadcasted_iota(jnp.int32, (1, LANES), 1)
    first = (lane & (2 * N_FREQ - 1)) < N_FREQ
    up = pltpu.roll(x, LANES - N_FREQ, 1)
    dn = pltpu.roll(x, N_FREQ, 1)
    return x * cos + jnp.where(first, up, dn) * sin_signed


def _score_bound(qn_ref, kn_ref, cached_k):
    root = math.sqrt(HEAD_DIM)
    qmax = root * jnp.max(jnp.abs(qn_ref[...]))
    kmax = root * jnp.max(jnp.abs(kn_ref[...]))
    if cached_k is not None:
        kmax = jnp.maximum(kmax, jnp.sqrt(jnp.max(jnp.sum(cached_k * cached_k, axis=-1, keepdims=True))))
    return qmax * kmax * (ATTN_SCALE * LOG2E)


def _lane_tile_sum(p):
    acc = p[:, 0:LANES]
    for j in range(1, p.shape[1] // LANES):
        acc = acc + p[:, j * LANES:(j + 1) * LANES]
    return acc


def _rope_tables(t):
    pos = jnp.arange(t)
    row = (pos // GRID_W).astype(F32)
    col = (pos % GRID_W).astype(F32)
    inv = ROPE_BASE ** (-jnp.arange(N_FREQ, dtype=F32) / N_FREQ)
    ar, ac = row[:, None] * inv, col[:, None] * inv
    cos = jnp.concatenate([jnp.cos(ar), jnp.cos(ar), jnp.cos(ac), jnp.cos(ac)], axis=-1)
    sin = jnp.concatenate([-jnp.sin(ar), jnp.sin(ar), -jnp.sin(ac), jnp.sin(ac)], axis=-1)
    return jnp.tile(cos, (1, 2)), jnp.tile(sin, (1, 2))


def _gqa_kernel(*refs, t, tq, n_ctx, banded, has_sink, has_rope, emit_k, ck):
    it = iter(refs)
    q_ref, k_ref, v_ref, qn_ref, kn_ref = (next(it) for _ in range(5))
    if has_rope:
        cq_ref, sq_ref, cka_ref, ska_ref = (next(it) for _ in range(4))
    if n_ctx:
        kc_ref, vc_ref = next(it), next(it)
    if has_sink:
        sink_ref = next(it)
    if emit_k:
        next(it), next(it)
    o_ref = next(it)
    if emit_k:
        ko_ref, vo_ref = next(it), next(it)
    k_scr, v_scr, bound_scr = next(it), next(it), next(it)

    g = pl.program_id(1)
    qi = pl.program_id(2)
    lo = _lane_lo()
    lat0 = BLOCK if banded else 0
    ctx0 = t + 2 * lat0
    first_head = g == 0

    def put(dst, kx, vx):
        kr = pltpu.roll(kx, HEAD_DIM, 1)
        vr = pltpu.roll(vx, HEAD_DIM, 1)
        zero = jnp.zeros_like(kx)
        k_scr[0, dst, :] = jnp.where(lo, jnp.where(first_head, kx, kr), zero).astype(BF16)
        k_scr[1, dst, :] = jnp.where(lo, zero, jnp.where(first_head, kr, kx)).astype(BF16)
        v_scr[0, dst, :] = jnp.where(lo, jnp.where(first_head, vx, vr), zero).astype(BF16)
        v_scr[1, dst, :] = jnp.where(lo, zero, jnp.where(first_head, vr, vx)).astype(BF16)

    @pl.when(qi == 0)
    def _build():
        step = min(t, 512)
        for r in range(0, t, step):
            kx = _head_rmsnorm(k_ref[r:r + step, :], kn_ref[...])
            if emit_k:
                ko_ref[0, 0, r:r + step, :] = kx
                vo_ref[0, 0, r:r + step, :] = v_ref[r:r + step, :]
            if has_rope:
                kx = _rope(kx, cka_ref[r:r + step, :], ska_ref[r:r + step, :])
            put(slice(lat0 + r, lat0 + r + step), kx, v_ref[r:r + step, :])
        if banded:
            zpad = jnp.zeros((BLOCK, LANES), BF16)
            for scr in (k_scr, v_scr):
                for var in range(2):
                    scr[var, 0:BLOCK, :] = zpad
                    scr[var, lat0 + t:lat0 + t + BLOCK, :] = zpad
        if n_ctx:
            put(slice(ctx0, ctx0 + n_ctx), kc_ref[0, 0], vc_ref[0, 0])
        bound = _score_bound(qn_ref, kn_ref, kc_ref[0, 0] if n_ctx else None)
        if has_sink:
            for h in range(WIN_HEADS):
                bound = jnp.maximum(bound, jnp.abs(sink_ref[h]) * LOG2E)
        bound_scr[0] = bound

    def step_fn(q2, segs, carry):
        m0, m1, l, acc = carry

        def scores(var):
            parts = []
            for start, size, mask in segs:
                s = _dot_nt(q2, k_scr[var, pl.ds(start, size), :])
                parts.append(s if mask is None else jnp.where(mask, s, NEG))
            return parts[0] if len(parts) == 1 else jnp.concatenate(parts, axis=1)

        s0, s1 = scores(0), scores(1)
        n0 = jnp.maximum(m0, jnp.max(s0, axis=-1, keepdims=True))
        n1 = jnp.maximum(m1, jnp.max(s1, axis=-1, keepdims=True))
        p0 = jnp.exp2(s0 - n0)
        p1 = jnp.exp2(s1 - n1)
        alpha = jnp.where(lo, jnp.exp2(m0 - n0), jnp.exp2(m1 - n1))
        rs = jnp.where(lo, jnp.sum(p0, axis=-1, keepdims=True), jnp.sum(p1, axis=-1, keepdims=True))
        l = alpha * l + rs
        acc = alpha * acc
        p0, p1 = p0.astype(BF16), p1.astype(BF16)
        off = 0
        for start, size, _ in segs:
            acc = (acc + _dot(p0[:, off:off + size], v_scr[0, pl.ds(start, size), :])
                   + _dot(p1[:, off:off + size], v_scr[1, pl.ds(start, size), :]))
            off += size
        return n0, n1, l, acc

    def fast_step(q2, segs, carry):
        ls0, ls1, acc = carry
        for start, size, mask in segs:
            s0 = _dot_nt(q2, k_scr[0, pl.ds(start, size), :])
            s1 = _dot_nt(q2, k_scr[1, pl.ds(start, size), :])
            if mask is not None:
                s0 = jnp.where(mask, s0, NEG)
                s1 = jnp.where(mask, s1, NEG)
            p0 = jnp.exp2(s0)
            p1 = jnp.exp2(s1)
            ls0 = ls0 + _lane_tile_sum(p0)
            ls1 = ls1 + _lane_tile_sum(p1)
            acc = (acc + _dot(p0.astype(BF16), v_scr[0, pl.ds(start, size), :])
                   + _dot(p1.astype(BF16), v_scr[1, pl.ds(start, size), :]))
        return ls0, ls1, acc

    if banded:
        span = tq + 2 * BLOCK
        rr = lax.broadcasted_iota(jnp.int32, (tq, span), 0)
        cc = lax.broadcasted_iota(jnp.int32, (tq, span), 1)
        kpos = qi * tq - BLOCK + cc
        band_mask = (cc >= rr) & (cc - rr <= 2 * BLOCK) & (kpos >= 0) & (kpos < t)

    def attend(fast):
        q = q_ref[...]
        for p in range(2):
            qp = _head_rmsnorm(q[:, p * LANES:(p + 1) * LANES], qn_ref[...])
            if has_rope:
                qp = _rope(qp, cq_ref[...], sq_ref[...])
            q2 = (qp * (ATTN_SCALE * LOG2E)).astype(BF16)
            if banded:
                chunks = [[(pl.multiple_of(qi * tq, tq), span, band_mask)]]
                if n_ctx:
                    chunks[0].append((ctx0, n_ctx, None))
            else:
                chunks = [[(c * ck, ck, None)] for c in range((t + n_ctx) // ck)]
            if has_sink:
                h0 = g * 4 + 2 * p
                sink0 = jnp.full((1, 1), sink_ref[h0] * LOG2E, F32)
                sink1 = jnp.full((1, 1), sink_ref[h0 + 1] * LOG2E, F32)
            zeros = jnp.zeros((tq, LANES), F32)
            if fast:
                carry = (zeros, zeros, zeros)
                for segs in chunks:
                    carry = fast_step(q2, segs, carry)
                ls0, ls1, acc = carry
                l = jnp.where(lo, jnp.sum(ls0, axis=-1, keepdims=True),
                              jnp.sum(ls1, axis=-1, keepdims=True))
                if has_sink:
                    l = l + jnp.where(lo, jnp.exp2(sink0), jnp.exp2(sink1))
            else:
                if has_sink:
                    m0 = jnp.broadcast_to(sink0, (tq, 1))
                    m1 = jnp.broadcast_to(sink1, (tq, 1))
                    l = jnp.ones((tq, LANES), F32)
                else:
                    m0 = jnp.full((tq, 1), NEG, F32)
                    m1 = jnp.full((tq, 1), NEG, F32)
                    l = zeros
                carry = (m0, m1, l, zeros)
                for segs in chunks:
                    carry = step_fn(q2, segs, carry)
                _, _, l, acc = carry
            o_ref[:, p * LANES:(p + 1) * LANES] = (acc / l).astype(o_ref.dtype)

    fast_ok = bound_scr[0] <= FAST_SOFTMAX_BOUND
    pl.when(fast_ok)(lambda: attend(True))
    pl.when(jnp.logical_not(fast_ok))(lambda: attend(False))


def _gqa_call(z, *, b, t, tq, qname, kname, vname, qn, kn, sink, rope, cache, layer, banded,
              kv_out, name):
    nq = t // tq
    n_ctx = PAST_LEN if cache is not None else 0
    rows = t + n_ctx + (2 * BLOCK if banded else 0)
    ck = min(ATTN_CHUNK, t + n_ctx)
    qblk, kblk, vblk = _COL[qname] // 256, _COL[kname] // LANES, _COL[vname] // LANES
    const = lambda bb, g, i: (0, 0)
    in_specs = [pl.BlockSpec((tq, 256), lambda bb, g, i: (bb * nq + i, qblk + g)),
                pl.BlockSpec((t, LANES), lambda bb, g, i: (bb, kblk)),
                pl.BlockSpec((t, LANES), lambda bb, g, i: (bb, vblk)),
                pl.BlockSpec((1, LANES), const),
                pl.BlockSpec((1, LANES), const)]
    args = [z, z, z, jnp.tile(qn, 2).reshape(1, LANES), jnp.tile(kn, 2).reshape(1, LANES)]
    if rope is not None:
        cos, sin = rope
        in_specs += [pl.BlockSpec((tq, LANES), lambda bb, g, i: (i, 0)),
                     pl.BlockSpec((tq, LANES), lambda bb, g, i: (i, 0)),
                     pl.BlockSpec((t, LANES), const),
                     pl.BlockSpec((t, LANES), const)]
        args += [cos, sin, cos, sin]
    if cache is not None:
        spec = pl.BlockSpec((1, 1, n_ctx, LANES), lambda bb, g, i: (bb, layer, 0, 0))
        in_specs += [spec, spec]
        args += [cache[0], cache[1]]
    if sink is not None:
        in_specs.append(pl.BlockSpec(memory_space=pltpu.SMEM))
        args.append(sink)
    out_shape = [jax.ShapeDtypeStruct((b * t, 512), BF16)]
    out_specs = [pl.BlockSpec((tq, 256), lambda bb, g, i: (bb * nq + i, g))]
    aliases = {}
    if kv_out is not None:
        for n_out, arr in enumerate(kv_out):
            aliases[len(args)] = 1 + n_out
            in_specs.append(pl.BlockSpec(memory_space=pl.ANY))
            args.append(arr)
            out_shape.append(jax.ShapeDtypeStruct(arr.shape, arr.dtype))
            out_specs.append(pl.BlockSpec((1, 1, t, LANES), lambda bb, g, i: (bb, layer, 0, 0)))
    res = pl.pallas_call(
        functools.partial(_gqa_kernel, t=t, tq=tq, n_ctx=n_ctx, banded=banded,
                          has_sink=sink is not None, has_rope=rope is not None,
                          emit_k=kv_out is not None, ck=ck),
        grid=(b, 2, nq),
        in_specs=in_specs,
        out_specs=out_specs,
        out_shape=out_shape,
        input_output_aliases=aliases,
        scratch_shapes=[pltpu.VMEM((2, rows, LANES), BF16), pltpu.VMEM((2, rows, LANES), BF16),
                        pltpu.SMEM((1,), F32)],
        compiler_params=_cparams(),
        name=name,
    )(*args)
    return res[0], tuple(res[1:])


def _diff_kernel(*refs, t, tq, n_ctx, has_rope, emit_k, ck, lam_init):
    it = iter(refs)
    q_ref, k_ref, v_ref, qn_ref, kn_ref, lp_ref, og_ref = (next(it) for _ in range(7))
    if has_rope:
        cq_ref, sq_ref, cka_ref, ska_ref = (next(it) for _ in range(4))
    if n_ctx:
        kc_ref, vc_ref = next(it), next(it)
    if emit_k:
        next(it), next(it)
    o_ref = next(it)
    if emit_k:
        ko_ref, vo_ref = next(it), next(it)
    k_scr, v_scr, bound_scr = next(it), next(it), next(it)

    qi = pl.program_id(2)
    lo = _lane_lo()

    def put(dst, kx, vx):
        zero = jnp.zeros_like(kx)
        k_scr[0, dst, :] = jnp.where(lo, kx, zero).astype(BF16)
        k_scr[1, dst, :] = jnp.where(lo, zero, kx).astype(BF16)
        v_scr[dst, :] = vx.astype(BF16)

    @pl.when(qi == 0)
    def _build():
        step = min(t, 512)
        for r in range(0, t, step):
            kx = _head_rmsnorm(k_ref[r:r + step, :], kn_ref[...])
            if emit_k:
                ko_ref[0, 0, r:r + step, :] = kx
                vo_ref[0, 0, r:r + step, :] = v_ref[r:r + step, :]
            if has_rope:
                kx = _rope(kx, cka_ref[r:r + step, :], ska_ref[r:r + step, :])
            put(slice(r, r + step), kx, v_ref[r:r + step, :])
        if n_ctx:
            put(slice(t, t + n_ctx), kc_ref[0, 0], vc_ref[0, 0])
        bound_scr[0] = _score_bound(qn_ref, kn_ref, kc_ref[0, 0] if n_ctx else None)

    qp = _head_rmsnorm(q_ref[...], qn_ref[...])
    if has_rope:
        qp = _rope(qp, cq_ref[...], sq_ref[...])
    q2 = (qp * (ATTN_SCALE * LOG2E)).astype(BF16)

    def step_fn(start, carry):
        m0, m1, l0, l1, a0, a1 = carry
        v = v_scr[pl.ds(start, ck), :]
        s0 = _dot_nt(q2, k_scr[0, pl.ds(start, ck), :])
        s1 = _dot_nt(q2, k_scr[1, pl.ds(start, ck), :])
        n0 = jnp.maximum(m0, jnp.max(s0, axis=-1, keepdims=True))
        n1 = jnp.maximum(m1, jnp.max(s1, axis=-1, keepdims=True))
        p0 = jnp.exp2(s0 - n0)
        p1 = jnp.exp2(s1 - n1)
        e0 = jnp.exp2(m0 - n0)
        e1 = jnp.exp2(m1 - n1)
        l0 = e0 * l0 + jnp.sum(p0, axis=-1, keepdims=True)
        l1 = e1 * l1 + jnp.sum(p1, axis=-1, keepdims=True)
        a0 = e0 * a0 + _dot(p0.astype(BF16), v)
        a1 = e1 * a1 + _dot(p1.astype(BF16), v)
        return n0, n1, l0, l1, a0, a1

    def fast_step(start, carry):
        ls0, ls1, a0, a1 = carry
        v = v_scr[pl.ds(start, ck), :]
        p0 = jnp.exp2(_dot_nt(q2, k_scr[0, pl.ds(start, ck), :]))
        p1 = jnp.exp2(_dot_nt(q2, k_scr[1, pl.ds(start, ck), :]))
        return (ls0 + _lane_tile_sum(p0), ls1 + _lane_tile_sum(p1),
                a0 + _dot(p0.astype(BF16), v), a1 + _dot(p1.astype(BF16), v))

    def finish(l0, l1, a0, a1):
        lp = lp_ref[...]
        lam = (jnp.exp(jnp.sum(lp[0:1] * lp[1:2], axis=-1, keepdims=True))
               - jnp.exp(jnp.sum(lp[2:3] * lp[3:4], axis=-1, keepdims=True)) + lam_init)
        o = a0 / l0 - lam * (a1 / l1)
        var = jnp.mean(o * o, axis=-1, keepdims=True)
        o = o * lax.rsqrt(var + RMS_EPS) * og_ref[...] * (1.0 - lam_init)
        o_ref[...] = o.astype(o_ref.dtype)

    zeros = jnp.zeros((tq, LANES), F32)
    starts = [c * ck for c in range((t + n_ctx) // ck)]

    def attend_fast():
        carry = (zeros, zeros, zeros, zeros)
        for start in starts:
            carry = fast_step(start, carry)
        ls0, ls1, a0, a1 = carry
        finish(jnp.sum(ls0, axis=-1, keepdims=True), jnp.sum(ls1, axis=-1, keepdims=True), a0, a1)

    def attend_online():
        col = lambda val: jnp.full((tq, 1), val, F32)
        carry = (col(NEG), col(NEG), col(0.0), col(0.0), zeros, zeros)
        for start in starts:
            carry = step_fn(start, carry)
        finish(*carry[2:])

    fast_ok = bound_scr[0] <= FAST_SOFTMAX_BOUND
    pl.when(fast_ok)(attend_fast)
    pl.when(jnp.logical_not(fast_ok))(attend_online)


def _diff_call(z, *, b, t, tq, qn, kn, lparams, out_g, rope, cache, layer, kv_out, name):
    nq = t // tq
    n_ctx = PAST_LEN if cache is not None else 0
    ck = min(ATTN_CHUNK, t + n_ctx)
    qblk, kblk, vblk = _COL['qd'] // LANES, _COL['kd'] // LANES, _COL['vd'] // LANES
    const = lambda bb, h, i: (0, 0)
    in_specs = [pl.BlockSpec((tq, LANES), lambda bb, h, i: (bb * nq + i, qblk + h)),
                pl.BlockSpec((t, LANES), lambda bb, h, i: (bb, kblk + h)),
                pl.BlockSpec((t, LANES), lambda bb, h, i: (bb, vblk + h)),
                pl.BlockSpec((1, LANES), const),
                pl.BlockSpec((1, LANES), const),
                pl.BlockSpec((4, HEAD_DIM), const),
                pl.BlockSpec((1, LANES), const)]
    args = [z, z, z, jnp.tile(qn, 2).reshape(1, LANES), jnp.tile(kn, 2).reshape(1, LANES),
            lparams, out_g.reshape(1, LANES)]
    if rope is not None:
        cos, sin = rope
        in_specs += [pl.BlockSpec((tq, LANES), lambda bb, h, i: (i, 0)),
                     pl.BlockSpec((tq, LANES), lambda bb, h, i: (i, 0)),
                     pl.BlockSpec((t, LANES), const),
                     pl.BlockSpec((t, LANES), const)]
        args += [cos, sin, cos, sin]
    if cache is not None:
        spec = pl.BlockSpec((1, 1, n_ctx, LANES), lambda bb, h, i: (bb, layer, 0, h))
        in_specs += [spec, spec]
        args += [cache[0], cache[1]]
    out_shape = [jax.ShapeDtypeStruct((b * t, 512), BF16)]
    out_specs = [pl.BlockSpec((tq, LANES), lambda bb, h, i: (bb * nq + i, h))]
    aliases = {}
    if kv_out is not None:
        for n_out, arr in enumerate(kv_out):
            aliases[len(args)] = 1 + n_out
            in_specs.append(pl.BlockSpec(memory_space=pl.ANY))
            args.append(arr)
            out_shape.append(jax.ShapeDtypeStruct(arr.shape, arr.dtype))
            out_specs.append(pl.BlockSpec((1, 1, t, LANES), lambda bb, h, i: (bb, layer, 0, h)))
    lam_init = 0.8 - 0.6 * math.exp(-0.3 * layer)
    res = pl.pallas_call(
        functools.partial(_diff_kernel, t=t, tq=tq, n_ctx=n_ctx, has_rope=rope is not None,
                          emit_k=kv_out is not None, ck=ck, lam_init=lam_init),
        grid=(b, DIFF_HEADS, nq),
        in_specs=in_specs,
        out_specs=out_specs,
        out_shape=out_shape,
        input_output_aliases=aliases,
        scratch_shapes=[pltpu.VMEM((2, t + n_ctx, LANES), BF16), pltpu.VMEM((t + n_ctx, LANES), BF16),
                        pltpu.SMEM((1,), F32)],
        compiler_params=_cparams(),
        name=name,
    )(*args)
    return res[0], tuple(res[1:])


def _softplus(x):
    return jnp.maximum(x, 0.0) + jnp.log1p(jnp.exp(-jnp.abs(x)))


def _lru_kernel(*refs, tc, nc, has_h0, emit_final):
    it = iter(refs)
    xp_ref, xm_ref, xn_ref, ya_ref, cw_ref, cb_ref, wl_ref, bl_ref, lam_ref = (next(it) for _ in range(9))
    if has_h0:
        h0_ref = next(it)
    if emit_final:
        next(it)
    o_ref = next(it)
    if emit_final:
        fin_ref = next(it)
    hf_scr, a_scr, b_scr, carry_scr = (next(it) for _ in range(4))

    s = pl.program_id(1)
    fwd = s < nc
    c = jnp.where(fwd, s, 2 * nc - 1 - s)
    t0 = pl.multiple_of(c * tc, tc)

    @pl.when(s == 0)
    def _():
        if has_h0:
            carry_scr[...] = h0_ref[0, 0]
        else:
            carry_scr[...] = jnp.zeros_like(carry_scr)

    prev = jnp.where(c > 0, xp_ref[...], 0.0)
    nxt = jnp.where(c < nc - 1, xn_ref[...], 0.0)
    ext = jnp.concatenate([prev, xm_ref[...], nxt], axis=0)
    n_ext = tc + 2 * SUBLANES
    u = cb_ref[...] + cw_ref[1:2] * xm_ref[...]
    u = u + cw_ref[0:1] * pltpu.roll(ext, 1, 0)[SUBLANES:SUBLANES + tc]
    u = u + cw_ref[2:3] * pltpu.roll(ext, n_ext - 1, 0)[SUBLANES:SUBLANES + tc]
    u = u + cw_ref[3:4] * pltpu.roll(ext, n_ext - 2, 0)[SUBLANES:SUBLANES + tc]

    gates = _dot(u.astype(BF16), wl_ref[0]) + bl_ref[0]
    r = jax.nn.sigmoid(gates[:, :LRU_W])
    ig = jax.nn.sigmoid(gates[:, LRU_W:])
    log_a = -LRU_C * r * _softplus(-lam_ref[0])
    a = jnp.exp(log_a)
    a_scr[...] = a
    b_scr[...] = jnp.sqrt(-jnp.tanh(log_a) * (a * a + 1.0)) * ig * u

    ntile = tc // SUBLANES
    row = lax.broadcasted_iota(jnp.int32, (SUBLANES, LRU_W), 0)

    def scan(forward):
        def tile(i, carry):
            j = i if forward else ntile - 1 - i
            r0 = pl.multiple_of(j * SUBLANES, SUBLANES)
            a = a_scr[pl.ds(r0, SUBLANES), :]
            bv = b_scr[pl.ds(r0, SUBLANES), :]
            for d in (1, 2, 4):
                shift = d if forward else SUBLANES - d
                msk = (row >= d) if forward else (row < SUBLANES - d)
                ap = pltpu.roll(a, shift, 0)
                bp = pltpu.roll(bv, shift, 0)
                bv = jnp.where(msk, a * bp + bv, bv)
                a = jnp.where(msk, a * ap, a)
            h = a * carry + bv
            g0 = pl.multiple_of(t0 + r0, SUBLANES)
            if forward:
                hf_scr[pl.ds(g0, SUBLANES), :] = h
                return h[SUBLANES - 1:SUBLANES, :]
            b_scr[pl.ds(r0, SUBLANES), :] = h
            return h[0:1, :]

        idx = 0 if forward else 1
        last = lax.fori_loop(0, ntile, tile, carry_scr[idx:idx + 1, :])
        carry_scr[idx:idx + 1, :] = last
        if emit_final:
            fin_ref[0, 0, idx:idx + 1, :] = last
        if not forward:
            o_ref[...] = ((hf_scr[pl.ds(t0, tc), :] + b_scr[...])
                          * jax.nn.gelu(ya_ref[...])).astype(o_ref.dtype)

    pl.when(fwd)(lambda: scan(True))
    pl.when(jnp.logical_not(fwd))(lambda: scan(False))


def _lru_call(z, *, b, t, conv_w, conv_b, w_lru, b_lru, lam, h0, state_out, layer, name):
    tc = min(t, 512)
    nc = t // tc
    per8 = tc // SUBLANES
    nrow8 = b * t // SUBLANES

    def chunk(s):
        return jnp.where(s < nc, s, 2 * nc - 1 - s)

    def hold(s):
        return jnp.where(s < nc, nc - 1, 2 * nc - 1 - s)

    in_specs = [
        pl.BlockSpec((SUBLANES, LRU_W),
                     lambda bb, s: (jnp.maximum((bb * nc + chunk(s)) * per8 - 1, 0), 0)),
        pl.BlockSpec((tc, LRU_W), lambda bb, s: (bb * nc + chunk(s), 0)),
        pl.BlockSpec((SUBLANES, LRU_W),
                     lambda bb, s: (jnp.minimum((bb * nc + chunk(s) + 1) * per8, nrow8 - 1), 0)),
        pl.BlockSpec((tc, LRU_W), lambda bb, s: (bb * nc + hold(s), 1)),
        pl.BlockSpec((CONV_W, LRU_W), lambda bb, s: (0, 0)),
        pl.BlockSpec((1, LRU_W), lambda bb, s: (0, 0)),
        pl.BlockSpec((1, LRU_W, 2 * LRU_W), lambda bb, s: (s // nc, 0, 0)),
        pl.BlockSpec((1, 1, 2 * LRU_W), lambda bb, s: (s // nc, 0, 0)),
        pl.BlockSpec((1, 1, LRU_W), lambda bb, s: (s // nc, 0, 0)),
    ]
    args = [z, z, z, z, conv_w, conv_b.reshape(1, LRU_W), w_lru, b_lru, lam.reshape(2, 1, LRU_W)]
    if h0 is not None:
        in_specs.append(pl.BlockSpec((1, 1, 2, LRU_W), lambda bb, s: (bb, layer, 0, 0)))
        args.append(h0)
    out_shape = [jax.ShapeDtypeStruct((b * t, LRU_W), BF16)]
    out_specs = [pl.BlockSpec((tc, LRU_W), lambda bb, s: (bb * nc + hold(s), 0))]
    aliases = {}
    if state_out is not None:
        aliases[len(args)] = 1
        in_specs.append(pl.BlockSpec(memory_space=pl.ANY))
        args.append(state_out)
        out_shape.append(jax.ShapeDtypeStruct(state_out.shape, state_out.dtype))
        out_specs.append(pl.BlockSpec((1, 1, 2, LRU_W), lambda bb, s: (bb, layer, 0, 0)))
    res = pl.pallas_call(
        functools.partial(_lru_kernel, tc=tc, nc=nc, has_h0=h0 is not None,
                          emit_final=state_out is not None),
        grid=(b, 2 * nc),
        in_specs=in_specs,
        out_specs=out_specs,
        out_shape=out_shape,
        input_output_aliases=aliases,
        scratch_shapes=[pltpu.VMEM((t, LRU_W), F32), pltpu.VMEM((tc, LRU_W), F32),
                        pltpu.VMEM((tc, LRU_W), F32), pltpu.VMEM((2, LRU_W), F32)],
        compiler_params=_cparams(),
        name=name,
    )(*args)
    return res[0], (res[1] if state_out is not None else None)


def _merge_kernel(x_ref, gt_ref, a_ref, w_ref, g_ref, d_ref, wb_ref, wo_ref, mod_ref, o_ref):
    merged = None
    for bi, br_ref in enumerate((a_ref, w_ref, g_ref, d_ref)):
        gate = gt_ref[:, bi * D_MODEL:(bi + 1) * D_MODEL].astype(F32)
        term = gate * _dot(br_ref[...], wb_ref[0, bi])
        merged = term if merged is None else merged + term
    o_ref[...] = x_ref[...] + mod_ref[0][2:3] * _dot(merged.astype(BF16), wo_ref[0])


def _merge_call(x, gates, branches, w_branch, w_o, layer, mod, latent, name):
    m = x.shape[0]
    tm = 256
    row = lambda i: (i, 0)
    resident = pl.Buffered(1)
    in_specs = [pl.BlockSpec((tm, D_MODEL), row),
                pl.BlockSpec((tm, N_BRANCH * D_MODEL), row)]
    in_specs += [pl.BlockSpec((tm, BRANCH_W), row)] * N_BRANCH
    in_specs += [pl.BlockSpec((1, N_BRANCH, BRANCH_W, D_MODEL), lambda i: (layer, 0, 0, 0),
                              pipeline_mode=resident),
                 pl.BlockSpec((1, D_MODEL, D_MODEL), lambda i: (layer, 0, 0), pipeline_mode=resident),
                 pl.BlockSpec((1, N_MOD, D_MODEL), _mod_index(latent, tm))]
    return pl.pallas_call(
        _merge_kernel,
        grid=(m // tm,),
        in_specs=in_specs,
        out_specs=pl.BlockSpec((tm, D_MODEL), row),
        out_shape=jax.ShapeDtypeStruct((m, D_MODEL), F32),
        compiler_params=_cparams(),
        name=name,
    )(x, gates, *branches, w_branch, w_o, mod)


def _ffn_kernel(x_ref, mod_ref, g_ref, w1_ref, b1_ref, w2_ref, b2_ref, o_ref, h_scr, acc_scr):
    j = pl.program_id(1)

    def chunk(h):
        a = jnp.maximum(_dot(h, w1_ref[0]) + b1_ref[...], 0.0)
        return _dot((a * a).astype(BF16), w2_ref[0])

    @pl.when(j == 0)
    def _():
        for r in range(0, x_ref.shape[0], NORM_PIECE):
            rows = slice(r, r + NORM_PIECE)
            h = _norm_mod(x_ref[rows, :], g_ref[...], mod_ref[0], 3, 4).astype(BF16)
            h_scr[rows, :] = h
            acc_scr[rows, :] = chunk(h)

    @pl.when(j > 0)
    def _():
        acc_scr[...] += chunk(h_scr[...])

    @pl.when(j == pl.num_programs(1) - 1)
    def _():
        o_ref[...] = x_ref[...] + mod_ref[0][5:6] * (acc_scr[...] + b2_ref[...])


def _ffn_call(x, mod, g, w1, b1, w2, b2, layer, latent, name):
    m = x.shape[0]
    tm, tf = 512, 1024
    return pl.pallas_call(
        _ffn_kernel,
        grid=(m // tm, D_FF // tf),
        in_specs=[pl.BlockSpec((tm, D_MODEL), lambda i, j: (i, 0)),
                  pl.BlockSpec((1, N_MOD, D_MODEL), _mod_index(latent, tm)),
                  pl.BlockSpec((1, D_MODEL), lambda i, j: (0, 0)),
                  pl.BlockSpec((1, D_MODEL, tf), lambda i, j: (layer, 0, j)),
                  pl.BlockSpec((1, tf), lambda i, j: (0, j)),
                  pl.BlockSpec((1, tf, D_MODEL), lambda i, j: (layer, j, 0)),
                  pl.BlockSpec((1, D_MODEL), lambda i, j: (0, 0))],
        out_specs=pl.BlockSpec((tm, D_MODEL), lambda i, j: (i, 0)),
        out_shape=jax.ShapeDtypeStruct((m, D_MODEL), F32),
        scratch_shapes=[pltpu.VMEM((tm, D_MODEL), BF16), pltpu.VMEM((tm, D_MODEL), F32)],
        compiler_params=_cparams(),
        name=name,
    )(x, mod, g.reshape(1, D_MODEL), w1, b1.reshape(1, D_FF), w2, b2.reshape(1, D_MODEL))


def _block_diag(w):
    eye = jnp.eye(LRU_BLOCKS, dtype=w.dtype)
    return jnp.einsum('ncd,nm->ncmd', w, eye).reshape(LRU_W, LRU_W)


def _trunk_layer(x, mod, wts, layer, latent, rope, cached, new):
    b, t = (DEC_BATCH, DEC_SEQ) if latent else (BATCH, SEQ)
    tag = ('lat' if latent else 'ctx') + str(layer)
    gates, h = _gates_call(x, mod, wts['norm1_g'], wts['w_gate'], layer, wts['b_gate'], latent,
                           'gates_' + tag)
    z = _inproj_call(h, wts['w_in'], layer, 'inproj_' + tag)
    a_out, new_state = _lru_call(
        z, b=b, t=t, conv_w=wts['conv_w'], conv_b=wts['conv_b'], w_lru=wts['w_lru'], b_lru=wts['b_lru'],
        lam=wts['lru_lambda'], h0=cached[6] if latent else None, state_out=None if latent else new[6],
        layer=layer, name='lru_' + tag)
    w_out, new_win = _gqa_call(
        z, b=b, t=t, tq=256, qname='qw', kname='kw', vname='vw',
        qn=wts['win_qn'], kn=wts['win_kn'], sink=wts['win_sink'], rope=rope,
        cache=(cached[0], cached[1]) if latent else None, layer=layer, banded=latent,
        kv_out=None if latent else new[0:2], name='win_' + tag)
    g_out, new_grid = _gqa_call(
        z, b=b, t=t, tq=256, qname='qg', kname='kg', vname='vg',
        qn=wts['grid_qn'], kn=wts['grid_kn'], sink=None, rope=rope,
        cache=(cached[2], cached[3]) if latent else None, layer=layer, banded=False,
        kv_out=None if latent else new[2:4], name='grid_' + tag)
    d_out, new_diff = _diff_call(
        z, b=b, t=t, tq=256, qn=wts['diff_qn'], kn=wts['diff_kn'], lparams=wts['diff_lp'],
        out_g=wts['diff_out_g'], rope=rope, cache=(cached[4], cached[5]) if latent else None,
        layer=layer, kv_out=None if latent else new[4:6], name='diff_' + tag)
    x = _merge_call(x, gates, (a_out, w_out, g_out, d_out), wts['w_branch'], wts['w_o'], layer, mod,
                    latent, 'merge_' + tag)
    x = _ffn_call(x, mod, wts['norm2_g'], wts['w_ff1'], wts['b_ff1'], wts['w_ff2'], wts['b_ff2'],
                  layer, latent, 'ffn_' + tag)
    if latent:
        return x, None
    return x, new_win + new_grid + new_diff + (new_state,)


def kernel(x_prompt, x_sample, cache_win_k, cache_win_v, cache_grid_k, cache_grid_v, cache_diff_k, cache_diff_v, state_lru, c, c_ctx, w_ada, b_ada, norm1_g, norm2_g, w_in, conv_w, conv_b, lru_wr, lru_br, lru_wi, lru_bi, lru_lambda, win_qn, win_kn, win_sink, grid_qn, grid_kn, diff_qn, diff_kn, diff_lq1, diff_lk1, diff_lq2, diff_lk2, diff_out_g, w_branch, w_gate, b_gate, w_o, w_ff1, b_ff1, w_ff2, b_ff2):
    cond = jnp.zeros((MOD_ROWS, D_MODEL), F32).at[0].set(c_ctx).at[1:1 + DEC_BATCH].set(c)
    mod_all = _modulation(cond, w_ada, b_ada).reshape(DEPTH, MOD_ROWS, N_MOD, D_MODEL)
    rope = _rope_tables(DEC_SEQ)
    cached_all = (cache_win_k.reshape(DEC_BATCH, DEPTH, PAST_LEN, LANES),
                  cache_win_v.reshape(DEC_BATCH, DEPTH, PAST_LEN, LANES),
                  cache_grid_k.reshape(DEC_BATCH, DEPTH, PAST_LEN, LANES),
                  cache_grid_v.reshape(DEC_BATCH, DEPTH, PAST_LEN, LANES),
                  cache_diff_k.reshape(DEC_BATCH, DEPTH, PAST_LEN, 512),
                  cache_diff_v.reshape(DEC_BATCH, DEPTH, PAST_LEN, 512))
    w_in_b, w_gate_b, w_branch_b, w_o_b, w_ff1_b, w_ff2_b = (
        w.astype(BF16) for w in (w_in, w_gate, w_branch, w_o, w_ff1, w_ff2))
    y_p = x_prompt.reshape(BATCH * SEQ, D_MODEL)
    y_s = x_sample.reshape(DEC_BATCH * DEC_SEQ, D_MODEL)
    new = tuple(jnp.zeros((BATCH, DEPTH, SEQ, w), F32) for w in (LANES,) * 4 + (512,) * 2)
    new += (jnp.zeros((BATCH, DEPTH, 2, LRU_W), F32),)
    for l in range(DEPTH):
        w_lru = jnp.stack([
            jnp.concatenate([_block_diag(lru_wr[l, k]), _block_diag(lru_wi[l, k])], axis=1)
            for k in range(2)]).astype(BF16)
        b_lru = jnp.concatenate([lru_br[l], lru_bi[l]], axis=-1).reshape(2, 1, 2 * LRU_W)
        wts = {
            'norm1_g': norm1_g[l], 'norm2_g': norm2_g[l],
            'w_in': w_in_b, 'w_gate': w_gate_b, 'b_gate': b_gate[l],
            'conv_w': conv_w[l], 'conv_b': conv_b[l], 'w_lru': w_lru, 'b_lru': b_lru,
            'lru_lambda': lru_lambda[l],
            'win_qn': win_qn[l], 'win_kn': win_kn[l], 'win_sink': win_sink[l],
            'grid_qn': grid_qn[l], 'grid_kn': grid_kn[l],
            'diff_qn': diff_qn[l], 'diff_kn': diff_kn[l],
            'diff_lp': jnp.stack([diff_lq1[l], diff_lk1[l], diff_lq2[l], diff_lk2[l]]),
            'diff_out_g': diff_out_g[l],
            'w_branch': w_branch_b, 'w_o': w_o_b,
            'w_ff1': w_ff1_b, 'b_ff1': b_ff1[l], 'w_ff2': w_ff2_b, 'b_ff2': b_ff2[l],
        }
        y_p, new = _trunk_layer(y_p, mod_all[l], wts, l, False, None, None, new)
        y_s, _ = _trunk_layer(y_s, mod_all[l], wts, l, True, rope, cached_all + (state_lru,), None)

    kv_shape = (BATCH, DEPTH, SEQ, WIN_KV, HEAD_DIM)
    return (y_p.reshape(BATCH, SEQ, D_MODEL),
            y_s.reshape(DEC_BATCH, DEC_SEQ, D_MODEL),
            new[0].reshape(kv_shape), new[1].reshape(kv_shape),
            new[2].reshape(kv_shape), new[3].reshape(kv_shape),
            new[4].reshape(BATCH, DEPTH, SEQ, DIFF_HEADS, 2, HEAD_DIM),
            new[5].reshape(BATCH, DEPTH, SEQ, DIFF_HEADS, 2 * HEAD_DIM),
            new[6])
```

```python
import functools
import math

import jax
import jax.numpy as jnp
import numpy as np
from jax import lax
from jax.experimental import pallas as pl
from jax.experimental.pallas import tpu as pltpu

F32 = jnp.float32
BF16 = jnp.bfloat16

D_MODEL = 2048
BATCH = 32
SEQ = 256
DEPTH = 4
DEC_BATCH = 4
DEC_SEQ = 4096
PAST_LEN = 512
GRID_W = 64
BLOCK = 128
HEAD_DIM = 64
N_FREQ = HEAD_DIM // 4
ROPE_BASE = 10000.0
ATTN_SCALE = HEAD_DIM ** -0.5
RMS_EPS = 1e-6
N_MOD = 6
N_BRANCH = 4
BRANCH_W = D_MODEL // 4
LRU_W = BRANCH_W
LRU_BLOCKS = 8
LRU_BW = LRU_W // LRU_BLOCKS
LRU_C = 8.0
CONV_W = 4
WIN_HEADS = 8
WIN_KV = 2
DIFF_HEADS = 4
D_FF = 4 * D_MODEL
D_IN = 4096

V7X_VMEM_BYTES = 64 * 1024 * 1024
VMEM_LIMIT = V7X_VMEM_BYTES - 8 * 1024 * 1024
LANES = 128
SUBLANES = 8
NEG = -1e30
LOG2E = math.log2(math.e)
ATTN_CHUNK = 1536
FAST_SOFTMAX_BOUND = 40.0
NORM_PIECE = 256
MOD_ROWS = 8

_COL = dict(xa=0, ya=512, qw=1024, kw=1536, vw=1664, qg=1792, kg=2304, vg=2432, qd=2560, kd=3072,
            vd=3584)


def _cparams():
    return pltpu.CompilerParams(vmem_limit_bytes=VMEM_LIMIT)


def _dot(a, b):
    return jnp.dot(a, b, preferred_element_type=F32)


def _dot_nt(a, b):
    return lax.dot_general(a, b, (((1,), (1,)), ((), ())), preferred_element_type=F32)


def _mod_kernel(c_ref, w_ref, b_ref, o_ref):
    c = c_ref[...]
    s = (c * jax.nn.sigmoid(c)).astype(BF16)
    o_ref[0] = _dot(s, w_ref[0].astype(BF16)) + b_ref[0]


def _modulation(cond, w_ada, b_ada):
    tn = 1024
    n = N_MOD * D_MODEL
    return pl.pallas_call(
        _mod_kernel,
        grid=(DEPTH, n // tn),
        in_specs=[pl.BlockSpec((MOD_ROWS, D_MODEL), lambda l, j: (0, 0)),
                  pl.BlockSpec((1, D_MODEL, tn), lambda l, j: (l, 0, j)),
                  pl.BlockSpec((1, 1, tn), lambda l, j: (l, 0, j))],
        out_specs=pl.BlockSpec((1, MOD_ROWS, tn), lambda l, j: (l, 0, j)),
        out_shape=jax.ShapeDtypeStruct((DEPTH, MOD_ROWS, n), F32),
        compiler_params=_cparams(),
        name='modulation',
    )(cond, w_ada, b_ada.reshape(DEPTH, 1, n))


def _mod_index(latent, tm):
    if latent:
        per = DEC_SEQ // tm
        return lambda i, *_: (1 + i // per, 0, 0)
    return lambda i, *_: (0, 0, 0)


def _norm_mod(x, g, mod, shift_idx, scale_idx):
    var = jnp.mean(x * x, axis=-1, keepdims=True)
    y = x * lax.rsqrt(var + RMS_EPS) * g
    return y * (1.0 + mod[scale_idx:scale_idx + 1]) + mod[shift_idx:shift_idx + 1]


def _gates_kernel(x_ref, mod_ref, g_ref, w_ref, b_ref, o_ref, h_ref):
    def gate(h):
        return jax.nn.sigmoid(_dot(h, w_ref[0]) + b_ref[...]).astype(o_ref.dtype)

    @pl.when(pl.program_id(1) == 0)
    def _():
        for r in range(0, x_ref.shape[0], NORM_PIECE):
            rows = slice(r, r + NORM_PIECE)
            h = _norm_mod(x_ref[rows, :], g_ref[...], mod_ref[0], 0, 1).astype(BF16)
            h_ref[rows, :] = h
            o_ref[rows, :] = gate(h)

    @pl.when(pl.program_id(1) > 0)
    def _():
        o_ref[...] = gate(h_ref[...])


def _gates_call(x, mod, g, w, layer, bias, latent, name):
    m, n = x.shape[0], w.shape[2]
    tm, tn = 1024, 1024
    return pl.pallas_call(
        _gates_kernel,
        grid=(m // tm, n // tn),
        in_specs=[pl.BlockSpec((tm, D_MODEL), lambda i, j: (i, 0)),
                  pl.BlockSpec((1, N_MOD, D_MODEL), _mod_index(latent, tm)),
                  pl.BlockSpec((1, D_MODEL), lambda i, j: (0, 0)),
                  pl.BlockSpec((1, D_MODEL, tn), lambda i, j: (layer, 0, j)),
                  pl.BlockSpec((1, tn), lambda i, j: (0, j))],
        out_specs=[pl.BlockSpec((tm, tn), lambda i, j: (i, j)),
                   pl.BlockSpec((tm, D_MODEL), lambda i, j: (i, 0))],
        out_shape=[jax.ShapeDtypeStruct((m, n), BF16), jax.ShapeDtypeStruct((m, D_MODEL), BF16)],
        compiler_params=_cparams(),
        name=name,
    )(x, mod, g.reshape(1, D_MODEL), w, bias.reshape(1, n))


def _inproj_kernel(h_ref, w_ref, o_ref):
    o_ref[...] = _dot(h_ref[...], w_ref[0])


def _inproj_call(h, w, layer, name):
    m, n = h.shape[0], w.shape[2]
    tm, tn = 1024, 2048
    return pl.pallas_call(
        _inproj_kernel,
        grid=(m // tm, n // tn),
        in_specs=[pl.BlockSpec((tm, D_MODEL), lambda i, j: (i, 0)),
                  pl.BlockSpec((1, D_MODEL, tn), lambda i, j: (layer, 0, j))],
        out_specs=pl.BlockSpec((tm, tn), lambda i, j: (i, j)),
        out_shape=jax.ShapeDtypeStruct((m, n), F32),
        compiler_params=_cparams(),
        name=name,
    )(h, w)


def _lane_lo():
    return lax.broadcasted_iota(jnp.int32, (1, LANES), 1) < HEAD_DIM


def _seg_matrix():
    r = lax.broadcasted_iota(jnp.int32, (LANES, LANES), 0) // HEAD_DIM
    c = lax.broadcasted_iota(jnp.int32, (LANES, LANES), 1) // HEAD_DIM
    return jnp.where(r == c, 1.0, 0.0).astype(BF16)


def _head_rmsnorm(x, gain):
    x2 = x * x
    hi = x2.astype(BF16)
    lo = (x2 - hi.astype(F32)).astype(BF16)
    seg = _seg_matrix()
    ms = (_dot(hi, seg) + _dot(lo, seg)) * (1.0 / HEAD_DIM)
    return x * lax.rsqrt(ms + RMS_EPS) * gain


def _rope(x, cos, sin_signed):
    lane = lax.broadcasted_iota(jnp.int32, (1, LANES), 1)
    first = (lane & (2 * N_FREQ - 1)) < N_FREQ
    up = pltpu.roll(x, LANES - N_FREQ, 1)
    dn = pltpu.roll(x, N_FREQ, 1)
    return x * cos + jnp.where(first, up, dn) * sin_signed


def _score_bound(qn_ref, kn_ref, cached_k):
    root = math.sqrt(HEAD_DIM)
    qmax = root * jnp.max(jnp.abs(qn_ref[...]))
    kmax = root * jnp.max(jnp.abs(kn_ref[...]))
    if cached_k is not None:
        kmax = jnp.maximum(kmax, jnp.sqrt(jnp.max(jnp.sum(cached_k * cached_k, axis=-1, keepdims=True))))
    return qmax * kmax * (ATTN_SCALE * LOG2E)


def _lane_tile_sum(p):
    acc = p[:, 0:LANES]
    for j in range(1, p.shape[1] // LANES):
        acc = acc + p[:, j * LANES:(j + 1) * LANES]
    return acc


def _rope_tables(t):
    pos = jnp.arange(t)
    row = (pos // GRID_W).astype(F32)
    col = (pos % GRID_W).astype(F32)
    inv = ROPE_BASE ** (-jnp.arange(N_FREQ, dtype=F32) / N_FREQ)
    ar, ac = row[:, None] * inv, col[:, None] * inv
    cos = jnp.concatenate([jnp.cos(ar), jnp.cos(ar), jnp.cos(ac), jnp.cos(ac)], axis=-1)
    sin = jnp.concatenate([-jnp.sin(ar), jnp.sin(ar), -jnp.sin(ac), jnp.sin(ac)], axis=-1)
    return jnp.tile(cos, (1, 2)), jnp.tile(sin, (1, 2))


def _gqa_kernel(*refs, t, tq, n_ctx, banded, has_sink, has_rope, emit_k, ck, gps):
    it = iter(refs)
    q_refs = [next(it) for _ in range(gps)]
    k_ref, v_ref, qn_ref, kn_ref = (next(it) for _ in range(4))
    if has_rope:
        cq_ref, sq_ref, cka_ref, ska_ref = (next(it) for _ in range(4))
    if n_ctx:
        kc_ref, vc_ref = next(it), next(it)
    if has_sink:
        sink_ref = next(it)
    if emit_k:
        next(it), next(it)
    o_ref = next(it)
    if emit_k:
        ko_ref, vo_ref = next(it), next(it)
    k_scr, v_scr, bound_scr = next(it), next(it), next(it)

    qi = pl.program_id(2)
    lo = _lane_lo()
    lat0 = BLOCK if banded else 0
    ctx0 = t + 2 * lat0
    groups = [0, 1] if gps == 2 else [pl.program_id(1)]

    def pick(g, own, other):
        if isinstance(g, int):
            return own if g == 0 else other
        return jnp.where(g == 0, own, other)

    def put(dst, kx, vx):
        kr = pltpu.roll(kx, HEAD_DIM, 1)
        vr = pltpu.roll(vx, HEAD_DIM, 1)
        zero = jnp.zeros_like(kx)
        for gg, g in enumerate(groups):
            k_scr[2 * gg, dst, :] = jnp.where(lo, pick(g, kx, kr), zero).astype(BF16)
            k_scr[2 * gg + 1, dst, :] = jnp.where(lo, zero, pick(g, kr, kx)).astype(BF16)
            v_scr[2 * gg, dst, :] = jnp.where(lo, pick(g, vx, vr), zero).astype(BF16)
            v_scr[2 * gg + 1, dst, :] = jnp.where(lo, zero, pick(g, vr, vx)).astype(BF16)

    @pl.when(qi == 0)
    def _build():
        step = min(t, 512)
        for r in range(0, t, step):
            kx = _head_rmsnorm(k_ref[r:r + step, :], kn_ref[...])
            if emit_k:
                ko_ref[0, 0, r:r + step, :] = kx
                vo_ref[0, 0, r:r + step, :] = v_ref[r:r + step, :]
            if has_rope:
                kx = _rope(kx, cka_ref[r:r + step, :], ska_ref[r:r + step, :])
            put(slice(lat0 + r, lat0 + r + step), kx, v_ref[r:r + step, :])
        if banded:
            zpad = jnp.zeros((BLOCK, LANES), BF16)
            for scr in (k_scr, v_scr):
                for var in range(2 * gps):
                    scr[var, 0:BLOCK, :] = zpad
                    scr[var, lat0 + t:lat0 + t + BLOCK, :] = zpad
        if n_ctx:
            put(slice(ctx0, ctx0 + n_ctx), kc_ref[0, 0], vc_ref[0, 0])
        bound = _score_bound(qn_ref, kn_ref, kc_ref[0, 0] if n_ctx else None)
        if has_sink:
            for h in range(WIN_HEADS):
                bound = jnp.maximum(bound, jnp.abs(sink_ref[h]) * LOG2E)
        bound_scr[0] = bound

    def step_fn(base, q2, segs, carry):
        m0, m1, l, acc = carry

        def scores(var):
            parts = []
            for start, size, mask in segs:
                s = _dot_nt(q2, k_scr[var, pl.ds(start, size), :])
                parts.append(s if mask is None else jnp.where(mask, s, NEG))
            return parts[0] if len(parts) == 1 else jnp.concatenate(parts, axis=1)

        s0, s1 = scores(base), scores(base + 1)
        n0 = jnp.maximum(m0, jnp.max(s0, axis=-1, keepdims=True))
        n1 = jnp.maximum(m1, jnp.max(s1, axis=-1, keepdims=True))
        p0 = jnp.exp2(s0 - n0)
        p1 = jnp.exp2(s1 - n1)
        alpha = jnp.where(lo, jnp.exp2(m0 - n0), jnp.exp2(m1 - n1))
        rs = jnp.where(lo, jnp.sum(p0, axis=-1, keepdims=True), jnp.sum(p1, axis=-1, keepdims=True))
        l = alpha * l + rs
        acc = alpha * acc
        p0, p1 = p0.astype(BF16), p1.astype(BF16)
        off = 0
        for start, size, _ in segs:
            acc = (acc + _dot(p0[:, off:off + size], v_scr[base, pl.ds(start, size), :])
                   + _dot(p1[:, off:off + size], v_scr[base + 1, pl.ds(start, size), :]))
            off += size
        return n0, n1, l, acc

    def fast_step(base, q2, segs, carry):
        ls0, ls1, acc = carry
        for start, size, mask in segs:
            s0 = _dot_nt(q2, k_scr[base, pl.ds(start, size), :])
            s1 = _dot_nt(q2, k_scr[base + 1, pl.ds(start, size), :])
            if mask is not None:
                s0 = jnp.where(mask, s0, NEG)
                s1 = jnp.where(mask, s1, NEG)
            p0 = jnp.exp2(s0)
            p1 = jnp.exp2(s1)
            ls0 = ls0 + _lane_tile_sum(p0)
            ls1 = ls1 + _lane_tile_sum(p1)
            acc = (acc + _dot(p0.astype(BF16), v_scr[base, pl.ds(start, size), :])
                   + _dot(p1.astype(BF16), v_scr[base + 1, pl.ds(start, size), :]))
        return ls0, ls1, acc

    if banded:
        span = tq + 2 * BLOCK
        rr = lax.broadcasted_iota(jnp.int32, (tq, span), 0)
        cc = lax.broadcasted_iota(jnp.int32, (tq, span), 1)
        kpos = qi * tq - BLOCK + cc
        band_mask = (cc >= rr) & (cc - rr <= 2 * BLOCK) & (kpos >= 0) & (kpos < t)

    def attend(fast):
        if banded:
            chunks = [[(pl.multiple_of(qi * tq, tq), span, band_mask)]]
            if n_ctx:
                chunks[0].append((ctx0, n_ctx, None))
        else:
            chunks = [[(c * ck, ck, None)] for c in range((t + n_ctx) // ck)]
        zeros = jnp.zeros((tq, LANES), F32)
        for gg, g in enumerate(groups):
            q = q_refs[gg][...]
            for p in range(2):
                qp = _head_rmsnorm(q[:, p * LANES:(p + 1) * LANES], qn_ref[...])
                if has_rope:
                    qp = _rope(qp, cq_ref[...], sq_ref[...])
                q2 = (qp * (ATTN_SCALE * LOG2E)).astype(BF16)
                if has_sink:
                    h0 = g * 4 + 2 * p
                    sink0 = jnp.full((1, 1), sink_ref[h0] * LOG2E, F32)
                    sink1 = jnp.full((1, 1), sink_ref[h0 + 1] * LOG2E, F32)
                if fast:
                    carry = (zeros, zeros, zeros)
                    for segs in chunks:
                        carry = fast_step(2 * gg, q2, segs, carry)
                    ls0, ls1, acc = carry
                    l = jnp.where(lo, jnp.sum(ls0, axis=-1, keepdims=True),
                                  jnp.sum(ls1, axis=-1, keepdims=True))
                    if has_sink:
                        l = l + jnp.where(lo, jnp.exp2(sink0), jnp.exp2(sink1))
                else:
                    if has_sink:
                        m0 = jnp.broadcast_to(sink0, (tq, 1))
                        m1 = jnp.broadcast_to(sink1, (tq, 1))
                        l = jnp.ones((tq, LANES), F32)
                    else:
                        m0 = jnp.full((tq, 1), NEG, F32)
                        m1 = jnp.full((tq, 1), NEG, F32)
                        l = zeros
                    carry = (m0, m1, l, zeros)
                    for segs in chunks:
                        carry = step_fn(2 * gg, q2, segs, carry)
                    _, _, l, acc = carry
                col = (2 * gg + p) * LANES
                o_ref[:, col:col + LANES] = (acc / l).astype(o_ref.dtype)

    fast_ok = bound_scr[0] <= FAST_SOFTMAX_BOUND
    pl.when(fast_ok)(lambda: attend(True))
    pl.when(jnp.logical_not(fast_ok))(lambda: attend(False))


def _gqa_call(z, *, b, t, tq, qname, kname, vname, qn, kn, sink, rope, cache, layer, banded,
              kv_out, gps, name):
    nq = t // tq
    n_ctx = PAST_LEN if cache is not None else 0
    rows = t + n_ctx + (2 * BLOCK if banded else 0)
    ck = min(ATTN_CHUNK, t + n_ctx)
    qblk, kblk, vblk = _COL[qname] // 256, _COL[kname] // LANES, _COL[vname] // LANES
    const = lambda bb, g, i: (0, 0)
    in_specs = [pl.BlockSpec((tq, 256), (lambda gg: lambda bb, g, i: (bb * nq + i, qblk + g * gps + gg))(gg))
                for gg in range(gps)]
    in_specs += [pl.BlockSpec((t, LANES), lambda bb, g, i: (bb, kblk)),
                 pl.BlockSpec((t, LANES), lambda bb, g, i: (bb, vblk)),
                 pl.BlockSpec((1, LANES), const),
                 pl.BlockSpec((1, LANES), const)]
    args = [z] * gps + [z, z, jnp.tile(qn, 2).reshape(1, LANES), jnp.tile(kn, 2).reshape(1, LANES)]
    if rope is not None:
        cos, sin = rope
        in_specs += [pl.BlockSpec((tq, LANES), lambda bb, g, i: (i, 0)),
                     pl.BlockSpec((tq, LANES), lambda bb, g, i: (i, 0)),
                     pl.BlockSpec((t, LANES), const),
                     pl.BlockSpec((t, LANES), const)]
        args += [cos, sin, cos, sin]
    if cache is not None:
        spec = pl.BlockSpec((1, 1, n_ctx, LANES), lambda bb, g, i: (bb, layer, 0, 0))
        in_specs += [spec, spec]
        args += [cache[0], cache[1]]
    if sink is not None:
        in_specs.append(pl.BlockSpec(memory_space=pltpu.SMEM))
        args.append(sink)
    out_shape = [jax.ShapeDtypeStruct((b * t, 512), BF16)]
    out_specs = [pl.BlockSpec((tq, 256 * gps), lambda bb, g, i: (bb * nq + i, g))]
    aliases = {}
    if kv_out is not None:
        for n_out, arr in enumerate(kv_out):
            aliases[len(args)] = 1 + n_out
            in_specs.append(pl.BlockSpec(memory_space=pl.ANY))
            args.append(arr)
            out_shape.append(jax.ShapeDtypeStruct(arr.shape, arr.dtype))
            out_specs.append(pl.BlockSpec((1, 1, t, LANES), lambda bb, g, i: (bb, layer, 0, 0)))
    res = pl.pallas_call(
        functools.partial(_gqa_kernel, t=t, tq=tq, n_ctx=n_ctx, banded=banded,
                          has_sink=sink is not None, has_rope=rope is not None,
                          emit_k=kv_out is not None, ck=ck, gps=gps),
        grid=(b, WIN_KV // gps, nq),
        in_specs=in_specs,
        out_specs=out_specs,
        out_shape=out_shape,
        input_output_aliases=aliases,
        scratch_shapes=[pltpu.VMEM((2 * gps, rows, LANES), BF16), pltpu.VMEM((2 * gps, rows, LANES), BF16),
                        pltpu.SMEM((1,), F32)],
        compiler_params=_cparams(),
        name=name,
    )(*args)
    return res[0], tuple(res[1:])


def _diff_kernel(*refs, t, tq, n_ctx, has_rope, emit_k, ck, lam_init, hps):
    it = iter(refs)
    q_ref, k_ref, v_ref, qn_ref, kn_ref, lp_ref, og_ref = (next(it) for _ in range(7))
    if has_rope:
        cq_ref, sq_ref, cka_ref, ska_ref = (next(it) for _ in range(4))
    if n_ctx:
        kc_ref, vc_ref = next(it), next(it)
    if emit_k:
        next(it), next(it)
    o_ref = next(it)
    if emit_k:
        ko_ref, vo_ref = next(it), next(it)
    k_scr, v_scr, bound_scr = next(it), next(it), next(it)

    qi = pl.program_id(2)
    lo = _lane_lo()
    heads = [(hh, slice(hh * LANES, (hh + 1) * LANES)) for hh in range(hps)]

    def put(hh, dst, kx, vx):
        zero = jnp.zeros_like(kx)
        k_scr[2 * hh, dst, :] = jnp.where(lo, kx, zero).astype(BF16)
        k_scr[2 * hh + 1, dst, :] = jnp.where(lo, zero, kx).astype(BF16)
        v_scr[hh, dst, :] = vx.astype(BF16)

    @pl.when(qi == 0)
    def _build():
        step = min(t, 512)
        for hh, cols in heads:
            for r in range(0, t, step):
                kx = _head_rmsnorm(k_ref[r:r + step, cols], kn_ref[...])
                if emit_k:
                    ko_ref[0, 0, r:r + step, cols] = kx
                    vo_ref[0, 0, r:r + step, cols] = v_ref[r:r + step, cols]
                if has_rope:
                    kx = _rope(kx, cka_ref[r:r + step, :], ska_ref[r:r + step, :])
                put(hh, slice(r, r + step), kx, v_ref[r:r + step, cols])
            if n_ctx:
                put(hh, slice(t, t + n_ctx), kc_ref[0, 0, :, cols], vc_ref[0, 0, :, cols])
        bound_scr[0] = _score_bound(qn_ref, kn_ref, kc_ref[0, 0] if n_ctx else None)

    def query(cols):
        qp = _head_rmsnorm(q_ref[:, cols], qn_ref[...])
        if has_rope:
            qp = _rope(qp, cq_ref[...], sq_ref[...])
        return (qp * (ATTN_SCALE * LOG2E)).astype(BF16)

    def step_fn(hh, q2, start, carry):
        m0, m1, l0, l1, a0, a1 = carry
        v = v_scr[hh, pl.ds(start, ck), :]
        s0 = _dot_nt(q2, k_scr[2 * hh, pl.ds(start, ck), :])
        s1 = _dot_nt(q2, k_scr[2 * hh + 1, pl.ds(start, ck), :])
        n0 = jnp.maximum(m0, jnp.max(s0, axis=-1, keepdims=True))
        n1 = jnp.maximum(m1, jnp.max(s1, axis=-1, keepdims=True))
        p0 = jnp.exp2(s0 - n0)
        p1 = jnp.exp2(s1 - n1)
        e0 = jnp.exp2(m0 - n0)
        e1 = jnp.exp2(m1 - n1)
        l0 = e0 * l0 + jnp.sum(p0, axis=-1, keepdims=True)
        l1 = e1 * l1 + jnp.sum(p1, axis=-1, keepdims=True)
        a0 = e0 * a0 + _dot(p0.astype(BF16), v)
        a1 = e1 * a1 + _dot(p1.astype(BF16), v)
        return n0, n1, l0, l1, a0, a1

    def fast_step(hh, q2, start, carry):
        ls0, ls1, a0, a1 = carry
        v = v_scr[hh, pl.ds(start, ck), :]
        p0 = jnp.exp2(_dot_nt(q2, k_scr[2 * hh, pl.ds(start, ck), :]))
        p1 = jnp.exp2(_dot_nt(q2, k_scr[2 * hh + 1, pl.ds(start, ck), :]))
        return (ls0 + _lane_tile_sum(p0), ls1 + _lane_tile_sum(p1),
                a0 + _dot(p0.astype(BF16), v), a1 + _dot(p1.astype(BF16), v))

    def finish(cols, l0, l1, a0, a1):
        lp = lp_ref[...]
        lam = (jnp.exp(jnp.sum(lp[0:1] * lp[1:2], axis=-1, keepdims=True))
               - jnp.exp(jnp.sum(lp[2:3] * lp[3:4], axis=-1, keepdims=True)) + lam_init)
        o = a0 / l0 - lam * (a1 / l1)
        var = jnp.mean(o * o, axis=-1, keepdims=True)
        o = o * lax.rsqrt(var + RMS_EPS) * og_ref[...] * (1.0 - lam_init)
        o_ref[:, cols] = o.astype(o_ref.dtype)

    zeros = jnp.zeros((tq, LANES), F32)
    starts = [c * ck for c in range((t + n_ctx) // ck)]

    def attend_fast():
        for hh, cols in heads:
            q2 = query(cols)
            carry = (zeros, zeros, zeros, zeros)
            for start in starts:
                carry = fast_step(hh, q2, start, carry)
            ls0, ls1, a0, a1 = carry
            finish(cols, jnp.sum(ls0, axis=-1, keepdims=True), jnp.sum(ls1, axis=-1, keepdims=True),
                   a0, a1)

    def attend_online():
        col = lambda val: jnp.full((tq, 1), val, F32)
        for hh, cols in heads:
            q2 = query(cols)
            carry = (col(NEG), col(NEG), col(0.0), col(0.0), zeros, zeros)
            for start in starts:
                carry = step_fn(hh, q2, start, carry)
            finish(cols, *carry[2:])

    fast_ok = bound_scr[0] <= FAST_SOFTMAX_BOUND
    pl.when(fast_ok)(attend_fast)
    pl.when(jnp.logical_not(fast_ok))(attend_online)


def _diff_call(z, *, b, t, tq, qn, kn, lparams, out_g, rope, cache, layer, kv_out, hps, name):
    nq = t // tq
    n_ctx = PAST_LEN if cache is not None else 0
    ck = min(ATTN_CHUNK, t + n_ctx)
    width = LANES * hps
    qblk, kblk, vblk = _COL['qd'] // width, _COL['kd'] // width, _COL['vd'] // width
    const = lambda bb, h, i: (0, 0)
    in_specs = [pl.BlockSpec((tq, width), lambda bb, h, i: (bb * nq + i, qblk + h)),
                pl.BlockSpec((t, width), lambda bb, h, i: (bb, kblk + h)),
                pl.BlockSpec((t, width), lambda bb, h, i: (bb, vblk + h)),
                pl.BlockSpec((1, LANES), const),
                pl.BlockSpec((1, LANES), const),
                pl.BlockSpec((4, HEAD_DIM), const),
                pl.BlockSpec((1, LANES), const)]
    args = [z, z, z, jnp.tile(qn, 2).reshape(1, LANES), jnp.tile(kn, 2).reshape(1, LANES),
            lparams, out_g.reshape(1, LANES)]
    if rope is not None:
        cos, sin = rope
        in_specs += [pl.BlockSpec((tq, LANES), lambda bb, h, i: (i, 0)),
                     pl.BlockSpec((tq, LANES), lambda bb, h, i: (i, 0)),
                     pl.BlockSpec((t, LANES), const),
                     pl.BlockSpec((t, LANES), const)]
        args += [cos, sin, cos, sin]
    if cache is not None:
        spec = pl.BlockSpec((1, 1, n_ctx, width), lambda bb, h, i: (bb, layer, 0, h))
        in_specs += [spec, spec]
        args += [cache[0], cache[1]]
    out_shape = [jax.ShapeDtypeStruct((b * t, 512), BF16)]
    out_specs = [pl.BlockSpec((tq, width), lambda bb, h, i: (bb * nq + i, h))]
    aliases = {}
    if kv_out is not None:
        for n_out, arr in enumerate(kv_out):
            aliases[len(args)] = 1 + n_out
            in_specs.append(pl.BlockSpec(memory_space=pl.ANY))
            args.append(arr)
            out_shape.append(jax.ShapeDtypeStruct(arr.shape, arr.dtype))
            out_specs.append(pl.BlockSpec((1, 1, t, width), lambda bb, h, i: (bb, layer, 0, h)))
    lam_init = 0.8 - 0.6 * math.exp(-0.3 * layer)
    res = pl.pallas_call(
        functools.partial(_diff_kernel, t=t, tq=tq, n_ctx=n_ctx, has_rope=rope is not None,
                          emit_k=kv_out is not None, ck=ck, lam_init=lam_init, hps=hps),
        grid=(b, DIFF_HEADS // hps, nq),
        in_specs=in_specs,
        out_specs=out_specs,
        out_shape=out_shape,
        input_output_aliases=aliases,
        scratch_shapes=[pltpu.VMEM((2 * hps, t + n_ctx, LANES), BF16),
                        pltpu.VMEM((hps, t + n_ctx, LANES), BF16), pltpu.SMEM((1,), F32)],
        compiler_params=_cparams(),
        name=name,
    )(*args)
    return res[0], tuple(res[1:])


def _softplus(x):
    return jnp.maximum(x, 0.0) + jnp.log1p(jnp.exp(-jnp.abs(x)))


def _lru_kernel(*refs, tc, nc, has_h0, emit_final):
    it = iter(refs)
    xp_ref, xm_ref, xn_ref, ya_ref, cw_ref, cb_ref, wl_ref, bl_ref, lam_ref = (next(it) for _ in range(9))
    if has_h0:
        h0_ref = next(it)
    if emit_final:
        next(it)
    o_ref = next(it)
    if emit_final:
        fin_ref = next(it)
    hf_scr, a_scr, b_scr, carry_scr = (next(it) for _ in range(4))

    s = pl.program_id(1)
    fwd = s < nc
    c = jnp.where(fwd, s, 2 * nc - 1 - s)
    t0 = pl.multiple_of(c * tc, tc)

    @pl.when(s == 0)
    def _():
        if has_h0:
            carry_scr[...] = h0_ref[0, 0]
        else:
            carry_scr[...] = jnp.zeros_like(carry_scr)

    prev = jnp.where(c > 0, xp_ref[...], 0.0)
    nxt = jnp.where(c < nc - 1, xn_ref[...], 0.0)
    ext = jnp.concatenate([prev, xm_ref[...], nxt], axis=0)
    n_ext = tc + 2 * SUBLANES
    u = cb_ref[...] + cw_ref[1:2] * xm_ref[...]
    u = u + cw_ref[0:1] * pltpu.roll(ext, 1, 0)[SUBLANES:SUBLANES + tc]
    u = u + cw_ref[2:3] * pltpu.roll(ext, n_ext - 1, 0)[SUBLANES:SUBLANES + tc]
    u = u + cw_ref[3:4] * pltpu.roll(ext, n_ext - 2, 0)[SUBLANES:SUBLANES + tc]

    gates = _dot(u.astype(BF16), wl_ref[0]) + bl_ref[0]
    r = jax.nn.sigmoid(gates[:, :LRU_W])
    ig = jax.nn.sigmoid(gates[:, LRU_W:])
    log_a = -LRU_C * r * _softplus(-lam_ref[0])
    a = jnp.exp(log_a)
    a_scr[...] = a
    b_scr[...] = jnp.sqrt(-jnp.tanh(log_a) * (a * a + 1.0)) * ig * u

    ntile = tc // SUBLANES
    row = lax.broadcasted_iota(jnp.int32, (SUBLANES, LRU_W), 0)

    def scan(forward):
        def tile(i, carry):
            j = i if forward else ntile - 1 - i
            r0 = pl.multiple_of(j * SUBLANES, SUBLANES)
            a = a_scr[pl.ds(r0, SUBLANES), :]
            bv = b_scr[pl.ds(r0, SUBLANES), :]
            for d in (1, 2, 4):
                shift = d if forward else SUBLANES - d
                msk = (row >= d) if forward else (row < SUBLANES - d)
                ap = pltpu.roll(a, shift, 0)
                bp = pltpu.roll(bv, shift, 0)
                bv = jnp.where(msk, a * bp + bv, bv)
                a = jnp.where(msk, a * ap, a)
            h = a * carry + bv
            g0 = pl.multiple_of(t0 + r0, SUBLANES)
            if forward:
                hf_scr[pl.ds(g0, SUBLANES), :] = h
                return h[SUBLANES - 1:SUBLANES, :]
            b_scr[pl.ds(r0, SUBLANES), :] = h
            return h[0:1, :]

        idx = 0 if forward else 1
        last = lax.fori_loop(0, ntile, tile, carry_scr[idx:idx + 1, :])
        carry_scr[idx:idx + 1, :] = last
        if emit_final:
            fin_ref[0, 0, idx:idx + 1, :] = last
        if not forward:
            o_ref[...] = ((hf_scr[pl.ds(t0, tc), :] + b_scr[...])
                          * jax.nn.gelu(ya_ref[...])).astype(o_ref.dtype)

    pl.when(fwd)(lambda: scan(True))
    pl.when(jnp.logical_not(fwd))(lambda: scan(False))


def _lru_call(z, *, b, t, conv_w, conv_b, w_lru, b_lru, lam, h0, state_out, layer, name):
    tc = min(t, 512)
    nc = t // tc
    per8 = tc // SUBLANES
    nrow8 = b * t // SUBLANES

    def chunk(s):
        return jnp.where(s < nc, s, 2 * nc - 1 - s)

    def hold(s):
        return jnp.where(s < nc, nc - 1, 2 * nc - 1 - s)

    in_specs = [
        pl.BlockSpec((SUBLANES, LRU_W),
                     lambda bb, s: (jnp.maximum((bb * nc + chunk(s)) * per8 - 1, 0), 0)),
        pl.BlockSpec((tc, LRU_W), lambda bb, s: (bb * nc + chunk(s), 0)),
        pl.BlockSpec((SUBLANES, LRU_W),
                     lambda bb, s: (jnp.minimum((bb * nc + chunk(s) + 1) * per8, nrow8 - 1), 0)),
        pl.BlockSpec((tc, LRU_W), lambda bb, s: (bb * nc + hold(s), 1)),
        pl.BlockSpec((CONV_W, LRU_W), lambda bb, s: (0, 0)),
        pl.BlockSpec((1, LRU_W), lambda bb, s: (0, 0)),
        pl.BlockSpec((1, LRU_W, 2 * LRU_W), lambda bb, s: (s // nc, 0, 0)),
        pl.BlockSpec((1, 1, 2 * LRU_W), lambda bb, s: (s // nc, 0, 0)),
        pl.BlockSpec((1, 1, LRU_W), lambda bb, s: (s // nc, 0, 0)),
    ]
    args = [z, z, z, z, conv_w, conv_b.reshape(1, LRU_W), w_lru, b_lru, lam.reshape(2, 1, LRU_W)]
    if h0 is not None:
        in_specs.append(pl.BlockSpec((1, 1, 2, LRU_W), lambda bb, s: (bb, layer, 0, 0)))
        args.append(h0)
    out_shape = [jax.ShapeDtypeStruct((b * t, LRU_W), BF16)]
    out_specs = [pl.BlockSpec((tc, LRU_W), lambda bb, s: (bb * nc + hold(s), 0))]
    aliases = {}
    if state_out is not None:
        aliases[len(args)] = 1
        in_specs.append(pl.BlockSpec(memory_space=pl.ANY))
        args.append(state_out)
        out_shape.append(jax.ShapeDtypeStruct(state_out.shape, state_out.dtype))
        out_specs.append(pl.BlockSpec((1, 1, 2, LRU_W), lambda bb, s: (bb, layer, 0, 0)))
    res = pl.pallas_call(
        functools.partial(_lru_kernel, tc=tc, nc=nc, has_h0=h0 is not None,
                          emit_final=state_out is not None),
        grid=(b, 2 * nc),
        in_specs=in_specs,
        out_specs=out_specs,
        out_shape=out_shape,
        input_output_aliases=aliases,
        scratch_shapes=[pltpu.VMEM((t, LRU_W), F32), pltpu.VMEM((tc, LRU_W), F32),
                        pltpu.VMEM((tc, LRU_W), F32), pltpu.VMEM((2, LRU_W), F32)],
        compiler_params=_cparams(),
        name=name,
    )(*args)
    return res[0], (res[1] if state_out is not None else None)


def _merge_kernel(x_ref, gt_ref, a_ref, w_ref, g_ref, d_ref, wb_ref, wo_ref, mod_ref, o_ref):
    merged = None
    for bi, br_ref in enumerate((a_ref, w_ref, g_ref, d_ref)):
        gate = gt_ref[:, bi * D_MODEL:(bi + 1) * D_MODEL].astype(F32)
        term = gate * _dot(br_ref[...], wb_ref[0, bi])
        merged = term if merged is None else merged + term
    o_ref[...] = x_ref[...] + mod_ref[0][2:3] * _dot(merged.astype(BF16), wo_ref[0])


def _merge_call(x, gates, branches, w_branch, w_o, layer, mod, latent, name):
    m = x.shape[0]
    tm = 256
    row = lambda i: (i, 0)
    resident = pl.Buffered(1)
    in_specs = [pl.BlockSpec((tm, D_MODEL), row),
                pl.BlockSpec((tm, N_BRANCH * D_MODEL), row)]
    in_specs += [pl.BlockSpec((tm, BRANCH_W), row)] * N_BRANCH
    in_specs += [pl.BlockSpec((1, N_BRANCH, BRANCH_W, D_MODEL), lambda i: (layer, 0, 0, 0),
                              pipeline_mode=resident),
                 pl.BlockSpec((1, D_MODEL, D_MODEL), lambda i: (layer, 0, 0), pipeline_mode=resident),
                 pl.BlockSpec((1, N_MOD, D_MODEL), _mod_index(latent, tm))]
    return pl.pallas_call(
        _merge_kernel,
        grid=(m // tm,),
        in_specs=in_specs,
        out_specs=pl.BlockSpec((tm, D_MODEL), row),
        out_shape=jax.ShapeDtypeStruct((m, D_MODEL), F32),
        compiler_params=_cparams(),
        name=name,
    )(x, gates, *branches, w_branch, w_o, mod)


def _ffn_kernel(x_ref, mod_ref, g_ref, w1_ref, b1_ref, w2_ref, b2_ref, o_ref, h_scr, acc_scr):
    j = pl.program_id(1)

    def chunk(h):
        a = jnp.maximum(_dot(h, w1_ref[0]) + b1_ref[...], 0.0)
        return _dot((a * a).astype(BF16), w2_ref[0])

    @pl.when(j == 0)
    def _():
        for r in range(0, x_ref.shape[0], NORM_PIECE):
            rows = slice(r, r + NORM_PIECE)
            h = _norm_mod(x_ref[rows, :], g_ref[...], mod_ref[0], 3, 4).astype(BF16)
            h_scr[rows, :] = h
            acc_scr[rows, :] = chunk(h)

    @pl.when(j > 0)
    def _():
        acc_scr[...] += chunk(h_scr[...])

    @pl.when(j == pl.num_programs(1) - 1)
    def _():
        o_ref[...] = x_ref[...] + mod_ref[0][5:6] * (acc_scr[...] + b2_ref[...])


def _ffn_call(x, mod, g, w1, b1, w2, b2, layer, latent, name):
    m = x.shape[0]
    tm, tf = 512, 1024
    return pl.pallas_call(
        _ffn_kernel,
        grid=(m // tm, D_FF // tf),
        in_specs=[pl.BlockSpec((tm, D_MODEL), lambda i, j: (i, 0)),
                  pl.BlockSpec((1, N_MOD, D_MODEL), _mod_index(latent, tm)),
                  pl.BlockSpec((1, D_MODEL), lambda i, j: (0, 0)),
                  pl.BlockSpec((1, D_MODEL, tf), lambda i, j: (layer, 0, j)),
                  pl.BlockSpec((1, tf), lambda i, j: (0, j)),
                  pl.BlockSpec((1, tf, D_MODEL), lambda i, j: (layer, j, 0)),
                  pl.BlockSpec((1, D_MODEL), lambda i, j: (0, 0))],
        out_specs=pl.BlockSpec((tm, D_MODEL), lambda i, j: (i, 0)),
        out_shape=jax.ShapeDtypeStruct((m, D_MODEL), F32),
        scratch_shapes=[pltpu.VMEM((tm, D_MODEL), BF16), pltpu.VMEM((tm, D_MODEL), F32)],
        compiler_params=_cparams(),
        name=name,
    )(x, mod, g.reshape(1, D_MODEL), w1, b1.reshape(1, D_FF), w2, b2.reshape(1, D_MODEL))


def _block_diag(w):
    eye = jnp.eye(LRU_BLOCKS, dtype=w.dtype)
    return jnp.einsum('ncd,nm->ncmd', w, eye).reshape(LRU_W, LRU_W)


def _trunk_layer(x, mod, wts, layer, latent, rope, cached, new):
    b, t = (DEC_BATCH, DEC_SEQ) if latent else (BATCH, SEQ)
    tag = ('lat' if latent else 'ctx') + str(layer)
    gates, h = _gates_call(x, mod, wts['norm1_g'], wts['w_gate'], layer, wts['b_gate'], latent,
                           'gates_' + tag)
    z = _inproj_call(h, wts['w_in'], layer, 'inproj_' + tag)
    a_out, new_state = _lru_call(
        z, b=b, t=t, conv_w=wts['conv_w'], conv_b=wts['conv_b'], w_lru=wts['w_lru'], b_lru=wts['b_lru'],
        lam=wts['lru_lambda'], h0=cached[6] if latent else None, state_out=None if latent else new[6],
        layer=layer, name='lru_' + tag)
    w_out, new_win = _gqa_call(
        z, b=b, t=t, tq=256, qname='qw', kname='kw', vname='vw',
        qn=wts['win_qn'], kn=wts['win_kn'], sink=wts['win_sink'], rope=rope,
        cache=(cached[0], cached[1]) if latent else None, layer=layer, banded=latent,
        kv_out=None if latent else new[0:2], gps=1 if latent else WIN_KV, name='win_' + tag)
    g_out, new_grid = _gqa_call(
        z, b=b, t=t, tq=256, qname='qg', kname='kg', vname='vg',
        qn=wts['grid_qn'], kn=wts['grid_kn'], sink=None, rope=rope,
        cache=(cached[2], cached[3]) if latent else None, layer=layer, banded=False,
        kv_out=None if latent else new[2:4], gps=1 if latent else WIN_KV, name='grid_' + tag)
    d_out, new_diff = _diff_call(
        z, b=b, t=t, tq=512 if latent else SEQ, qn=wts['diff_qn'], kn=wts['diff_kn'],
        lparams=wts['diff_lp'],
        out_g=wts['diff_out_g'], rope=rope, cache=(cached[4], cached[5]) if latent else None,
        layer=layer, kv_out=None if latent else new[4:6], hps=1 if latent else DIFF_HEADS,
        name='diff_' + tag)
    x = _merge_call(x, gates, (a_out, w_out, g_out, d_out), wts['w_branch'], wts['w_o'], layer, mod,
                    latent, 'merge_' + tag)
    x = _ffn_call(x, mod, wts['norm2_g'], wts['w_ff1'], wts['b_ff1'], wts['w_ff2'], wts['b_ff2'],
                  layer, latent, 'ffn_' + tag)
    if latent:
        return x, None
    return x, new_win + new_grid + new_diff + (new_state,)


def kernel(x_prompt, x_sample, cache_win_k, cache_win_v, cache_grid_k, cache_grid_v, cache_diff_k, cache_diff_v, state_lru, c, c_ctx, w_ada, b_ada, norm1_g, norm2_g, w_in, conv_w, conv_b, lru_wr, lru_br, lru_wi, lru_bi, lru_lambda, win_qn, win_kn, win_sink, grid_qn, grid_kn, diff_qn, diff_kn, diff_lq1, diff_lk1, diff_lq2, diff_lk2, diff_out_g, w_branch, w_gate, b_gate, w_o, w_ff1, b_ff1, w_ff2, b_ff2):
    cond = jnp.zeros((MOD_ROWS, D_MODEL), F32).at[0].set(c_ctx).at[1:1 + DEC_BATCH].set(c)
    mod_all = _modulation(cond, w_ada, b_ada).reshape(DEPTH, MOD_ROWS, N_MOD, D_MODEL)
    rope = _rope_tables(DEC_SEQ)
    cached_all = (cache_win_k.reshape(DEC_BATCH, DEPTH, PAST_LEN, LANES),
                  cache_win_v.reshape(DEC_BATCH, DEPTH, PAST_LEN, LANES),
                  cache_grid_k.reshape(DEC_BATCH, DEPTH, PAST_LEN, LANES),
                  cache_grid_v.reshape(DEC_BATCH, DEPTH, PAST_LEN, LANES),
                  cache_diff_k.reshape(DEC_BATCH, DEPTH, PAST_LEN, 512),
                  cache_diff_v.reshape(DEC_BATCH, DEPTH, PAST_LEN, 512))
    w_in_b, w_gate_b, w_branch_b, w_o_b, w_ff1_b, w_ff2_b = (
        w.astype(BF16) for w in (w_in, w_gate, w_branch, w_o, w_ff1, w_ff2))
    y_p = x_prompt.reshape(BATCH * SEQ, D_MODEL)
    y_s = x_sample.reshape(DEC_BATCH * DEC_SEQ, D_MODEL)
    new = tuple(jnp.zeros((BATCH, DEPTH, SEQ, w), F32) for w in (LANES,) * 4 + (512,) * 2)
    new += (jnp.zeros((BATCH, DEPTH, 2, LRU_W), F32),)
    for l in range(DEPTH):
        w_lru = jnp.stack([
            jnp.concatenate([_block_diag(lru_wr[l, k]), _block_diag(lru_wi[l, k])], axis=1)
            for k in range(2)]).astype(BF16)
        b_lru = jnp.concatenate([lru_br[l], lru_bi[l]], axis=-1).reshape(2, 1, 2 * LRU_W)
        wts = {
            'norm1_g': norm1_g[l], 'norm2_g': norm2_g[l],
            'w_in': w_in_b, 'w_gate': w_gate_b, 'b_gate': b_gate[l],
            'conv_w': conv_w[l], 'conv_b': conv_b[l], 'w_lru': w_lru, 'b_lru': b_lru,
            'lru_lambda': lru_lambda[l],
            'win_qn': win_qn[l], 'win_kn': win_kn[l], 'win_sink': win_sink[l],
            'grid_qn': grid_qn[l], 'grid_kn': grid_kn[l],
            'diff_qn': diff_qn[l], 'diff_kn': diff_kn[l],
            'diff_lp': jnp.stack([diff_lq1[l], diff_lk1[l], diff_lq2[l], diff_lk2[l]]),
            'diff_out_g': diff_out_g[l],
            'w_branch': w_branch_b, 'w_o': w_o_b,
            'w_ff1': w_ff1_b, 'b_ff1': b_ff1[l], 'w_ff2': w_ff2_b, 'b_ff2': b_ff2[l],
        }
        y_p, new = _trunk_layer(y_p, mod_all[l], wts, l, False, None, None, new)
        y_s, _ = _trunk_layer(y_s, mod_all[l], wts, l, True, rope, cached_all + (state_lru,), None)

    kv_shape = (BATCH, DEPTH, SEQ, WIN_KV, HEAD_DIM)
    return (y_p.reshape(BATCH, SEQ, D_MODEL),
            y_s.reshape(DEC_BATCH, DEC_SEQ, D_MODEL),
            new[0].reshape(kv_shape), new[1].reshape(kv_shape),
            new[2].reshape(kv_shape), new[3].reshape(kv_shape),
            new[4].reshape(BATCH, DEPTH, SEQ, DIFF_HEADS, 2, HEAD_DIM),
            new[5].reshape(BATCH, DEPTH, SEQ, DIFF_HEADS, 2 * HEAD_DIM),
            new[6])
```

```python
import functools
import math

import jax
import jax.numpy as jnp
import numpy as np
from jax import lax
from jax.experimental import pallas as pl
from jax.experimental.pallas import tpu as pltpu

F32 = jnp.float32
BF16 = jnp.bfloat16

D_MODEL = 2048
BATCH = 32
SEQ = 256
DEPTH = 4
DEC_BATCH = 4
DEC_SEQ = 4096
PAST_LEN = 512
GRID_W = 64
BLOCK = 128
HEAD_DIM = 64
N_FREQ = HEAD_DIM // 4
ROPE_BASE = 10000.0
ATTN_SCALE = HEAD_DIM ** -0.5
RMS_EPS = 1e-6
N_MOD = 6
N_BRANCH = 4
BRANCH_W = D_MODEL // 4
LRU_W = BRANCH_W
LRU_BLOCKS = 8
LRU_BW = LRU_W // LRU_BLOCKS
LRU_C = 8.0
CONV_W = 4
WIN_HEADS = 8
WIN_KV = 2
DIFF_HEADS = 4
D_FF = 4 * D_MODEL
D_IN = 4096

V7X_VMEM_BYTES = 64 * 1024 * 1024
VMEM_LIMIT = V7X_VMEM_BYTES - 8 * 1024 * 1024
LANES = 128
SUBLANES = 8
NEG = -1e30
LOG2E = math.log2(math.e)
ATTN_CHUNK = 512
FAST_SOFTMAX_BOUND = 40.0
NORM_PIECE = 256
MOD_ROWS = 8

_COL = dict(xa=0, ya=512, qw=1024, kw=1536, vw=1664, qg=1792, kg=2304, vg=2432, qd=2560, kd=3072,
            vd=3584)


def _cparams():
    return pltpu.CompilerParams(vmem_limit_bytes=VMEM_LIMIT)


def _dot(a, b):
    return jnp.dot(a, b, preferred_element_type=F32)


def _dot_nt(a, b):
    return lax.dot_general(a, b, (((1,), (1,)), ((), ())), preferred_element_type=F32)


def _mod_kernel(c_ref, w_ref, b_ref, o_ref):
    c = c_ref[...]
    s = (c * jax.nn.sigmoid(c)).astype(BF16)
    o_ref[0] = _dot(s, w_ref[0].astype(BF16)) + b_ref[0]


def _modulation(cond, w_ada, b_ada):
    tn = 1024
    n = N_MOD * D_MODEL
    return pl.pallas_call(
        _mod_kernel,
        grid=(DEPTH, n // tn),
        in_specs=[pl.BlockSpec((MOD_ROWS, D_MODEL), lambda l, j: (0, 0)),
                  pl.BlockSpec((1, D_MODEL, tn), lambda l, j: (l, 0, j)),
                  pl.BlockSpec((1, 1, tn), lambda l, j: (l, 0, j))],
        out_specs=pl.BlockSpec((1, MOD_ROWS, tn), lambda l, j: (l, 0, j)),
        out_shape=jax.ShapeDtypeStruct((DEPTH, MOD_ROWS, n), F32),
        compiler_params=_cparams(),
        name='modulation',
    )(cond, w_ada, b_ada.reshape(DEPTH, 1, n))


def _mod_index(latent, tm):
    if latent:
        per = DEC_SEQ // tm
        return lambda i, *_: (1 + i // per, 0, 0)
    return lambda i, *_: (0, 0, 0)


def _norm_mod(x, g, mod, shift_idx, scale_idx):
    var = jnp.mean(x * x, axis=-1, keepdims=True)
    y = x * lax.rsqrt(var + RMS_EPS) * g
    return y * (1.0 + mod[scale_idx:scale_idx + 1]) + mod[shift_idx:shift_idx + 1]


def _gates_kernel(x_ref, mod_ref, g_ref, w_ref, b_ref, o_ref, h_ref):
    def gate(h):
        return jax.nn.sigmoid(_dot(h, w_ref[0]) + b_ref[...]).astype(o_ref.dtype)

    @pl.when(pl.program_id(1) == 0)
    def _():
        for r in range(0, x_ref.shape[0], NORM_PIECE):
            rows = slice(r, r + NORM_PIECE)
            h = _norm_mod(x_ref[rows, :], g_ref[...], mod_ref[0], 0, 1).astype(BF16)
            h_ref[rows, :] = h
            o_ref[rows, :] = gate(h)

    @pl.when(pl.program_id(1) > 0)
    def _():
        o_ref[...] = gate(h_ref[...])


def _gates_call(x, mod, g, w, layer, bias, latent, name):
    m, n = x.shape[0], w.shape[2]
    tm, tn = 1024, 1024
    return pl.pallas_call(
        _gates_kernel,
        grid=(m // tm, n // tn),
        in_specs=[pl.BlockSpec((tm, D_MODEL), lambda i, j: (i, 0)),
                  pl.BlockSpec((1, N_MOD, D_MODEL), _mod_index(latent, tm)),
                  pl.BlockSpec((1, D_MODEL), lambda i, j: (0, 0)),
                  pl.BlockSpec((1, D_MODEL, tn), lambda i, j: (layer, 0, j)),
                  pl.BlockSpec((1, tn), lambda i, j: (0, j))],
        out_specs=[pl.BlockSpec((tm, tn), lambda i, j: (i, j)),
                   pl.BlockSpec((tm, D_MODEL), lambda i, j: (i, 0))],
        out_shape=[jax.ShapeDtypeStruct((m, n), BF16), jax.ShapeDtypeStruct((m, D_MODEL), BF16)],
        compiler_params=_cparams(),
        name=name,
    )(x, mod, g.reshape(1, D_MODEL), w, bias.reshape(1, n))


def _inproj_kernel(h_ref, w_ref, o_ref):
    o_ref[...] = _dot(h_ref[...], w_ref[0])


def _inproj_call(h, w, layer, name):
    m, n = h.shape[0], w.shape[2]
    tm, tn = 1024, 2048
    return pl.pallas_call(
        _inproj_kernel,
        grid=(m // tm, n // tn),
        in_specs=[pl.BlockSpec((tm, D_MODEL), lambda i, j: (i, 0)),
                  pl.BlockSpec((1, D_MODEL, tn), lambda i, j: (layer, 0, j))],
        out_specs=pl.BlockSpec((tm, tn), lambda i, j: (i, j)),
        out_shape=jax.ShapeDtypeStruct((m, n), F32),
        compiler_params=_cparams(),
        name=name,
    )(h, w)


def _lane_lo():
    return lax.broadcasted_iota(jnp.int32, (1, LANES), 1) < HEAD_DIM


def _seg_matrix():
    r = lax.broadcasted_iota(jnp.int32, (LANES, LANES), 0) // HEAD_DIM
    c = lax.broadcasted_iota(jnp.int32, (LANES, LANES), 1) // HEAD_DIM
    return jnp.where(r == c, 1.0, 0.0).astype(BF16)


def _head_rmsnorm(x, gain):
    x2 = x * x
    hi = x2.astype(BF16)
    lo = (x2 - hi.astype(F32)).astype(BF16)
    seg = _seg_matrix()
    ms = (_dot(hi, seg) + _dot(lo, seg)) * (1.0 / HEAD_DIM)
    return x * lax.rsqrt(ms + RMS_EPS) * gain


def _rope(x, cos, sin_signed):
    lane = lax.broadcasted_iota(jnp.int32, (1, LANES), 1)
    first = (lane & (2 * N_FREQ - 1)) < N_FREQ
    up = pltpu.roll(x, LANES - N_FREQ, 1)
    dn = pltpu.roll(x, N_FREQ, 1)
    return x * cos + jnp.where(first, up, dn) * sin_signed


def _score_bound(qn_ref, kn_ref, cached_k):
    root = math.sqrt(HEAD_DIM)
    qmax = root * jnp.max(jnp.abs(qn_ref[...]))
    kmax = root * jnp.max(jnp.abs(kn_ref[...]))
    if cached_k is not None:
        kmax = jnp.maximum(kmax, jnp.sqrt(jnp.max(jnp.sum(cached_k * cached_k, axis=-1, keepdims=True))))
    return qmax * kmax * (ATTN_SCALE * LOG2E)


def _lane_tile_sum(p):
    acc = p[:, 0:LANES]
    for j in range(1, p.shape[1] // LANES):
        acc = acc + p[:, j * LANES:(j + 1) * LANES]
    return acc


def _rope_tables(t):
    pos = jnp.arange(t)
    row = (pos // GRID_W).astype(F32)
    col = (pos % GRID_W).astype(F32)
    inv = ROPE_BASE ** (-jnp.arange(N_FREQ, dtype=F32) / N_FREQ)
    ar, ac = row[:, None] * inv, col[:, None] * inv
    cos = jnp.concatenate([jnp.cos(ar), jnp.cos(ar), jnp.cos(ac), jnp.cos(ac)], axis=-1)
    sin = jnp.concatenate([-jnp.sin(ar), jnp.sin(ar), -jnp.sin(ac), jnp.sin(ac)], axis=-1)
    return jnp.tile(cos, (1, 2)), jnp.tile(sin, (1, 2))


def _gqa_kernel(*refs, t, tq, n_ctx, banded, has_sink, has_rope, emit_k, ck, gps):
    it = iter(refs)
    q_refs = [next(it) for _ in range(gps)]
    k_ref, v_ref, qn_ref, kn_ref = (next(it) for _ in range(4))
    if has_rope:
        cq_ref, sq_ref, cka_ref, ska_ref = (next(it) for _ in range(4))
    if n_ctx:
        kc_ref, vc_ref = next(it), next(it)
    if has_sink:
        sink_ref = next(it)
    if emit_k:
        next(it), next(it)
    o_ref = next(it)
    if emit_k:
        ko_ref, vo_ref = next(it), next(it)
    k_scr, v_scr, bound_scr = next(it), next(it), next(it)

    qi = pl.program_id(2)
    lo = _lane_lo()
    lat0 = BLOCK if banded else 0
    ctx0 = t + 2 * lat0
    groups = [0, 1] if gps == 2 else [pl.program_id(1)]

    def pick(g, own, other):
        if isinstance(g, int):
            return own if g == 0 else other
        return jnp.where(g == 0, own, other)

    def put(dst, kx, vx):
        kr = pltpu.roll(kx, HEAD_DIM, 1)
        vr = pltpu.roll(vx, HEAD_DIM, 1)
        zero = jnp.zeros_like(kx)
        for gg, g in enumerate(groups):
            k_scr[2 * gg, dst, :] = jnp.where(lo, pick(g, kx, kr), zero).astype(BF16)
            k_scr[2 * gg + 1, dst, :] = jnp.where(lo, zero, pick(g, kr, kx)).astype(BF16)
            v_scr[2 * gg, dst, :] = jnp.where(lo, pick(g, vx, vr), zero).astype(BF16)
            v_scr[2 * gg + 1, dst, :] = jnp.where(lo, zero, pick(g, vr, vx)).astype(BF16)

    @pl.when(qi == 0)
    def _build():
        step = min(t, 512)
        for r in range(0, t, step):
            kx = _head_rmsnorm(k_ref[r:r + step, :], kn_ref[...])
            if emit_k:
                ko_ref[0, 0, r:r + step, :] = kx
                vo_ref[0, 0, r:r + step, :] = v_ref[r:r + step, :]
            if has_rope:
                kx = _rope(kx, cka_ref[r:r + step, :], ska_ref[r:r + step, :])
            put(slice(lat0 + r, lat0 + r + step), kx, v_ref[r:r + step, :])
        if banded:
            zpad = jnp.zeros((BLOCK, LANES), BF16)
            for scr in (k_scr, v_scr):
                for var in range(2 * gps):
                    scr[var, 0:BLOCK, :] = zpad
                    scr[var, lat0 + t:lat0 + t + BLOCK, :] = zpad
        if n_ctx:
            put(slice(ctx0, ctx0 + n_ctx), kc_ref[0, 0], vc_ref[0, 0])
        bound = _score_bound(qn_ref, kn_ref, kc_ref[0, 0] if n_ctx else None)
        if has_sink:
            for h in range(WIN_HEADS):
                bound = jnp.maximum(bound, jnp.abs(sink_ref[h]) * LOG2E)
        bound_scr[0] = bound

    def step_fn(base, q2, segs, carry):
        m0, m1, l, acc = carry

        def scores(var):
            parts = []
            for start, size, mask in segs:
                s = _dot_nt(q2, k_scr[var, pl.ds(start, size), :])
                parts.append(s if mask is None else jnp.where(mask, s, NEG))
            return parts[0] if len(parts) == 1 else jnp.concatenate(parts, axis=1)

        s0, s1 = scores(base), scores(base + 1)
        n0 = jnp.maximum(m0, jnp.max(s0, axis=-1, keepdims=True))
        n1 = jnp.maximum(m1, jnp.max(s1, axis=-1, keepdims=True))
        p0 = jnp.exp2(s0 - n0)
        p1 = jnp.exp2(s1 - n1)
        alpha = jnp.where(lo, jnp.exp2(m0 - n0), jnp.exp2(m1 - n1))
        rs = jnp.where(lo, jnp.sum(p0, axis=-1, keepdims=True), jnp.sum(p1, axis=-1, keepdims=True))
        l = alpha * l + rs
        acc = alpha * acc
        p0, p1 = p0.astype(BF16), p1.astype(BF16)
        off = 0
        for start, size, _ in segs:
            acc = (acc + _dot(p0[:, off:off + size], v_scr[base, pl.ds(start, size), :])
                   + _dot(p1[:, off:off + size], v_scr[base + 1, pl.ds(start, size), :]))
            off += size
        return n0, n1, l, acc

    def fast_step(base, q2, segs, carry):
        ls0, ls1, acc = carry
        for start, size, mask in segs:
            s0 = _dot_nt(q2, k_scr[base, pl.ds(start, size), :])
            s1 = _dot_nt(q2, k_scr[base + 1, pl.ds(start, size), :])
            if mask is not None:
                s0 = jnp.where(mask, s0, NEG)
                s1 = jnp.where(mask, s1, NEG)
            p0 = jnp.exp2(s0)
            p1 = jnp.exp2(s1)
            ls0 = ls0 + _lane_tile_sum(p0)
            ls1 = ls1 + _lane_tile_sum(p1)
            acc = (acc + _dot(p0.astype(BF16), v_scr[base, pl.ds(start, size), :])
                   + _dot(p1.astype(BF16), v_scr[base + 1, pl.ds(start, size), :]))
        return ls0, ls1, acc

    if banded:
        span = tq + 2 * BLOCK
        rr = lax.broadcasted_iota(jnp.int32, (tq, span), 0)
        cc = lax.broadcasted_iota(jnp.int32, (tq, span), 1)
        kpos = qi * tq - BLOCK + cc
        band_mask = (cc >= rr) & (cc - rr <= 2 * BLOCK) & (kpos >= 0) & (kpos < t)

    def attend(fast):
        if banded:
            chunks = [[(pl.multiple_of(qi * tq, tq), span, band_mask)]]
            if n_ctx:
                chunks[0].append((ctx0, n_ctx, None))
        else:
            chunks = [[(c * ck, ck, None)] for c in range((t + n_ctx) // ck)]
        zeros = jnp.zeros((tq, LANES), F32)
        for gg, g in enumerate(groups):
            q = q_refs[gg][...]
            for p in range(2):
                qp = _head_rmsnorm(q[:, p * LANES:(p + 1) * LANES], qn_ref[...])
                if has_rope:
                    qp = _rope(qp, cq_ref[...], sq_ref[...])
                q2 = (qp * (ATTN_SCALE * LOG2E)).astype(BF16)
                if has_sink:
                    h0 = g * 4 + 2 * p
                    sink0 = jnp.full((1, 1), sink_ref[h0] * LOG2E, F32)
                    sink1 = jnp.full((1, 1), sink_ref[h0 + 1] * LOG2E, F32)
                if fast:
                    carry = (zeros, zeros, zeros)
                    for segs in chunks:
                        carry = fast_step(2 * gg, q2, segs, carry)
                    ls0, ls1, acc = carry
                    l = jnp.where(lo, jnp.sum(ls0, axis=-1, keepdims=True),
                                  jnp.sum(ls1, axis=-1, keepdims=True))
                    if has_sink:
                        l = l + jnp.where(lo, jnp.exp2(sink0), jnp.exp2(sink1))
                else:
                    if has_sink:
                        m0 = jnp.broadcast_to(sink0, (tq, 1))
                        m1 = jnp.broadcast_to(sink1, (tq, 1))
                        l = jnp.ones((tq, LANES), F32)
                    else:
                        m0 = jnp.full((tq, 1), NEG, F32)
                        m1 = jnp.full((tq, 1), NEG, F32)
                        l = zeros
                    carry = (m0, m1, l, zeros)
                    if banded:
                        carry = step_fn(2 * gg, q2, chunks[0], carry)
                    else:
                        carry = lax.fori_loop(
                            0, len(chunks),
                            lambda c, cr: step_fn(2 * gg, q2, [(pl.multiple_of(c * ck, ck), ck, None)], cr),
                            carry)
                    _, _, l, acc = carry
                col = (2 * gg + p) * LANES
                o_ref[:, col:col + LANES] = (acc / l).astype(o_ref.dtype)

    fast_ok = bound_scr[0] <= FAST_SOFTMAX_BOUND
    pl.when(fast_ok)(lambda: attend(True))
    pl.when(jnp.logical_not(fast_ok))(lambda: attend(False))


def _gqa_call(z, *, b, t, tq, qname, kname, vname, qn, kn, sink, rope, cache, layer, banded,
              kv_out, gps, name):
    nq = t // tq
    n_ctx = PAST_LEN if cache is not None else 0
    rows = t + n_ctx + (2 * BLOCK if banded else 0)
    ck = min(ATTN_CHUNK, t + n_ctx)
    qblk, kblk, vblk = _COL[qname] // 256, _COL[kname] // LANES, _COL[vname] // LANES
    const = lambda bb, g, i: (0, 0)
    in_specs = [pl.BlockSpec((tq, 256), (lambda gg: lambda bb, g, i: (bb * nq + i, qblk + g * gps + gg))(gg))
                for gg in range(gps)]
    in_specs += [pl.BlockSpec((t, LANES), lambda bb, g, i: (bb, kblk)),
                 pl.BlockSpec((t, LANES), lambda bb, g, i: (bb, vblk)),
                 pl.BlockSpec((1, LANES), const),
                 pl.BlockSpec((1, LANES), const)]
    args = [z] * gps + [z, z, jnp.tile(qn, 2).reshape(1, LANES), jnp.tile(kn, 2).reshape(1, LANES)]
    if rope is not None:
        cos, sin = rope
        in_specs += [pl.BlockSpec((tq, LANES), lambda bb, g, i: (i, 0)),
                     pl.BlockSpec((tq, LANES), lambda bb, g, i: (i, 0)),
                     pl.BlockSpec((t, LANES), const),
                     pl.BlockSpec((t, LANES), const)]
        args += [cos, sin, cos, sin]
    if cache is not None:
        spec = pl.BlockSpec((1, 1, n_ctx, LANES), lambda bb, g, i: (bb, layer, 0, 0))
        in_specs += [spec, spec]
        args += [cache[0], cache[1]]
    if sink is not None:
        in_specs.append(pl.BlockSpec(memory_space=pltpu.SMEM))
        args.append(sink)
    out_shape = [jax.ShapeDtypeStruct((b * t, 512), BF16)]
    out_specs = [pl.BlockSpec((tq, 256 * gps), lambda bb, g, i: (bb * nq + i, g))]
    aliases = {}
    if kv_out is not None:
        for n_out, arr in enumerate(kv_out):
            aliases[len(args)] = 1 + n_out
            in_specs.append(pl.BlockSpec(memory_space=pl.ANY))
            args.append(arr)
            out_shape.append(jax.ShapeDtypeStruct(arr.shape, arr.dtype))
            out_specs.append(pl.BlockSpec((1, 1, t, LANES), lambda bb, g, i: (bb, layer, 0, 0)))
    res = pl.pallas_call(
        functools.partial(_gqa_kernel, t=t, tq=tq, n_ctx=n_ctx, banded=banded,
                          has_sink=sink is not None, has_rope=rope is not None,
                          emit_k=kv_out is not None, ck=ck, gps=gps),
        grid=(b, WIN_KV // gps, nq),
        in_specs=in_specs,
        out_specs=out_specs,
        out_shape=out_shape,
        input_output_aliases=aliases,
        scratch_shapes=[pltpu.VMEM((2 * gps, rows, LANES), BF16), pltpu.VMEM((2 * gps, rows, LANES), BF16),
                        pltpu.SMEM((1,), F32)],
        compiler_params=_cparams(),
        name=name,
    )(*args)
    return res[0], tuple(res[1:])


def _diff_kernel(*refs, t, tq, n_ctx, has_rope, emit_k, ck, lam_init, hps):
    it = iter(refs)
    q_ref, k_ref, v_ref, qn_ref, kn_ref, lp_ref, og_ref = (next(it) for _ in range(7))
    if has_rope:
        cq_ref, sq_ref, cka_ref, ska_ref = (next(it) for _ in range(4))
    if n_ctx:
        kc_ref, vc_ref = next(it), next(it)
    if emit_k:
        next(it), next(it)
    o_ref = next(it)
    if emit_k:
        ko_ref, vo_ref = next(it), next(it)
    k_scr, v_scr, bound_scr = next(it), next(it), next(it)

    qi = pl.program_id(2)
    lo = _lane_lo()
    heads = [(hh, slice(hh * LANES, (hh + 1) * LANES)) for hh in range(hps)]

    def put(hh, dst, kx, vx):
        zero = jnp.zeros_like(kx)
        k_scr[2 * hh, dst, :] = jnp.where(lo, kx, zero).astype(BF16)
        k_scr[2 * hh + 1, dst, :] = jnp.where(lo, zero, kx).astype(BF16)
        v_scr[hh, dst, :] = vx.astype(BF16)

    @pl.when(qi == 0)
    def _build():
        step = min(t, 512)
        for hh, cols in heads:
            for r in range(0, t, step):
                kx = _head_rmsnorm(k_ref[r:r + step, cols], kn_ref[...])
                if emit_k:
                    ko_ref[0, 0, r:r + step, cols] = kx
                    vo_ref[0, 0, r:r + step, cols] = v_ref[r:r + step, cols]
                if has_rope:
                    kx = _rope(kx, cka_ref[r:r + step, :], ska_ref[r:r + step, :])
                put(hh, slice(r, r + step), kx, v_ref[r:r + step, cols])
            if n_ctx:
                put(hh, slice(t, t + n_ctx), kc_ref[0, 0, :, cols], vc_ref[0, 0, :, cols])
        bound_scr[0] = _score_bound(qn_ref, kn_ref, kc_ref[0, 0] if n_ctx else None)

    def query(cols):
        qp = _head_rmsnorm(q_ref[:, cols], qn_ref[...])
        if has_rope:
            qp = _rope(qp, cq_ref[...], sq_ref[...])
        return (qp * (ATTN_SCALE * LOG2E)).astype(BF16)

    def step_fn(hh, q2, start, carry):
        m0, m1, l0, l1, a0, a1 = carry
        v = v_scr[hh, pl.ds(start, ck), :]
        s0 = _dot_nt(q2, k_scr[2 * hh, pl.ds(start, ck), :])
        s1 = _dot_nt(q2, k_scr[2 * hh + 1, pl.ds(start, ck), :])
        n0 = jnp.maximum(m0, jnp.max(s0, axis=-1, keepdims=True))
        n1 = jnp.maximum(m1, jnp.max(s1, axis=-1, keepdims=True))
        p0 = jnp.exp2(s0 - n0)
        p1 = jnp.exp2(s1 - n1)
        e0 = jnp.exp2(m0 - n0)
        e1 = jnp.exp2(m1 - n1)
        l0 = e0 * l0 + jnp.sum(p0, axis=-1, keepdims=True)
        l1 = e1 * l1 + jnp.sum(p1, axis=-1, keepdims=True)
        a0 = e0 * a0 + _dot(p0.astype(BF16), v)
        a1 = e1 * a1 + _dot(p1.astype(BF16), v)
        return n0, n1, l0, l1, a0, a1

    def fast_step(hh, q2, start, carry):
        ls0, ls1, a0, a1 = carry
        v = v_scr[hh, pl.ds(start, ck), :]
        p0 = jnp.exp2(_dot_nt(q2, k_scr[2 * hh, pl.ds(start, ck), :]))
        p1 = jnp.exp2(_dot_nt(q2, k_scr[2 * hh + 1, pl.ds(start, ck), :]))
        return (ls0 + _lane_tile_sum(p0), ls1 + _lane_tile_sum(p1),
                a0 + _dot(p0.astype(BF16), v), a1 + _dot(p1.astype(BF16), v))

    def finish(cols, l0, l1, a0, a1):
        lp = lp_ref[...]
        lam = (jnp.exp(jnp.sum(lp[0:1] * lp[1:2], axis=-1, keepdims=True))
               - jnp.exp(jnp.sum(lp[2:3] * lp[3:4], axis=-1, keepdims=True)) + lam_init)
        o = a0 / l0 - lam * (a1 / l1)
        var = jnp.mean(o * o, axis=-1, keepdims=True)
        o = o * lax.rsqrt(var + RMS_EPS) * og_ref[...] * (1.0 - lam_init)
        o_ref[:, cols] = o.astype(o_ref.dtype)

    zeros = jnp.zeros((tq, LANES), F32)
    starts = [c * ck for c in range((t + n_ctx) // ck)]

    def attend_fast():
        for hh, cols in heads:
            q2 = query(cols)
            carry = (zeros, zeros, zeros, zeros)
            for start in starts:
                carry = fast_step(hh, q2, start, carry)
            ls0, ls1, a0, a1 = carry
            finish(cols, jnp.sum(ls0, axis=-1, keepdims=True), jnp.sum(ls1, axis=-1, keepdims=True),
                   a0, a1)

    def attend_online():
        col = lambda val: jnp.full((tq, 1), val, F32)
        for hh, cols in heads:
            q2 = query(cols)
            carry = (col(NEG), col(NEG), col(0.0), col(0.0), zeros, zeros)
            carry = lax.fori_loop(
                0, len(starts), lambda c, cr: step_fn(hh, q2, pl.multiple_of(c * ck, ck), cr), carry)
            finish(cols, *carry[2:])

    fast_ok = bound_scr[0] <= FAST_SOFTMAX_BOUND
    pl.when(fast_ok)(attend_fast)
    pl.when(jnp.logical_not(fast_ok))(attend_online)


def _diff_call(z, *, b, t, tq, qn, kn, lparams, out_g, rope, cache, layer, kv_out, hps, name):
    nq = t // tq
    n_ctx = PAST_LEN if cache is not None else 0
    ck = min(ATTN_CHUNK, t + n_ctx)
    width = LANES * hps
    qblk, kblk, vblk = _COL['qd'] // width, _COL['kd'] // width, _COL['vd'] // width
    const = lambda bb, h, i: (0, 0)
    in_specs = [pl.BlockSpec((tq, width), lambda bb, h, i: (bb * nq + i, qblk + h)),
                pl.BlockSpec((t, width), lambda bb, h, i: (bb, kblk + h)),
                pl.BlockSpec((t, width), lambda bb, h, i: (bb, vblk + h)),
                pl.BlockSpec((1, LANES), const),
                pl.BlockSpec((1, LANES), const),
                pl.BlockSpec((4, HEAD_DIM), const),
                pl.BlockSpec((1, LANES), const)]
    args = [z, z, z, jnp.tile(qn, 2).reshape(1, LANES), jnp.tile(kn, 2).reshape(1, LANES),
            lparams, out_g.reshape(1, LANES)]
    if rope is not None:
        cos, sin = rope
        in_specs += [pl.BlockSpec((tq, LANES), lambda bb, h, i: (i, 0)),
                     pl.BlockSpec((tq, LANES), lambda bb, h, i: (i, 0)),
                     pl.BlockSpec((t, LANES), const),
                     pl.BlockSpec((t, LANES), const)]
        args += [cos, sin, cos, sin]
    if cache is not None:
        spec = pl.BlockSpec((1, 1, n_ctx, width), lambda bb, h, i: (bb, layer, 0, h))
        in_specs += [spec, spec]
        args += [cache[0], cache[1]]
    out_shape = [jax.ShapeDtypeStruct((b * t, 512), BF16)]
    out_specs = [pl.BlockSpec((tq, width), lambda bb, h, i: (bb * nq + i, h))]
    aliases = {}
    if kv_out is not None:
        for n_out, arr in enumerate(kv_out):
            aliases[len(args)] = 1 + n_out
            in_specs.append(pl.BlockSpec(memory_space=pl.ANY))
            args.append(arr)
            out_shape.append(jax.ShapeDtypeStruct(arr.shape, arr.dtype))
            out_specs.append(pl.BlockSpec((1, 1, t, width), lambda bb, h, i: (bb, layer, 0, h)))
    lam_init = 0.8 - 0.6 * math.exp(-0.3 * layer)
    res = pl.pallas_call(
        functools.partial(_diff_kernel, t=t, tq=tq, n_ctx=n_ctx, has_rope=rope is not None,
                          emit_k=kv_out is not None, ck=ck, lam_init=lam_init, hps=hps),
        grid=(b, DIFF_HEADS // hps, nq),
        in_specs=in_specs,
        out_specs=out_specs,
        out_shape=out_shape,
        input_output_aliases=aliases,
        scratch_shapes=[pltpu.VMEM((2 * hps, t + n_ctx, LANES), BF16),
                        pltpu.VMEM((hps, t + n_ctx, LANES), BF16), pltpu.SMEM((1,), F32)],
        compiler_params=_cparams(),
        name=name,
    )(*args)
    return res[0], tuple(res[1:])


def _softplus(x):
    return jnp.maximum(x, 0.0) + jnp.log1p(jnp.exp(-jnp.abs(x)))


def _lru_kernel(*refs, tc, nc, has_h0, emit_final):
    it = iter(refs)
    xp_ref, xm_ref, xn_ref, ya_ref, cw_ref, cb_ref, wl_ref, bl_ref, lam_ref = (next(it) for _ in range(9))
    if has_h0:
        h0_ref = next(it)
    if emit_final:
        next(it)
    o_ref = next(it)
    if emit_final:
        fin_ref = next(it)
    hf_scr, a_scr, b_scr, carry_scr = (next(it) for _ in range(4))

    s = pl.program_id(1)
    fwd = s < nc
    c = jnp.where(fwd, s, 2 * nc - 1 - s)
    t0 = pl.multiple_of(c * tc, tc)

    @pl.when(s == 0)
    def _():
        if has_h0:
            carry_scr[...] = h0_ref[0, 0]
        else:
            carry_scr[...] = jnp.zeros_like(carry_scr)

    prev = jnp.where(c > 0, xp_ref[...], 0.0)
    nxt = jnp.where(c < nc - 1, xn_ref[...], 0.0)
    ext = jnp.concatenate([prev, xm_ref[...], nxt], axis=0)
    n_ext = tc + 2 * SUBLANES
    u = cb_ref[...] + cw_ref[1:2] * xm_ref[...]
    u = u + cw_ref[0:1] * pltpu.roll(ext, 1, 0)[SUBLANES:SUBLANES + tc]
    u = u + cw_ref[2:3] * pltpu.roll(ext, n_ext - 1, 0)[SUBLANES:SUBLANES + tc]
    u = u + cw_ref[3:4] * pltpu.roll(ext, n_ext - 2, 0)[SUBLANES:SUBLANES + tc]

    gates = _dot(u.astype(BF16), wl_ref[0]) + bl_ref[0]
    r = jax.nn.sigmoid(gates[:, :LRU_W])
    ig = jax.nn.sigmoid(gates[:, LRU_W:])
    log_a = -LRU_C * r * _softplus(-lam_ref[0])
    a = jnp.exp(log_a)
    a_scr[...] = a
    b_scr[...] = jnp.sqrt(-jnp.tanh(log_a) * (a * a + 1.0)) * ig * u

    ntile = tc // SUBLANES
    row = lax.broadcasted_iota(jnp.int32, (SUBLANES, LRU_W), 0)

    def scan(forward):
        def tile(i, carry):
            j = i if forward else ntile - 1 - i
            r0 = pl.multiple_of(j * SUBLANES, SUBLANES)
            a = a_scr[pl.ds(r0, SUBLANES), :]
            bv = b_scr[pl.ds(r0, SUBLANES), :]
            for d in (1, 2, 4):
                shift = d if forward else SUBLANES - d
                msk = (row >= d) if forward else (row < SUBLANES - d)
                ap = pltpu.roll(a, shift, 0)
                bp = pltpu.roll(bv, shift, 0)
                bv = jnp.where(msk, a * bp + bv, bv)
                a = jnp.where(msk, a * ap, a)
            h = a * carry + bv
            g0 = pl.multiple_of(t0 + r0, SUBLANES)
            if forward:
                hf_scr[pl.ds(g0, SUBLANES), :] = h
                return h[SUBLANES - 1:SUBLANES, :]
            b_scr[pl.ds(r0, SUBLANES), :] = h
            return h[0:1, :]

        idx = 0 if forward else 1
        last = lax.fori_loop(0, ntile, tile, carry_scr[idx:idx + 1, :])
        carry_scr[idx:idx + 1, :] = last
        if emit_final:
            fin_ref[0, 0, idx:idx + 1, :] = last
        if not forward:
            o_ref[...] = ((hf_scr[pl.ds(t0, tc), :] + b_scr[...])
                          * jax.nn.gelu(ya_ref[...])).astype(o_ref.dtype)

    pl.when(fwd)(lambda: scan(True))
    pl.when(jnp.logical_not(fwd))(lambda: scan(False))


def _lru_call(z, *, b, t, conv_w, conv_b, w_lru, b_lru, lam, h0, state_out, layer, name):
    tc = min(t, 512)
    nc = t // tc
    per8 = tc // SUBLANES
    nrow8 = b * t // SUBLANES

    def chunk(s):
        return jnp.where(s < nc, s, 2 * nc - 1 - s)

    def hold(s):
        return jnp.where(s < nc, nc - 1, 2 * nc - 1 - s)

    in_specs = [
        pl.BlockSpec((SUBLANES, LRU_W),
                     lambda bb, s: (jnp.maximum((bb * nc + chunk(s)) * per8 - 1, 0), 0)),
        pl.BlockSpec((tc, LRU_W), lambda bb, s: (bb * nc + chunk(s), 0)),
        pl.BlockSpec((SUBLANES, LRU_W),
                     lambda bb, s: (jnp.minimum((bb * nc + chunk(s) + 1) * per8, nrow8 - 1), 0)),
        pl.BlockSpec((tc, LRU_W), lambda bb, s: (bb * nc + hold(s), 1)),
        pl.BlockSpec((CONV_W, LRU_W), lambda bb, s: (0, 0)),
        pl.BlockSpec((1, LRU_W), lambda bb, s: (0, 0)),
        pl.BlockSpec((1, LRU_W, 2 * LRU_W), lambda bb, s: (s // nc, 0, 0)),
        pl.BlockSpec((1, 1, 2 * LRU_W), lambda bb, s: (s // nc, 0, 0)),
        pl.BlockSpec((1, 1, LRU_W), lambda bb, s: (s // nc, 0, 0)),
    ]
    args = [z, z, z, z, conv_w, conv_b.reshape(1, LRU_W), w_lru, b_lru, lam.reshape(2, 1, LRU_W)]
    if h0 is not None:
        in_specs.append(pl.BlockSpec((1, 1, 2, LRU_W), lambda bb, s: (bb, layer, 0, 0)))
        args.append(h0)
    out_shape = [jax.ShapeDtypeStruct((b * t, LRU_W), BF16)]
    out_specs = [pl.BlockSpec((tc, LRU_W), lambda bb, s: (bb * nc + hold(s), 0))]
    aliases = {}
    if state_out is not None:
        aliases[len(args)] = 1
        in_specs.append(pl.BlockSpec(memory_space=pl.ANY))
        args.append(state_out)
        out_shape.append(jax.ShapeDtypeStruct(state_out.shape, state_out.dtype))
        out_specs.append(pl.BlockSpec((1, 1, 2, LRU_W), lambda bb, s: (bb, layer, 0, 0)))
    res = pl.pallas_call(
        functools.partial(_lru_kernel, tc=tc, nc=nc, has_h0=h0 is not None,
                          emit_final=state_out is not None),
        grid=(b, 2 * nc),
        in_specs=in_specs,
        out_specs=out_specs,
        out_shape=out_shape,
        input_output_aliases=aliases,
        scratch_shapes=[pltpu.VMEM((t, LRU_W), F32), pltpu.VMEM((tc, LRU_W), F32),
                        pltpu.VMEM((tc, LRU_W), F32), pltpu.VMEM((2, LRU_W), F32)],
        compiler_params=_cparams(),
        name=name,
    )(*args)
    return res[0], (res[1] if state_out is not None else None)


def _merge_kernel(x_ref, gt_ref, a_ref, w_ref, g_ref, d_ref, wb_ref, wo_ref, mod_ref, o_ref):
    merged = None
    for bi, br_ref in enumerate((a_ref, w_ref, g_ref, d_ref)):
        gate = gt_ref[:, bi * D_MODEL:(bi + 1) * D_MODEL].astype(F32)
        term = gate * _dot(br_ref[...], wb_ref[0, bi])
        merged = term if merged is None else merged + term
    o_ref[...] = x_ref[...] + mod_ref[0][2:3] * _dot(merged.astype(BF16), wo_ref[0])


def _merge_call(x, gates, branches, w_branch, w_o, layer, mod, latent, name):
    m = x.shape[0]
    tm = 256
    row = lambda i: (i, 0)
    resident = pl.Buffered(1)
    in_specs = [pl.BlockSpec((tm, D_MODEL), row),
                pl.BlockSpec((tm, N_BRANCH * D_MODEL), row)]
    in_specs += [pl.BlockSpec((tm, BRANCH_W), row)] * N_BRANCH
    in_specs += [pl.BlockSpec((1, N_BRANCH, BRANCH_W, D_MODEL), lambda i: (layer, 0, 0, 0),
                              pipeline_mode=resident),
                 pl.BlockSpec((1, D_MODEL, D_MODEL), lambda i: (layer, 0, 0), pipeline_mode=resident),
                 pl.BlockSpec((1, N_MOD, D_MODEL), _mod_index(latent, tm))]
    return pl.pallas_call(
        _merge_kernel,
        grid=(m // tm,),
        in_specs=in_specs,
        out_specs=pl.BlockSpec((tm, D_MODEL), row),
        out_shape=jax.ShapeDtypeStruct((m, D_MODEL), F32),
        compiler_params=_cparams(),
        name=name,
    )(x, gates, *branches, w_branch, w_o, mod)


def _ffn_kernel(x_ref, mod_ref, g_ref, w1_ref, b1_ref, w2_ref, b2_ref, o_ref, h_scr, acc_scr):
    j = pl.program_id(1)

    def chunk(h):
        a = jnp.maximum(_dot(h, w1_ref[0]) + b1_ref[...], 0.0)
        return _dot((a * a).astype(BF16), w2_ref[0])

    @pl.when(j == 0)
    def _():
        for r in range(0, x_ref.shape[0], NORM_PIECE):
            rows = slice(r, r + NORM_PIECE)
            h = _norm_mod(x_ref[rows, :], g_ref[...], mod_ref[0], 3, 4).astype(BF16)
            h_scr[rows, :] = h
            acc_scr[rows, :] = chunk(h)

    @pl.when(j > 0)
    def _():
        acc_scr[...] += chunk(h_scr[...])

    @pl.when(j == pl.num_programs(1) - 1)
    def _():
        o_ref[...] = x_ref[...] + mod_ref[0][5:6] * (acc_scr[...] + b2_ref[...])


def _ffn_call(x, mod, g, w1, b1, w2, b2, layer, latent, name):
    m = x.shape[0]
    tm, tf = 512, 1024
    return pl.pallas_call(
        _ffn_kernel,
        grid=(m // tm, D_FF // tf),
        in_specs=[pl.BlockSpec((tm, D_MODEL), lambda i, j: (i, 0)),
                  pl.BlockSpec((1, N_MOD, D_MODEL), _mod_index(latent, tm)),
                  pl.BlockSpec((1, D_MODEL), lambda i, j: (0, 0)),
                  pl.BlockSpec((1, D_MODEL, tf), lambda i, j: (layer, 0, j)),
                  pl.BlockSpec((1, tf), lambda i, j: (0, j)),
                  pl.BlockSpec((1, tf, D_MODEL), lambda i, j: (layer, j, 0)),
                  pl.BlockSpec((1, D_MODEL), lambda i, j: (0, 0))],
        out_specs=pl.BlockSpec((tm, D_MODEL), lambda i, j: (i, 0)),
        out_shape=jax.ShapeDtypeStruct((m, D_MODEL), F32),
        scratch_shapes=[pltpu.VMEM((tm, D_MODEL), BF16), pltpu.VMEM((tm, D_MODEL), F32)],
        compiler_params=_cparams(),
        name=name,
    )(x, mod, g.reshape(1, D_MODEL), w1, b1.reshape(1, D_FF), w2, b2.reshape(1, D_MODEL))


def _block_diag(w):
    eye = jnp.eye(LRU_BLOCKS, dtype=w.dtype)
    return jnp.einsum('ncd,nm->ncmd', w, eye).reshape(LRU_W, LRU_W)


def _trunk_layer(x, mod, wts, layer, latent, rope, cached, new):
    b, t = (DEC_BATCH, DEC_SEQ) if latent else (BATCH, SEQ)
    tag = ('lat' if latent else 'ctx') + str(layer)
    gates, h = _gates_call(x, mod, wts['norm1_g'], wts['w_gate'], layer, wts['b_gate'], latent,
                           'gates_' + tag)
    z = _inproj_call(h, wts['w_in'], layer, 'inproj_' + tag)
    a_out, new_state = _lru_call(
        z, b=b, t=t, conv_w=wts['conv_w'], conv_b=wts['conv_b'], w_lru=wts['w_lru'], b_lru=wts['b_lru'],
        lam=wts['lru_lambda'], h0=cached[6] if latent else None, state_out=None if latent else new[6],
        layer=layer, name='lru_' + tag)
    w_out, new_win = _gqa_call(
        z, b=b, t=t, tq=256, qname='qw', kname='kw', vname='vw',
        qn=wts['win_qn'], kn=wts['win_kn'], sink=wts['win_sink'], rope=rope,
        cache=(cached[0], cached[1]) if latent else None, layer=layer, banded=latent,
        kv_out=None if latent else new[0:2], gps=1 if latent else WIN_KV, name='win_' + tag)
    g_out, new_grid = _gqa_call(
        z, b=b, t=t, tq=1024 if latent else SEQ, qname='qg', kname='kg', vname='vg',
        qn=wts['grid_qn'], kn=wts['grid_kn'], sink=None, rope=rope,
        cache=(cached[2], cached[3]) if latent else None, layer=layer, banded=False,
        kv_out=None if latent else new[2:4], gps=1 if latent else WIN_KV, name='grid_' + tag)
    d_out, new_diff = _diff_call(
        z, b=b, t=t, tq=1024 if latent else SEQ, qn=wts['diff_qn'], kn=wts['diff_kn'],
        lparams=wts['diff_lp'],
        out_g=wts['diff_out_g'], rope=rope, cache=(cached[4], cached[5]) if latent else None,
        layer=layer, kv_out=None if latent else new[4:6], hps=1 if latent else DIFF_HEADS,
        name='diff_' + tag)
    x = _merge_call(x, gates, (a_out, w_out, g_out, d_out), wts['w_branch'], wts['w_o'], layer, mod,
                    latent, 'merge_' + tag)
    x = _ffn_call(x, mod, wts['norm2_g'], wts['w_ff1'], wts['b_ff1'], wts['w_ff2'], wts['b_ff2'],
                  layer, latent, 'ffn_' + tag)
    if latent:
        return x, None
    return x, new_win + new_grid + new_diff + (new_state,)


def kernel(x_prompt, x_sample, cache_win_k, cache_win_v, cache_grid_k, cache_grid_v, cache_diff_k, cache_diff_v, state_lru, c, c_ctx, w_ada, b_ada, norm1_g, norm2_g, w_in, conv_w, conv_b, lru_wr, lru_br, lru_wi, lru_bi, lru_lambda, win_qn, win_kn, win_sink, grid_qn, grid_kn, diff_qn, diff_kn, diff_lq1, diff_lk1, diff_lq2, diff_lk2, diff_out_g, w_branch, w_gate, b_gate, w_o, w_ff1, b_ff1, w_ff2, b_ff2):
    cond = jnp.zeros((MOD_ROWS, D_MODEL), F32).at[0].set(c_ctx).at[1:1 + DEC_BATCH].set(c)
    mod_all = _modulation(cond, w_ada, b_ada).reshape(DEPTH, MOD_ROWS, N_MOD, D_MODEL)
    rope = _rope_tables(DEC_SEQ)
    cached_all = (cache_win_k.reshape(DEC_BATCH, DEPTH, PAST_LEN, LANES),
                  cache_win_v.reshape(DEC_BATCH, DEPTH, PAST_LEN, LANES),
                  cache_grid_k.reshape(DEC_BATCH, DEPTH, PAST_LEN, LANES),
                  cache_grid_v.reshape(DEC_BATCH, DEPTH, PAST_LEN, LANES),
                  cache_diff_k.reshape(DEC_BATCH, DEPTH, PAST_LEN, 512),
                  cache_diff_v.reshape(DEC_BATCH, DEPTH, PAST_LEN, 512))
    w_in_b, w_gate_b, w_branch_b, w_o_b, w_ff1_b, w_ff2_b = (
        w.astype(BF16) for w in (w_in, w_gate, w_branch, w_o, w_ff1, w_ff2))
    y_p = x_prompt.reshape(BATCH * SEQ, D_MODEL)
    y_s = x_sample.reshape(DEC_BATCH * DEC_SEQ, D_MODEL)
    new = tuple(jnp.zeros((BATCH, DEPTH, SEQ, w), F32) for w in (LANES,) * 4 + (512,) * 2)
    new += (jnp.zeros((BATCH, DEPTH, 2, LRU_W), F32),)
    for l in range(DEPTH):
        w_lru = jnp.stack([
            jnp.concatenate([_block_diag(lru_wr[l, k]), _block_diag(lru_wi[l, k])], axis=1)
            for k in range(2)]).astype(BF16)
        b_lru = jnp.concatenate([lru_br[l], lru_bi[l]], axis=-1).reshape(2, 1, 2 * LRU_W)
        wts = {
            'norm1_g': norm1_g[l], 'norm2_g': norm2_g[l],
            'w_in': w_in_b, 'w_gate': w_gate_b, 'b_gate': b_gate[l],
            'conv_w': conv_w[l], 'conv_b': conv_b[l], 'w_lru': w_lru, 'b_lru': b_lru,
            'lru_lambda': lru_lambda[l],
            'win_qn': win_qn[l], 'win_kn': win_kn[l], 'win_sink': win_sink[l],
            'grid_qn': grid_qn[l], 'grid_kn': grid_kn[l],
            'diff_qn': diff_qn[l], 'diff_kn': diff_kn[l],
            'diff_lp': jnp.stack([diff_lq1[l], diff_lk1[l], diff_lq2[l], diff_lk2[l]]),
            'diff_out_g': diff_out_g[l],
            'w_branch': w_branch_b, 'w_o': w_o_b,
            'w_ff1': w_ff1_b, 'b_ff1': b_ff1[l], 'w_ff2': w_ff2_b, 'b_ff2': b_ff2[l],
        }
        y_p, new = _trunk_layer(y_p, mod_all[l], wts, l, False, None, None, new)
        y_s, _ = _trunk_layer(y_s, mod_all[l], wts, l, True, rope, cached_all + (state_lru,), None)

    kv_shape = (BATCH, DEPTH, SEQ, WIN_KV, HEAD_DIM)
    return (y_p.reshape(BATCH, SEQ, D_MODEL),
            y_s.reshape(DEC_BATCH, DEC_SEQ, D_MODEL),
            new[0].reshape(kv_shape), new[1].reshape(kv_shape),
            new[2].reshape(kv_shape), new[3].reshape(kv_shape),
            new[4].reshape(BATCH, DEPTH, SEQ, DIFF_HEADS, 2, HEAD_DIM),
            new[5].reshape(BATCH, DEPTH, SEQ, DIFF_HEADS, 2 * HEAD_DIM),
            new[6])
```

```python
import functools
import math

import jax
import jax.numpy as jnp
import numpy as np
from jax import lax
from jax.experimental import pallas as pl
from jax.experimental.pallas import tpu as pltpu

F32 = jnp.float32
BF16 = jnp.bfloat16

D_MODEL = 2048
BATCH = 32
SEQ = 256
DEPTH = 4
DEC_BATCH = 4
DEC_SEQ = 4096
PAST_LEN = 512
GRID_W = 64
BLOCK = 128
HEAD_DIM = 64
N_FREQ = HEAD_DIM // 4
ROPE_BASE = 10000.0
ATTN_SCALE = HEAD_DIM ** -0.5
RMS_EPS = 1e-6
N_MOD = 6
N_BRANCH = 4
BRANCH_W = D_MODEL // 4
LRU_W = BRANCH_W
LRU_BLOCKS = 8
LRU_BW = LRU_W // LRU_BLOCKS
LRU_C = 8.0
CONV_W = 4
WIN_HEADS = 8
WIN_KV = 2
DIFF_HEADS = 4
D_FF = 4 * D_MODEL
D_IN = 4096

V7X_VMEM_BYTES = 64 * 1024 * 1024
VMEM_LIMIT = V7X_VMEM_BYTES - 8 * 1024 * 1024
LANES = 128
SUBLANES = 8
NEG = -1e30
LOG2E = math.log2(math.e)
ATTN_CHUNK = 512
FAST_SOFTMAX_BOUND = 40.0
NORM_PIECE = 256
MOD_ROWS = 8

_COL = dict(xa=0, ya=512, qw=1024, kw=1536, vw=1664, qg=1792, kg=2304, vg=2432, qd=2560, kd=3072,
            vd=3584)


def _cparams():
    return pltpu.CompilerParams(vmem_limit_bytes=VMEM_LIMIT)


def _dot(a, b):
    return jnp.dot(a, b, preferred_element_type=F32)


def _dot_nt(a, b):
    return lax.dot_general(a, b, (((1,), (1,)), ((), ())), preferred_element_type=F32)


def _mod_kernel(c_ref, w_ref, b_ref, o_ref):
    c = c_ref[...]
    s = (c * jax.nn.sigmoid(c)).astype(BF16)
    o_ref[0] = _dot(s, w_ref[0].astype(BF16)) + b_ref[0]


def _modulation(cond, w_ada, b_ada):
    tn = 1024
    n = N_MOD * D_MODEL
    return pl.pallas_call(
        _mod_kernel,
        grid=(DEPTH, n // tn),
        in_specs=[pl.BlockSpec((MOD_ROWS, D_MODEL), lambda l, j: (0, 0)),
                  pl.BlockSpec((1, D_MODEL, tn), lambda l, j: (l, 0, j)),
                  pl.BlockSpec((1, 1, tn), lambda l, j: (l, 0, j))],
        out_specs=pl.BlockSpec((1, MOD_ROWS, tn), lambda l, j: (l, 0, j)),
        out_shape=jax.ShapeDtypeStruct((DEPTH, MOD_ROWS, n), F32),
        compiler_params=_cparams(),
        name='modulation',
    )(cond, w_ada, b_ada.reshape(DEPTH, 1, n))


def _mod_index(latent, tm):
    if latent:
        per = DEC_SEQ // tm
        return lambda i, *_: (1 + i // per, 0, 0)
    return lambda i, *_: (0, 0, 0)


def _norm_mod(x, g, mod, shift_idx, scale_idx):
    var = jnp.mean(x * x, axis=-1, keepdims=True)
    y = x * lax.rsqrt(var + RMS_EPS) * g
    return y * (1.0 + mod[scale_idx:scale_idx + 1]) + mod[shift_idx:shift_idx + 1]


def _gates_kernel(x_ref, mod_ref, g_ref, w_ref, b_ref, o_ref, h_ref):
    def gate(h):
        v = _dot(h, w_ref[0]) + b_ref[...]
        return (0.5 * jnp.tanh(0.5 * v) + 0.5).astype(o_ref.dtype)

    @pl.when(pl.program_id(1) == 0)
    def _():
        for r in range(0, x_ref.shape[0], NORM_PIECE):
            rows = slice(r, r + NORM_PIECE)
            h = _norm_mod(x_ref[rows, :], g_ref[...], mod_ref[0], 0, 1).astype(BF16)
            h_ref[rows, :] = h
            o_ref[rows, :] = gate(h)

    @pl.when(pl.program_id(1) > 0)
    def _():
        o_ref[...] = gate(h_ref[...])


def _gates_call(x, mod, g, w, layer, bias, latent, name):
    m, n = x.shape[0], w.shape[2]
    tm, tn = 1024, 1024
    return pl.pallas_call(
        _gates_kernel,
        grid=(m // tm, n // tn),
        in_specs=[pl.BlockSpec((tm, D_MODEL), lambda i, j: (i, 0)),
                  pl.BlockSpec((1, N_MOD, D_MODEL), _mod_index(latent, tm)),
                  pl.BlockSpec((1, D_MODEL), lambda i, j: (0, 0)),
                  pl.BlockSpec((1, D_MODEL, tn), lambda i, j: (layer, 0, j)),
                  pl.BlockSpec((1, tn), lambda i, j: (0, j))],
        out_specs=[pl.BlockSpec((tm, tn), lambda i, j: (i, j)),
                   pl.BlockSpec((tm, D_MODEL), lambda i, j: (i, 0))],
        out_shape=[jax.ShapeDtypeStruct((m, n), BF16), jax.ShapeDtypeStruct((m, D_MODEL), BF16)],
        compiler_params=_cparams(),
        name=name,
    )(x, mod, g.reshape(1, D_MODEL), w, bias.reshape(1, n))


def _inproj_kernel(h_ref, w_ref, o_ref):
    o_ref[...] = _dot(h_ref[...], w_ref[0])


def _inproj_call(h, w, layer, name):
    m, n = h.shape[0], w.shape[2]
    tm, tn = 1024, 2048
    return pl.pallas_call(
        _inproj_kernel,
        grid=(m // tm, n // tn),
        in_specs=[pl.BlockSpec((tm, D_MODEL), lambda i, j: (i, 0)),
                  pl.BlockSpec((1, D_MODEL, tn), lambda i, j: (layer, 0, j))],
        out_specs=pl.BlockSpec((tm, tn), lambda i, j: (i, j)),
        out_shape=jax.ShapeDtypeStruct((m, n), F32),
        compiler_params=_cparams(),
        name=name,
    )(h, w)


def _lane_lo():
    return lax.broadcasted_iota(jnp.int32, (1, LANES), 1) < HEAD_DIM


def _seg_matrix():
    r = lax.broadcasted_iota(jnp.int32, (LANES, LANES), 0) // HEAD_DIM
    c = lax.broadcasted_iota(jnp.int32, (LANES, LANES), 1) // HEAD_DIM
    return jnp.where(r == c, 1.0, 0.0).astype(BF16)


def _head_rmsnorm(x, gain):
    x2 = x * x
    hi = x2.astype(BF16)
    lo = (x2 - hi.astype(F32)).astype(BF16)
    seg = _seg_matrix()
    ms = (_dot(hi, seg) + _dot(lo, seg)) * (1.0 / HEAD_DIM)
    return x * lax.rsqrt(ms + RMS_EPS) * gain


def _rope(x, cos, sin_signed):
    lane = lax.broadcasted_iota(jnp.int32, (1, LANES), 1)
    first = (lane & (2 * N_FREQ - 1)) < N_FREQ
    up = pltpu.roll(x, LANES - N_FREQ, 1)
    dn = pltpu.roll(x, N_FREQ, 1)
    return x * cos + jnp.where(first, up, dn) * sin_signed


def _score_bound(qn_ref, kn_ref, cached_k):
    root = math.sqrt(HEAD_DIM)
    qmax = root * jnp.max(jnp.abs(qn_ref[...]))
    kmax = root * jnp.max(jnp.abs(kn_ref[...]))
    if cached_k is not None:
        kmax = jnp.maximum(kmax, jnp.sqrt(jnp.max(jnp.sum(cached_k * cached_k, axis=-1, keepdims=True))))
    return qmax * kmax * (ATTN_SCALE * LOG2E)


def _lane_tile_sum(p):
    acc = p[:, 0:LANES]
    for j in range(1, p.shape[1] // LANES):
        acc = acc + p[:, j * LANES:(j + 1) * LANES]
    return acc


def _rope_tables(t):
    pos = jnp.arange(t)
    row = (pos // GRID_W).astype(F32)
    col = (pos % GRID_W).astype(F32)
    inv = ROPE_BASE ** (-jnp.arange(N_FREQ, dtype=F32) / N_FREQ)
    ar, ac = row[:, None] * inv, col[:, None] * inv
    cos = jnp.concatenate([jnp.cos(ar), jnp.cos(ar), jnp.cos(ac), jnp.cos(ac)], axis=-1)
    sin = jnp.concatenate([-jnp.sin(ar), jnp.sin(ar), -jnp.sin(ac), jnp.sin(ac)], axis=-1)
    return jnp.tile(cos, (1, 2)), jnp.tile(sin, (1, 2))


def _gqa_kernel(*refs, t, tq, n_ctx, banded, has_sink, has_rope, emit_k, ck, gps):
    it = iter(refs)
    q_refs = [next(it) for _ in range(gps)]
    k_ref, v_ref, qn_ref, kn_ref = (next(it) for _ in range(4))
    if has_rope:
        cq_ref, sq_ref, cka_ref, ska_ref = (next(it) for _ in range(4))
    if n_ctx:
        kc_ref, vc_ref = next(it), next(it)
    if has_sink:
        sink_ref = next(it)
    if emit_k:
        next(it), next(it)
    o_ref = next(it)
    if emit_k:
        ko_ref, vo_ref = next(it), next(it)
    k_scr, v_scr, bound_scr = next(it), next(it), next(it)

    qi = pl.program_id(2)
    lo = _lane_lo()
    lat0 = BLOCK if banded else 0
    ctx0 = t + 2 * lat0
    groups = [0, 1] if gps == 2 else [pl.program_id(1)]

    def pick(g, own, other):
        if isinstance(g, int):
            return own if g == 0 else other
        return jnp.where(g == 0, own, other)

    def put(dst, kx, vx):
        kr = pltpu.roll(kx, HEAD_DIM, 1)
        vr = pltpu.roll(vx, HEAD_DIM, 1)
        zero = jnp.zeros_like(kx)
        for gg, g in enumerate(groups):
            k_scr[2 * gg, dst, :] = jnp.where(lo, pick(g, kx, kr), zero).astype(BF16)
            k_scr[2 * gg + 1, dst, :] = jnp.where(lo, zero, pick(g, kr, kx)).astype(BF16)
            v_scr[2 * gg, dst, :] = jnp.where(lo, pick(g, vx, vr), zero).astype(BF16)
            v_scr[2 * gg + 1, dst, :] = jnp.where(lo, zero, pick(g, vr, vx)).astype(BF16)

    @pl.when(qi == 0)
    def _build():
        step = min(t, 512)
        for r in range(0, t, step):
            kx = _head_rmsnorm(k_ref[r:r + step, :], kn_ref[...])
            if emit_k:
                ko_ref[0, 0, r:r + step, :] = kx
                vo_ref[0, 0, r:r + step, :] = v_ref[r:r + step, :]
            if has_rope:
                kx = _rope(kx, cka_ref[r:r + step, :], ska_ref[r:r + step, :])
            put(slice(lat0 + r, lat0 + r + step), kx, v_ref[r:r + step, :])
        if banded:
            zpad = jnp.zeros((BLOCK, LANES), BF16)
            for scr in (k_scr, v_scr):
                for var in range(2 * gps):
                    scr[var, 0:BLOCK, :] = zpad
                    scr[var, lat0 + t:lat0 + t + BLOCK, :] = zpad
        if n_ctx:
            put(slice(ctx0, ctx0 + n_ctx), kc_ref[0, 0], vc_ref[0, 0])
        bound = _score_bound(qn_ref, kn_ref, kc_ref[0, 0] if n_ctx else None)
        if has_sink:
            for h in range(WIN_HEADS):
                bound = jnp.maximum(bound, jnp.abs(sink_ref[h]) * LOG2E)
        bound_scr[0] = bound

    def step_fn(base, q2, segs, carry):
        m0, m1, l, acc = carry

        def scores(var):
            parts = []
            for start, size, mask in segs:
                s = _dot_nt(q2, k_scr[var, pl.ds(start, size), :])
                parts.append(s if mask is None else jnp.where(mask, s, NEG))
            return parts[0] if len(parts) == 1 else jnp.concatenate(parts, axis=1)

        s0, s1 = scores(base), scores(base + 1)
        n0 = jnp.maximum(m0, jnp.max(s0, axis=-1, keepdims=True))
        n1 = jnp.maximum(m1, jnp.max(s1, axis=-1, keepdims=True))
        p0 = jnp.exp2(s0 - n0)
        p1 = jnp.exp2(s1 - n1)
        alpha = jnp.where(lo, jnp.exp2(m0 - n0), jnp.exp2(m1 - n1))
        rs = jnp.where(lo, jnp.sum(p0, axis=-1, keepdims=True), jnp.sum(p1, axis=-1, keepdims=True))
        l = alpha * l + rs
        acc = alpha * acc
        p0, p1 = p0.astype(BF16), p1.astype(BF16)
        off = 0
        for start, size, _ in segs:
            acc = (acc + _dot(p0[:, off:off + size], v_scr[base, pl.ds(start, size), :])
                   + _dot(p1[:, off:off + size], v_scr[base + 1, pl.ds(start, size), :]))
            off += size
        return n0, n1, l, acc

    def fast_step(base, q2, segs, carry):
        ls0, ls1, acc = carry
        for start, size, mask in segs:
            s0 = _dot_nt(q2, k_scr[base, pl.ds(start, size), :])
            s1 = _dot_nt(q2, k_scr[base + 1, pl.ds(start, size), :])
            if mask is not None:
                s0 = jnp.where(mask, s0, NEG)
                s1 = jnp.where(mask, s1, NEG)
            p0 = jnp.exp2(s0)
            p1 = jnp.exp2(s1)
            ls0 = ls0 + _lane_tile_sum(p0)
            ls1 = ls1 + _lane_tile_sum(p1)
            acc = (acc + _dot(p0.astype(BF16), v_scr[base, pl.ds(start, size), :])
                   + _dot(p1.astype(BF16), v_scr[base + 1, pl.ds(start, size), :]))
        return ls0, ls1, acc

    if banded:
        span = tq + 2 * BLOCK
        rr = lax.broadcasted_iota(jnp.int32, (tq, span), 0)
        cc = lax.broadcasted_iota(jnp.int32, (tq, span), 1)
        kpos = qi * tq - BLOCK + cc
        band_mask = (cc >= rr) & (cc - rr <= 2 * BLOCK) & (kpos >= 0) & (kpos < t)

    def attend(fast):
        if banded:
            chunks = [[(pl.multiple_of(qi * tq, tq), span, band_mask)]]
            if n_ctx:
                chunks[0].append((ctx0, n_ctx, None))
        else:
            chunks = [[(c * ck, ck, None)] for c in range((t + n_ctx) // ck)]
        zeros = jnp.zeros((tq, LANES), F32)
        for gg, g in enumerate(groups):
            q = q_refs[gg][...]
            for p in range(2):
                qp = _head_rmsnorm(q[:, p * LANES:(p + 1) * LANES], qn_ref[...])
                if has_rope:
                    qp = _rope(qp, cq_ref[...], sq_ref[...])
                q2 = (qp * (ATTN_SCALE * LOG2E)).astype(BF16)
                if has_sink:
                    h0 = g * 4 + 2 * p
                    sink0 = jnp.full((1, 1), sink_ref[h0] * LOG2E, F32)
                    sink1 = jnp.full((1, 1), sink_ref[h0 + 1] * LOG2E, F32)
                if fast:
                    carry = (zeros, zeros, zeros)
                    for segs in chunks:
                        carry = fast_step(2 * gg, q2, segs, carry)
                    ls0, ls1, acc = carry
                    l = jnp.where(lo, jnp.sum(ls0, axis=-1, keepdims=True),
                                  jnp.sum(ls1, axis=-1, keepdims=True))
                    if has_sink:
                        l = l + jnp.where(lo, jnp.exp2(sink0), jnp.exp2(sink1))
                else:
                    if has_sink:
                        m0 = jnp.broadcast_to(sink0, (tq, 1))
                        m1 = jnp.broadcast_to(sink1, (tq, 1))
                        l = jnp.ones((tq, LANES), F32)
                    else:
                        m0 = jnp.full((tq, 1), NEG, F32)
                        m1 = jnp.full((tq, 1), NEG, F32)
                        l = zeros
                    carry = (m0, m1, l, zeros)
                    if banded:
                        carry = step_fn(2 * gg, q2, chunks[0], carry)
                    else:
                        carry = lax.fori_loop(
                            0, len(chunks),
                            lambda c, cr: step_fn(2 * gg, q2, [(pl.multiple_of(c * ck, ck), ck, None)], cr),
                            carry)
                    _, _, l, acc = carry
                col = (2 * gg + p) * LANES
                o_ref[:, col:col + LANES] = (acc / l).astype(o_ref.dtype)

    fast_ok = bound_scr[0] <= FAST_SOFTMAX_BOUND
    pl.when(fast_ok)(lambda: attend(True))
    pl.when(jnp.logical_not(fast_ok))(lambda: attend(False))


def _gqa_call(z, *, b, t, tq, qname, kname, vname, qn, kn, sink, rope, cache, layer, banded,
              kv_out, gps, name):
    nq = t // tq
    n_ctx = PAST_LEN if cache is not None else 0
    rows = t + n_ctx + (2 * BLOCK if banded else 0)
    ck = min(ATTN_CHUNK, t + n_ctx)
    qblk, kblk, vblk = _COL[qname] // 256, _COL[kname] // LANES, _COL[vname] // LANES
    const = lambda bb, g, i: (0, 0)
    in_specs = [pl.BlockSpec((tq, 256), (lambda gg: lambda bb, g, i: (bb * nq + i, qblk + g * gps + gg))(gg))
                for gg in range(gps)]
    in_specs += [pl.BlockSpec((t, LANES), lambda bb, g, i: (bb, kblk)),
                 pl.BlockSpec((t, LANES), lambda bb, g, i: (bb, vblk)),
                 pl.BlockSpec((1, LANES), const),
                 pl.BlockSpec((1, LANES), const)]
    args = [z] * gps + [z, z, jnp.tile(qn, 2).reshape(1, LANES), jnp.tile(kn, 2).reshape(1, LANES)]
    if rope is not None:
        cos, sin = rope
        in_specs += [pl.BlockSpec((tq, LANES), lambda bb, g, i: (i, 0)),
                     pl.BlockSpec((tq, LANES), lambda bb, g, i: (i, 0)),
                     pl.BlockSpec((t, LANES), const),
                     pl.BlockSpec((t, LANES), const)]
        args += [cos, sin, cos, sin]
    if cache is not None:
        spec = pl.BlockSpec((1, 1, n_ctx, LANES), lambda bb, g, i: (bb, layer, 0, 0))
        in_specs += [spec, spec]
        args += [cache[0], cache[1]]
    if sink is not None:
        in_specs.append(pl.BlockSpec(memory_space=pltpu.SMEM))
        args.append(sink)
    out_shape = [jax.ShapeDtypeStruct((b * t, 512), BF16)]
    out_specs = [pl.BlockSpec((tq, 256 * gps), lambda bb, g, i: (bb * nq + i, g))]
    aliases = {}
    if kv_out is not None:
        for n_out, arr in enumerate(kv_out):
            aliases[len(args)] = 1 + n_out
            in_specs.append(pl.BlockSpec(memory_space=pl.ANY))
            args.append(arr)
            out_shape.append(jax.ShapeDtypeStruct(arr.shape, arr.dtype))
            out_specs.append(pl.BlockSpec((1, 1, t, LANES), lambda bb, g, i: (bb, layer, 0, 0)))
    res = pl.pallas_call(
        functools.partial(_gqa_kernel, t=t, tq=tq, n_ctx=n_ctx, banded=banded,
                          has_sink=sink is not None, has_rope=rope is not None,
                          emit_k=kv_out is not None, ck=ck, gps=gps),
        grid=(b, WIN_KV // gps, nq),
        in_specs=in_specs,
        out_specs=out_specs,
        out_shape=out_shape,
        input_output_aliases=aliases,
        scratch_shapes=[pltpu.VMEM((2 * gps, rows, LANES), BF16), pltpu.VMEM((2 * gps, rows, LANES), BF16),
                        pltpu.SMEM((1,), F32)],
        compiler_params=_cparams(),
        name=name,
    )(*args)
    return res[0], tuple(res[1:])


def _diff_kernel(*refs, t, tq, n_ctx, has_rope, emit_k, ck, lam_init, hps):
    it = iter(refs)
    q_ref, k_ref, v_ref, qn_ref, kn_ref, lp_ref, og_ref = (next(it) for _ in range(7))
    if has_rope:
        cq_ref, sq_ref, cka_ref, ska_ref = (next(it) for _ in range(4))
    if n_ctx:
        kc_ref, vc_ref = next(it), next(it)
    if emit_k:
        next(it), next(it)
    o_ref = next(it)
    if emit_k:
        ko_ref, vo_ref = next(it), next(it)
    k_scr, v_scr, bound_scr = next(it), next(it), next(it)

    qi = pl.program_id(2)
    lo = _lane_lo()
    heads = [(hh, slice(hh * LANES, (hh + 1) * LANES)) for hh in range(hps)]

    def put(hh, dst, kx, vx):
        zero = jnp.zeros_like(kx)
        k_scr[2 * hh, dst, :] = jnp.where(lo, kx, zero).astype(BF16)
        k_scr[2 * hh + 1, dst, :] = jnp.where(lo, zero, kx).astype(BF16)
        v_scr[hh, dst, :] = vx.astype(BF16)

    @pl.when(qi == 0)
    def _build():
        step = min(t, 512)
        for hh, cols in heads:
            for r in range(0, t, step):
                kx = _head_rmsnorm(k_ref[r:r + step, cols], kn_ref[...])
                if emit_k:
                    ko_ref[0, 0, r:r + step, cols] = kx
                    vo_ref[0, 0, r:r + step, cols] = v_ref[r:r + step, cols]
                if has_rope:
                    kx = _rope(kx, cka_ref[r:r + step, :], ska_ref[r:r + step, :])
                put(hh, slice(r, r + step), kx, v_ref[r:r + step, cols])
            if n_ctx:
                put(hh, slice(t, t + n_ctx), kc_ref[0, 0, :, cols], vc_ref[0, 0, :, cols])
        bound_scr[0] = _score_bound(qn_ref, kn_ref, kc_ref[0, 0] if n_ctx else None)

    def query(cols):
        qp = _head_rmsnorm(q_ref[:, cols], qn_ref[...])
        if has_rope:
            qp = _rope(qp, cq_ref[...], sq_ref[...])
        return (qp * (ATTN_SCALE * LOG2E)).astype(BF16)

    def step_fn(hh, q2, start, carry):
        m0, m1, l0, l1, a0, a1 = carry
        v = v_scr[hh, pl.ds(start, ck), :]
        s0 = _dot_nt(q2, k_scr[2 * hh, pl.ds(start, ck), :])
        s1 = _dot_nt(q2, k_scr[2 * hh + 1, pl.ds(start, ck), :])
        n0 = jnp.maximum(m0, jnp.max(s0, axis=-1, keepdims=True))
        n1 = jnp.maximum(m1, jnp.max(s1, axis=-1, keepdims=True))
        p0 = jnp.exp2(s0 - n0)
        p1 = jnp.exp2(s1 - n1)
        e0 = jnp.exp2(m0 - n0)
        e1 = jnp.exp2(m1 - n1)
        l0 = e0 * l0 + jnp.sum(p0, axis=-1, keepdims=True)
        l1 = e1 * l1 + jnp.sum(p1, axis=-1, keepdims=True)
        a0 = e0 * a0 + _dot(p0.astype(BF16), v)
        a1 = e1 * a1 + _dot(p1.astype(BF16), v)
        return n0, n1, l0, l1, a0, a1

    def fast_step(hh, q2, start, carry):
        ls0, ls1, a0, a1 = carry
        v = v_scr[hh, pl.ds(start, ck), :]
        p0 = jnp.exp2(_dot_nt(q2, k_scr[2 * hh, pl.ds(start, ck), :]))
        p1 = jnp.exp2(_dot_nt(q2, k_scr[2 * hh + 1, pl.ds(start, ck), :]))
        return (ls0 + _lane_tile_sum(p0), ls1 + _lane_tile_sum(p1),
                a0 + _dot(p0.astype(BF16), v), a1 + _dot(p1.astype(BF16), v))

    def finish(cols, l0, l1, a0, a1):
        lp = lp_ref[...]
        lam = (jnp.exp(jnp.sum(lp[0:1] * lp[1:2], axis=-1, keepdims=True))
               - jnp.exp(jnp.sum(lp[2:3] * lp[3:4], axis=-1, keepdims=True)) + lam_init)
        o = a0 / l0 - lam * (a1 / l1)
        var = jnp.mean(o * o, axis=-1, keepdims=True)
        o = o * lax.rsqrt(var + RMS_EPS) * og_ref[...] * (1.0 - lam_init)
        o_ref[:, cols] = o.astype(o_ref.dtype)

    zeros = jnp.zeros((tq, LANES), F32)
    starts = [c * ck for c in range((t + n_ctx) // ck)]

    def attend_fast():
        for hh, cols in heads:
            q2 = query(cols)
            carry = (zeros, zeros, zeros, zeros)
            for start in starts:
                carry = fast_step(hh, q2, start, carry)
            ls0, ls1, a0, a1 = carry
            finish(cols, jnp.sum(ls0, axis=-1, keepdims=True), jnp.sum(ls1, axis=-1, keepdims=True),
                   a0, a1)

    def attend_online():
        col = lambda val: jnp.full((tq, 1), val, F32)
        for hh, cols in heads:
            q2 = query(cols)
            carry = (col(NEG), col(NEG), col(0.0), col(0.0), zeros, zeros)
            carry = lax.fori_loop(
                0, len(starts), lambda c, cr: step_fn(hh, q2, pl.multiple_of(c * ck, ck), cr), carry)
            finish(cols, *carry[2:])

    fast_ok = bound_scr[0] <= FAST_SOFTMAX_BOUND
    pl.when(fast_ok)(attend_fast)
    pl.when(jnp.logical_not(fast_ok))(attend_online)


def _diff_call(z, *, b, t, tq, qn, kn, lparams, out_g, rope, cache, layer, kv_out, hps, name):
    nq = t // tq
    n_ctx = PAST_LEN if cache is not None else 0
    ck = min(ATTN_CHUNK, t + n_ctx)
    width = LANES * hps
    qblk, kblk, vblk = _COL['qd'] // width, _COL['kd'] // width, _COL['vd'] // width
    const = lambda bb, h, i: (0, 0)
    in_specs = [pl.BlockSpec((tq, width), lambda bb, h, i: (bb * nq + i, qblk + h)),
                pl.BlockSpec((t, width), lambda bb, h, i: (bb, kblk + h)),
                pl.BlockSpec((t, width), lambda bb, h, i: (bb, vblk + h)),
                pl.BlockSpec((1, LANES), const),
                pl.BlockSpec((1, LANES), const),
                pl.BlockSpec((4, HEAD_DIM), const),
                pl.BlockSpec((1, LANES), const)]
    args = [z, z, z, jnp.tile(qn, 2).reshape(1, LANES), jnp.tile(kn, 2).reshape(1, LANES),
            lparams, out_g.reshape(1, LANES)]
    if rope is not None:
        cos, sin = rope
        in_specs += [pl.BlockSpec((tq, LANES), lambda bb, h, i: (i, 0)),
                     pl.BlockSpec((tq, LANES), lambda bb, h, i: (i, 0)),
                     pl.BlockSpec((t, LANES), const),
                     pl.BlockSpec((t, LANES), const)]
        args += [cos, sin, cos, sin]
    if cache is not None:
        spec = pl.BlockSpec((1, 1, n_ctx, width), lambda bb, h, i: (bb, layer, 0, h))
        in_specs += [spec, spec]
        args += [cache[0], cache[1]]
    out_shape = [jax.ShapeDtypeStruct((b * t, 512), BF16)]
    out_specs = [pl.BlockSpec((tq, width), lambda bb, h, i: (bb * nq + i, h))]
    aliases = {}
    if kv_out is not None:
        for n_out, arr in enumerate(kv_out):
            aliases[len(args)] = 1 + n_out
            in_specs.append(pl.BlockSpec(memory_space=pl.ANY))
            args.append(arr)
            out_shape.append(jax.ShapeDtypeStruct(arr.shape, arr.dtype))
            out_specs.append(pl.BlockSpec((1, 1, t, width), lambda bb, h, i: (bb, layer, 0, h)))
    lam_init = 0.8 - 0.6 * math.exp(-0.3 * layer)
    res = pl.pallas_call(
        functools.partial(_diff_kernel, t=t, tq=tq, n_ctx=n_ctx, has_rope=rope is not None,
                          emit_k=kv_out is not None, ck=ck, lam_init=lam_init, hps=hps),
        grid=(b, DIFF_HEADS // hps, nq),
        in_specs=in_specs,
        out_specs=out_specs,
        out_shape=out_shape,
        input_output_aliases=aliases,
        scratch_shapes=[pltpu.VMEM((2 * hps, t + n_ctx, LANES), BF16),
                        pltpu.VMEM((hps, t + n_ctx, LANES), BF16), pltpu.SMEM((1,), F32)],
        compiler_params=_cparams(),
        name=name,
    )(*args)
    return res[0], tuple(res[1:])


def _softplus(x):
    return jnp.maximum(x, 0.0) + jnp.log1p(jnp.exp(-jnp.abs(x)))


def _lru_kernel(*refs, tc, nc, has_h0, emit_final):
    it = iter(refs)
    xp_ref, xm_ref, xn_ref, ya_ref, cw_ref, cb_ref, wl_ref, bl_ref, lam_ref = (next(it) for _ in range(9))
    if has_h0:
        h0_ref = next(it)
    if emit_final:
        next(it)
    o_ref = next(it)
    if emit_final:
        fin_ref = next(it)
    hf_scr, u_scr, a_scr, b_scr, carry_scr = (next(it) for _ in range(5))

    s = pl.program_id(1)
    fwd = s < nc
    c = jnp.where(fwd, s, 2 * nc - 1 - s)
    t0 = pl.multiple_of(c * tc, tc)

    @pl.when(s == 0)
    def _():
        if has_h0:
            carry_scr[...] = h0_ref[0, 0]
        else:
            carry_scr[...] = jnp.zeros_like(carry_scr)

    def conv():
        prev = jnp.where(c > 0, xp_ref[...], 0.0)
        nxt = jnp.where(c < nc - 1, xn_ref[...], 0.0)
        ext = jnp.concatenate([prev, xm_ref[...], nxt], axis=0)
        n_ext = tc + 2 * SUBLANES
        u = cb_ref[...] + cw_ref[1:2] * xm_ref[...]
        u = u + cw_ref[0:1] * pltpu.roll(ext, 1, 0)[SUBLANES:SUBLANES + tc]
        u = u + cw_ref[2:3] * pltpu.roll(ext, n_ext - 1, 0)[SUBLANES:SUBLANES + tc]
        u = u + cw_ref[3:4] * pltpu.roll(ext, n_ext - 2, 0)[SUBLANES:SUBLANES + tc]
        u_scr[pl.ds(t0, tc), :] = u
        return u

    def gate_terms(u):
        gates = _dot(u.astype(BF16), wl_ref[0]) + bl_ref[0]
        r = jax.nn.sigmoid(gates[:, :LRU_W])
        ig = jax.nn.sigmoid(gates[:, LRU_W:])
        log_a = -LRU_C * r * _softplus(-lam_ref[0])
        a = jnp.exp(log_a)
        a_scr[...] = a
        b_scr[...] = jnp.sqrt(-jnp.tanh(log_a) * (a * a + 1.0)) * ig * u

    ntile = tc // SUBLANES
    row = lax.broadcasted_iota(jnp.int32, (SUBLANES, LRU_W), 0)

    def scan(forward):
        def tile(i, carry):
            j = i if forward else ntile - 1 - i
            r0 = pl.multiple_of(j * SUBLANES, SUBLANES)
            a = a_scr[pl.ds(r0, SUBLANES), :]
            bv = b_scr[pl.ds(r0, SUBLANES), :]
            for d in (1, 2, 4):
                shift = d if forward else SUBLANES - d
                msk = (row >= d) if forward else (row < SUBLANES - d)
                ap = pltpu.roll(a, shift, 0)
                bp = pltpu.roll(bv, shift, 0)
                bv = jnp.where(msk, a * bp + bv, bv)
                a = jnp.where(msk, a * ap, a)
            h = a * carry + bv
            g0 = pl.multiple_of(t0 + r0, SUBLANES)
            if forward:
                hf_scr[pl.ds(g0, SUBLANES), :] = h
                return h[SUBLANES - 1:SUBLANES, :]
            b_scr[pl.ds(r0, SUBLANES), :] = h
            return h[0:1, :]

        gate_terms(conv() if forward else u_scr[pl.ds(t0, tc), :])
        idx = 0 if forward else 1
        last = lax.fori_loop(0, ntile, tile, carry_scr[idx:idx + 1, :])
        carry_scr[idx:idx + 1, :] = last
        if emit_final:
            fin_ref[0, 0, idx:idx + 1, :] = last
        if not forward:
            o_ref[...] = ((hf_scr[pl.ds(t0, tc), :] + b_scr[...])
                          * jax.nn.gelu(ya_ref[...])).astype(o_ref.dtype)

    pl.when(fwd)(lambda: scan(True))
    pl.when(jnp.logical_not(fwd))(lambda: scan(False))


def _lru_call(z, *, b, t, conv_w, conv_b, w_lru, b_lru, lam, h0, state_out, layer, name):
    tc = min(t, 512)
    nc = t // tc
    per8 = tc // SUBLANES
    nrow8 = b * t // SUBLANES

    def chunk(s):
        return jnp.where(s < nc, s, 2 * nc - 1 - s)

    def hold(s):
        return jnp.where(s < nc, nc - 1, 2 * nc - 1 - s)

    in_specs = [
        pl.BlockSpec((SUBLANES, LRU_W),
                     lambda bb, s: (jnp.maximum((bb * nc + chunk(s)) * per8 - 1, 0), 0)),
        pl.BlockSpec((tc, LRU_W), lambda bb, s: (bb * nc + chunk(s), 0)),
        pl.BlockSpec((SUBLANES, LRU_W),
                     lambda bb, s: (jnp.minimum((bb * nc + chunk(s) + 1) * per8, nrow8 - 1), 0)),
        pl.BlockSpec((tc, LRU_W), lambda bb, s: (bb * nc + hold(s), 1)),
        pl.BlockSpec((CONV_W, LRU_W), lambda bb, s: (0, 0)),
        pl.BlockSpec((1, LRU_W), lambda bb, s: (0, 0)),
        pl.BlockSpec((1, LRU_W, 2 * LRU_W), lambda bb, s: (s // nc, 0, 0)),
        pl.BlockSpec((1, 1, 2 * LRU_W), lambda bb, s: (s // nc, 0, 0)),
        pl.BlockSpec((1, 1, LRU_W), lambda bb, s: (s // nc, 0, 0)),
    ]
    args = [z, z, z, z, conv_w, conv_b.reshape(1, LRU_W), w_lru, b_lru, lam.reshape(2, 1, LRU_W)]
    if h0 is not None:
        in_specs.append(pl.BlockSpec((1, 1, 2, LRU_W), lambda bb, s: (bb, layer, 0, 0)))
        args.append(h0)
    out_shape = [jax.ShapeDtypeStruct((b * t, LRU_W), BF16)]
    out_specs = [pl.BlockSpec((tc, LRU_W), lambda bb, s: (bb * nc + hold(s), 0))]
    aliases = {}
    if state_out is not None:
        aliases[len(args)] = 1
        in_specs.append(pl.BlockSpec(memory_space=pl.ANY))
        args.append(state_out)
        out_shape.append(jax.ShapeDtypeStruct(state_out.shape, state_out.dtype))
        out_specs.append(pl.BlockSpec((1, 1, 2, LRU_W), lambda bb, s: (bb, layer, 0, 0)))
    res = pl.pallas_call(
        functools.partial(_lru_kernel, tc=tc, nc=nc, has_h0=h0 is not None,
                          emit_final=state_out is not None),
        grid=(b, 2 * nc),
        in_specs=in_specs,
        out_specs=out_specs,
        out_shape=out_shape,
        input_output_aliases=aliases,
        scratch_shapes=[pltpu.VMEM((t, LRU_W), F32), pltpu.VMEM((t, LRU_W), F32),
                        pltpu.VMEM((tc, LRU_W), F32), pltpu.VMEM((tc, LRU_W), F32),
                        pltpu.VMEM((2, LRU_W), F32)],
        compiler_params=_cparams(),
        name=name,
    )(*args)
    return res[0], (res[1] if state_out is not None else None)


def _merge_kernel(x_ref, gt_ref, a_ref, w_ref, g_ref, d_ref, wb_ref, wo_ref, mod_ref, o_ref):
    merged = None
    for bi, br_ref in enumerate((a_ref, w_ref, g_ref, d_ref)):
        gate = gt_ref[:, bi * D_MODEL:(bi + 1) * D_MODEL].astype(F32)
        term = gate * _dot(br_ref[...], wb_ref[0, bi])
        merged = term if merged is None else merged + term
    o_ref[...] = x_ref[...] + mod_ref[0][2:3] * _dot(merged.astype(BF16), wo_ref[0])


def _merge_call(x, gates, branches, w_branch, w_o, layer, mod, latent, name):
    m = x.shape[0]
    tm = 256
    row = lambda i: (i, 0)
    resident = pl.Buffered(1)
    in_specs = [pl.BlockSpec((tm, D_MODEL), row),
                pl.BlockSpec((tm, N_BRANCH * D_MODEL), row)]
    in_specs += [pl.BlockSpec((tm, BRANCH_W), row)] * N_BRANCH
    in_specs += [pl.BlockSpec((1, N_BRANCH, BRANCH_W, D_MODEL), lambda i: (layer, 0, 0, 0),
                              pipeline_mode=resident),
                 pl.BlockSpec((1, D_MODEL, D_MODEL), lambda i: (layer, 0, 0), pipeline_mode=resident),
                 pl.BlockSpec((1, N_MOD, D_MODEL), _mod_index(latent, tm))]
    return pl.pallas_call(
        _merge_kernel,
        grid=(m // tm,),
        in_specs=in_specs,
        out_specs=pl.BlockSpec((tm, D_MODEL), row),
        out_shape=jax.ShapeDtypeStruct((m, D_MODEL), F32),
        compiler_params=_cparams(),
        name=name,
    )(x, gates, *branches, w_branch, w_o, mod)


def _ffn_kernel(x_ref, mod_ref, g_ref, w1_ref, b1_ref, w2_ref, b2_ref, o_ref, h_scr, acc_scr):
    j = pl.program_id(1)

    def chunk(h):
        a = jnp.maximum(_dot(h, w1_ref[0]) + b1_ref[...], 0.0)
        return _dot((a * a).astype(BF16), w2_ref[0])

    @pl.when(j == 0)
    def _():
        for r in range(0, x_ref.shape[0], NORM_PIECE):
            rows = slice(r, r + NORM_PIECE)
            h = _norm_mod(x_ref[rows, :], g_ref[...], mod_ref[0], 3, 4).astype(BF16)
            h_scr[rows, :] = h
            acc_scr[rows, :] = chunk(h)

    @pl.when(j > 0)
    def _():
        acc_scr[...] += chunk(h_scr[...])

    @pl.when(j == pl.num_programs(1) - 1)
    def _():
        o_ref[...] = x_ref[...] + mod_ref[0][5:6] * (acc_scr[...] + b2_ref[...])


def _ffn_call(x, mod, g, w1, b1, w2, b2, layer, latent, name):
    m = x.shape[0]
    tm, tf = 512, 1024
    return pl.pallas_call(
        _ffn_kernel,
        grid=(m // tm, D_FF // tf),
        in_specs=[pl.BlockSpec((tm, D_MODEL), lambda i, j: (i, 0)),
                  pl.BlockSpec((1, N_MOD, D_MODEL), _mod_index(latent, tm)),
                  pl.BlockSpec((1, D_MODEL), lambda i, j: (0, 0)),
                  pl.BlockSpec((1, D_MODEL, tf), lambda i, j: (layer, 0, j)),
                  pl.BlockSpec((1, tf), lambda i, j: (0, j)),
                  pl.BlockSpec((1, tf, D_MODEL), lambda i, j: (layer, j, 0)),
                  pl.BlockSpec((1, D_MODEL), lambda i, j: (0, 0))],
        out_specs=pl.BlockSpec((tm, D_MODEL), lambda i, j: (i, 0)),
        out_shape=jax.ShapeDtypeStruct((m, D_MODEL), F32),
        scratch_shapes=[pltpu.VMEM((tm, D_MODEL), BF16), pltpu.VMEM((tm, D_MODEL), F32)],
        compiler_params=_cparams(),
        name=name,
    )(x, mod, g.reshape(1, D_MODEL), w1, b1.reshape(1, D_FF), w2, b2.reshape(1, D_MODEL))


def _block_diag(w):
    eye = jnp.eye(LRU_BLOCKS, dtype=w.dtype)
    return jnp.einsum('ncd,nm->ncmd', w, eye).reshape(LRU_W, LRU_W)


def _trunk_layer(x, mod, wts, layer, latent, rope, cached, new):
    b, t = (DEC_BATCH, DEC_SEQ) if latent else (BATCH, SEQ)
    tag = ('lat' if latent else 'ctx') + str(layer)
    gates, h = _gates_call(x, mod, wts['norm1_g'], wts['w_gate'], layer, wts['b_gate'], latent,
                           'gates_' + tag)
    z = _inproj_call(h, wts['w_in'], layer, 'inproj_' + tag)
    a_out, new_state = _lru_call(
        z, b=b, t=t, conv_w=wts['conv_w'], conv_b=wts['conv_b'], w_lru=wts['w_lru'], b_lru=wts['b_lru'],
        lam=wts['lru_lambda'], h0=cached[6] if latent else None, state_out=None if latent else new[6],
        layer=layer, name='lru_' + tag)
    w_out, new_win = _gqa_call(
        z, b=b, t=t, tq=256, qname='qw', kname='kw', vname='vw',
        qn=wts['win_qn'], kn=wts['win_kn'], sink=wts['win_sink'], rope=rope,
        cache=(cached[0], cached[1]) if latent else None, layer=layer, banded=latent,
        kv_out=None if latent else new[0:2], gps=1 if latent else WIN_KV, name='win_' + tag)
    g_out, new_grid = _gqa_call(
        z, b=b, t=t, tq=1024 if latent else SEQ, qname='qg', kname='kg', vname='vg',
        qn=wts['grid_qn'], kn=wts['grid_kn'], sink=None, rope=rope,
        cache=(cached[2], cached[3]) if latent else None, layer=layer, banded=False,
        kv_out=None if latent else new[2:4], gps=1 if latent else WIN_KV, name='grid_' + tag)
    d_out, new_diff = _diff_call(
        z, b=b, t=t, tq=1024 if latent else SEQ, qn=wts['diff_qn'], kn=wts['diff_kn'],
        lparams=wts['diff_lp'],
        out_g=wts['diff_out_g'], rope=rope, cache=(cached[4], cached[5]) if latent else None,
        layer=layer, kv_out=None if latent else new[4:6], hps=1 if latent else DIFF_HEADS,
        name='diff_' + tag)
    x = _merge_call(x, gates, (a_out, w_out, g_out, d_out), wts['w_branch'], wts['w_o'], layer, mod,
                    latent, 'merge_' + tag)
    x = _ffn_call(x, mod, wts['norm2_g'], wts['w_ff1'], wts['b_ff1'], wts['w_ff2'], wts['b_ff2'],
                  layer, latent, 'ffn_' + tag)
    if latent:
        return x, None
    return x, new_win + new_grid + new_diff + (new_state,)


def kernel(x_prompt, x_sample, cache_win_k, cache_win_v, cache_grid_k, cache_grid_v, cache_diff_k, cache_diff_v, state_lru, c, c_ctx, w_ada, b_ada, norm1_g, norm2_g, w_in, conv_w, conv_b, lru_wr, lru_br, lru_wi, lru_bi, lru_lambda, win_qn, win_kn, win_sink, grid_qn, grid_kn, diff_qn, diff_kn, diff_lq1, diff_lk1, diff_lq2, diff_lk2, diff_out_g, w_branch, w_gate, b_gate, w_o, w_ff1, b_ff1, w_ff2, b_ff2):
    cond = jnp.zeros((MOD_ROWS, D_MODEL), F32).at[0].set(c_ctx).at[1:1 + DEC_BATCH].set(c)
    mod_all = _modulation(cond, w_ada, b_ada).reshape(DEPTH, MOD_ROWS, N_MOD, D_MODEL)
    rope = _rope_tables(DEC_SEQ)
    cached_all = (cache_win_k.reshape(DEC_BATCH, DEPTH, PAST_LEN, LANES),
                  cache_win_v.reshape(DEC_BATCH, DEPTH, PAST_LEN, LANES),
                  cache_grid_k.reshape(DEC_BATCH, DEPTH, PAST_LEN, LANES),
                  cache_grid_v.reshape(DEC_BATCH, DEPTH, PAST_LEN, LANES),
                  cache_diff_k.reshape(DEC_BATCH, DEPTH, PAST_LEN, 512),
                  cache_diff_v.reshape(DEC_BATCH, DEPTH, PAST_LEN, 512))
    w_in_b, w_gate_b, w_branch_b, w_o_b, w_ff1_b, w_ff2_b = (
        w.astype(BF16) for w in (w_in, w_gate, w_branch, w_o, w_ff1, w_ff2))
    y_p = x_prompt.reshape(BATCH * SEQ, D_MODEL)
    y_s = x_sample.reshape(DEC_BATCH * DEC_SEQ, D_MODEL)
    new = tuple(jnp.zeros((BATCH, DEPTH, SEQ, w), F32) for w in (LANES,) * 4 + (512,) * 2)
    new += (jnp.zeros((BATCH, DEPTH, 2, LRU_W), F32),)
    for l in range(DEPTH):
        w_lru = jnp.stack([
            jnp.concatenate([_block_diag(lru_wr[l, k]), _block_diag(lru_wi[l, k])], axis=1)
            for k in range(2)]).astype(BF16)
        b_lru = jnp.concatenate([lru_br[l], lru_bi[l]], axis=-1).reshape(2, 1, 2 * LRU_W)
        wts = {
            'norm1_g': norm1_g[l], 'norm2_g': norm2_g[l],
            'w_in': w_in_b, 'w_gate': w_gate_b, 'b_gate': b_gate[l],
            'conv_w': conv_w[l], 'conv_b': conv_b[l], 'w_lru': w_lru, 'b_lru': b_lru,
            'lru_lambda': lru_lambda[l],
            'win_qn': win_qn[l], 'win_kn': win_kn[l], 'win_sink': win_sink[l],
            'grid_qn': grid_qn[l], 'grid_kn': grid_kn[l],
            'diff_qn': diff_qn[l], 'diff_kn': diff_kn[l],
            'diff_lp': jnp.stack([diff_lq1[l], diff_lk1[l], diff_lq2[l], diff_lk2[l]]),
            'diff_out_g': diff_out_g[l],
            'w_branch': w_branch_b, 'w_o': w_o_b,
            'w_ff1': w_ff1_b, 'b_ff1': b_ff1[l], 'w_ff2': w_ff2_b, 'b_ff2': b_ff2[l],
        }
        y_p, new = _trunk_layer(y_p, mod_all[l], wts, l, False, None, None, new)
        y_s, _ = _trunk_layer(y_s, mod_all[l], wts, l, True, rope, cached_all + (state_lru,), None)

    kv_shape = (BATCH, DEPTH, SEQ, WIN_KV, HEAD_DIM)
    return (y_p.reshape(BATCH, SEQ, D_MODEL),
            y_s.reshape(DEC_BATCH, DEC_SEQ, D_MODEL),
            new[0].reshape(kv_shape), new[1].reshape(kv_shape),
            new[2].reshape(kv_shape), new[3].reshape(kv_shape),
            new[4].reshape(BATCH, DEPTH, SEQ, DIFF_HEADS, 2, HEAD_DIM),
            new[5].reshape(BATCH, DEPTH, SEQ, DIFF_HEADS, 2 * HEAD_DIM),
            new[6])
```

```python
import functools
import math

import jax
import jax.numpy as jnp
from jax import lax
from jax.experimental import pallas as pl
from jax.experimental.pallas import tpu as pltpu

F32 = jnp.float32
BF16 = jnp.bfloat16

D_MODEL = 2048
BATCH = 32
SEQ = 256
DEPTH = 4
DEC_BATCH = 4
DEC_SEQ = 4096
PAST_LEN = 512
GRID_W = 64
BLOCK = 128
HEAD_DIM = 64
N_FREQ = HEAD_DIM // 4
ROPE_BASE = 10000.0
ATTN_SCALE = HEAD_DIM ** -0.5
RMS_EPS = 1e-6
N_MOD = 6
N_BRANCH = 4
BRANCH_W = D_MODEL // 4
LRU_W = BRANCH_W
LRU_BLOCKS = 8
LRU_BW = LRU_W // LRU_BLOCKS
LRU_C = 8.0
CONV_W = 4
WIN_HEADS = 8
WIN_KV = 2
DIFF_HEADS = 4
D_FF = 4 * D_MODEL

V7X_VMEM_BYTES = 64 * 1024 * 1024
VMEM_LIMIT = V7X_VMEM_BYTES - 8 * 1024 * 1024
LANES = 128
SUBLANES = 8
NEG = -1e30
LOG2E = math.log2(math.e)
ATTN_CHUNK = 512
FAST_SOFTMAX_BOUND = 40.0
NORM_PIECE = 256
MOD_ROWS = 8

_COL = dict(xa=0, ya=512, qw=1024, kw=1536, vw=1664, qg=1792, kg=2304, vg=2432, qd=2560, kd=3072,
            vd=3584)


def _cparams():
    return pltpu.CompilerParams(vmem_limit_bytes=VMEM_LIMIT)


def _dot(a, b):
    return jnp.dot(a, b, preferred_element_type=F32)


def _dot_nt(a, b):
    return lax.dot_general(a, b, (((1,), (1,)), ((), ())), preferred_element_type=F32)


def _mod_kernel(c_ref, w_ref, b_ref, o_ref):
    c = c_ref[...]
    s = (c * jax.nn.sigmoid(c)).astype(BF16)
    o_ref[0] = _dot(s, w_ref[0].astype(BF16)) + b_ref[0]


def _modulation(cond, w_ada, b_ada):
    tn = 1024
    n = N_MOD * D_MODEL
    return pl.pallas_call(
        _mod_kernel,
        grid=(DEPTH, n // tn),
        in_specs=[pl.BlockSpec((MOD_ROWS, D_MODEL), lambda l, j: (0, 0)),
                  pl.BlockSpec((1, D_MODEL, tn), lambda l, j: (l, 0, j)),
                  pl.BlockSpec((1, 1, tn), lambda l, j: (l, 0, j))],
        out_specs=pl.BlockSpec((1, MOD_ROWS, tn), lambda l, j: (l, 0, j)),
        out_shape=jax.ShapeDtypeStruct((DEPTH, MOD_ROWS, n), F32),
        compiler_params=_cparams(),
        name='modulation',
    )(cond, w_ada, b_ada.reshape(DEPTH, 1, n))


def _mod_index(latent, tm):
    if latent:
        per = DEC_SEQ // tm
        return lambda i, *_: (1 + i // per, 0, 0)
    return lambda i, *_: (0, 0, 0)


def _norm_mod(x, g, mod, shift_idx, scale_idx):
    var = jnp.mean(x * x, axis=-1, keepdims=True)
    y = x * lax.rsqrt(var + RMS_EPS) * g
    return y * (1.0 + mod[scale_idx:scale_idx + 1]) + mod[shift_idx:shift_idx + 1]


def _gates_kernel(x_ref, mod_ref, g_ref, w_ref, b_ref, o_ref, h_ref):
    def gate(h):
        v = _dot(h, w_ref[0]) + b_ref[...]
        return (0.5 * jnp.tanh(0.5 * v) + 0.5).astype(o_ref.dtype)

    @pl.when(pl.program_id(1) == 0)
    def _():
        for r in range(0, x_ref.shape[0], NORM_PIECE):
            rows = slice(r, r + NORM_PIECE)
            h = _norm_mod(x_ref[rows, :], g_ref[...], mod_ref[0], 0, 1).astype(BF16)
            h_ref[rows, :] = h
            o_ref[rows, :] = gate(h)

    @pl.when(pl.program_id(1) > 0)
    def _():
        o_ref[...] = gate(h_ref[...])


def _gates_call(x, mod, g, w, layer, bias, latent, name):
    m, n = x.shape[0], w.shape[2]
    tm, tn = 1024, 1024
    return pl.pallas_call(
        _gates_kernel,
        grid=(m // tm, n // tn),
        in_specs=[pl.BlockSpec((tm, D_MODEL), lambda i, j: (i, 0)),
                  pl.BlockSpec((1, N_MOD, D_MODEL), _mod_index(latent, tm)),
                  pl.BlockSpec((1, D_MODEL), lambda i, j: (0, 0)),
                  pl.BlockSpec((1, D_MODEL, tn), lambda i, j: (layer, 0, j)),
                  pl.BlockSpec((1, tn), lambda i, j: (0, j))],
        out_specs=[pl.BlockSpec((tm, tn), lambda i, j: (i, j)),
                   pl.BlockSpec((tm, D_MODEL), lambda i, j: (i, 0))],
        out_shape=[jax.ShapeDtypeStruct((m, n), BF16), jax.ShapeDtypeStruct((m, D_MODEL), BF16)],
        compiler_params=_cparams(),
        name=name,
    )(x, mod, g.reshape(1, D_MODEL), w, bias.reshape(1, n))


def _inproj_kernel(h_ref, w_ref, o_ref):
    o_ref[...] = _dot(h_ref[...], w_ref[0])


def _inproj_call(h, w, layer, name):
    m, n = h.shape[0], w.shape[2]
    tm, tn = 1024, 2048
    return pl.pallas_call(
        _inproj_kernel,
        grid=(m // tm, n // tn),
        in_specs=[pl.BlockSpec((tm, D_MODEL), lambda i, j: (i, 0)),
                  pl.BlockSpec((1, D_MODEL, tn), lambda i, j: (layer, 0, j))],
        out_specs=pl.BlockSpec((tm, tn), lambda i, j: (i, j)),
        out_shape=jax.ShapeDtypeStruct((m, n), F32),
        compiler_params=_cparams(),
        name=name,
    )(h, w)


def _lane_lo():
    return lax.broadcasted_iota(jnp.int32, (1, LANES), 1) < HEAD_DIM


def _seg_matrix():
    r = lax.broadcasted_iota(jnp.int32, (LANES, LANES), 0) // HEAD_DIM
    c = lax.broadcasted_iota(jnp.int32, (LANES, LANES), 1) // HEAD_DIM
    return jnp.where(r == c, 1.0, 0.0).astype(BF16)


def _head_rmsnorm(x, gain):
    x2 = x * x
    hi = x2.astype(BF16)
    lo = (x2 - hi.astype(F32)).astype(BF16)
    seg = _seg_matrix()
    ms = (_dot(hi, seg) + _dot(lo, seg)) * (1.0 / HEAD_DIM)
    return x * lax.rsqrt(ms + RMS_EPS) * gain


def _rope(x, cos, sin_signed):
    lane = lax.broadcasted_iota(jnp.int32, (1, LANES), 1)
    first = (lane & (2 * N_FREQ - 1)) < N_FREQ
    up = pltpu.roll(x, LANES - N_FREQ, 1)
    dn = pltpu.roll(x, N_FREQ, 1)
    return x * cos + jnp.where(first, up, dn) * sin_signed


def _score_bound(qn_ref, kn_ref, cached_k):
    root = math.sqrt(HEAD_DIM)
    qmax = root * jnp.max(jnp.abs(qn_ref[...]))
    kmax = root * jnp.max(jnp.abs(kn_ref[...]))
    if cached_k is not None:
        kmax = jnp.maximum(kmax, jnp.sqrt(jnp.max(jnp.sum(cached_k * cached_k, axis=-1, keepdims=True))))
    return qmax * kmax * (ATTN_SCALE * LOG2E)


def _lane_tile_sum(p):
    acc = p[:, 0:LANES]
    for j in range(1, p.shape[1] // LANES):
        acc = acc + p[:, j * LANES:(j + 1) * LANES]
    return acc


def _rope_tables(t):
    pos = jnp.arange(t)
    row = (pos // GRID_W).astype(F32)
    col = (pos % GRID_W).astype(F32)
    inv = ROPE_BASE ** (-jnp.arange(N_FREQ, dtype=F32) / N_FREQ)
    ar, ac = row[:, None] * inv, col[:, None] * inv
    cos = jnp.concatenate([jnp.cos(ar), jnp.cos(ar), jnp.cos(ac), jnp.cos(ac)], axis=-1)
    sin = jnp.concatenate([-jnp.sin(ar), jnp.sin(ar), -jnp.sin(ac), jnp.sin(ac)], axis=-1)
    return jnp.tile(cos, (1, 2)), jnp.tile(sin, (1, 2))


def _gqa_kernel(*refs, t, tq, n_ctx, banded, has_sink, has_rope, emit_k, ck, gps):
    it = iter(refs)
    q_refs = [next(it) for _ in range(gps)]
    k_ref, v_ref, qn_ref, kn_ref = (next(it) for _ in range(4))
    if has_rope:
        cq_ref, sq_ref, cka_ref, ska_ref = (next(it) for _ in range(4))
    if n_ctx:
        kc_ref, vc_ref = next(it), next(it)
    if has_sink:
        sink_ref = next(it)
    if emit_k:
        next(it), next(it)
    o_ref = next(it)
    if emit_k:
        ko_ref, vo_ref = next(it), next(it)
    k_scr, v_scr, bound_scr = next(it), next(it), next(it)

    qi = pl.program_id(2)
    lo = _lane_lo()
    lat0 = BLOCK if banded else 0
    ctx0 = t + 2 * lat0
    groups = [0, 1] if gps == 2 else [pl.program_id(1)]

    def pick(g, own, other):
        if isinstance(g, int):
            return own if g == 0 else other
        return jnp.where(g == 0, own, other)

    def put(dst, kx, vx):
        kr = pltpu.roll(kx, HEAD_DIM, 1)
        vr = pltpu.roll(vx, HEAD_DIM, 1)
        zero = jnp.zeros_like(kx)
        for gg, g in enumerate(groups):
            k_scr[2 * gg, dst, :] = jnp.where(lo, pick(g, kx, kr), zero).astype(BF16)
            k_scr[2 * gg + 1, dst, :] = jnp.where(lo, zero, pick(g, kr, kx)).astype(BF16)
            v_scr[2 * gg, dst, :] = jnp.where(lo, pick(g, vx, vr), zero).astype(BF16)
            v_scr[2 * gg + 1, dst, :] = jnp.where(lo, zero, pick(g, vr, vx)).astype(BF16)

    @pl.when(qi == 0)
    def _build():
        step = min(t, 512)
        for r in range(0, t, step):
            kx = _head_rmsnorm(k_ref[r:r + step, :], kn_ref[...])
            if emit_k:
                ko_ref[0, 0, r:r + step, :] = kx
                vo_ref[0, 0, r:r + step, :] = v_ref[r:r + step, :]
            if has_rope:
                kx = _rope(kx, cka_ref[r:r + step, :], ska_ref[r:r + step, :])
            put(slice(lat0 + r, lat0 + r + step), kx, v_ref[r:r + step, :])
        if banded:
            zpad = jnp.zeros((BLOCK, LANES), BF16)
            for scr in (k_scr, v_scr):
                for var in range(2 * gps):
                    scr[var, 0:BLOCK, :] = zpad
                    scr[var, lat0 + t:lat0 + t + BLOCK, :] = zpad
        if n_ctx:
            put(slice(ctx0, ctx0 + n_ctx), kc_ref[0, 0], vc_ref[0, 0])
        bound = _score_bound(qn_ref, kn_ref, kc_ref[0, 0] if n_ctx else None)
        if has_sink:
            for h in range(WIN_HEADS):
                bound = jnp.maximum(bound, jnp.abs(sink_ref[h]) * LOG2E)
        bound_scr[0] = bound

    def step_fn(base, q2, segs, carry):
        m0, m1, l, acc = carry

        def scores(var):
            parts = []
            for start, size, mask in segs:
                s = _dot_nt(q2, k_scr[var, pl.ds(start, size), :])
                parts.append(s if mask is None else jnp.where(mask, s, NEG))
            return parts[0] if len(parts) == 1 else jnp.concatenate(parts, axis=1)

        s0, s1 = scores(base), scores(base + 1)
        n0 = jnp.maximum(m0, jnp.max(s0, axis=-1, keepdims=True))
        n1 = jnp.maximum(m1, jnp.max(s1, axis=-1, keepdims=True))
        p0 = jnp.exp2(s0 - n0)
        p1 = jnp.exp2(s1 - n1)
        alpha = jnp.where(lo, jnp.exp2(m0 - n0), jnp.exp2(m1 - n1))
        rs = jnp.where(lo, jnp.sum(p0, axis=-1, keepdims=True), jnp.sum(p1, axis=-1, keepdims=True))
        l = alpha * l + rs
        acc = alpha * acc
        p0, p1 = p0.astype(BF16), p1.astype(BF16)
        off = 0
        for start, size, _ in segs:
            acc = (acc + _dot(p0[:, off:off + size], v_scr[base, pl.ds(start, size), :])
                   + _dot(p1[:, off:off + size], v_scr[base + 1, pl.ds(start, size), :]))
            off += size
        return n0, n1, l, acc

    def fast_step(base, q2, segs, carry):
        ls0, ls1, acc = carry
        for start, size, mask in segs:
            s0 = _dot_nt(q2, k_scr[base, pl.ds(start, size), :])
            s1 = _dot_nt(q2, k_scr[base + 1, pl.ds(start, size), :])
            if mask is not None:
                s0 = jnp.where(mask, s0, NEG)
                s1 = jnp.where(mask, s1, NEG)
            p0 = jnp.exp2(s0)
            p1 = jnp.exp2(s1)
            ls0 = ls0 + _lane_tile_sum(p0)
            ls1 = ls1 + _lane_tile_sum(p1)
            acc = (acc + _dot(p0.astype(BF16), v_scr[base, pl.ds(start, size), :])
                   + _dot(p1.astype(BF16), v_scr[base + 1, pl.ds(start, size), :]))
        return ls0, ls1, acc

    if banded:
        span = tq + 2 * BLOCK
        rr = lax.broadcasted_iota(jnp.int32, (tq, span), 0)
        cc = lax.broadcasted_iota(jnp.int32, (tq, span), 1)
        kpos = qi * tq - BLOCK + cc
        band_mask = (cc >= rr) & (cc - rr <= 2 * BLOCK) & (kpos >= 0) & (kpos < t)

    def attend(fast):
        if banded:
            chunks = [[(pl.multiple_of(qi * tq, tq), span, band_mask)]]
            if n_ctx:
                chunks[0].append((ctx0, n_ctx, None))
        else:
            chunks = [[(c * ck, ck, None)] for c in range((t + n_ctx) // ck)]
        zeros = jnp.zeros((tq, LANES), F32)
        for gg, g in enumerate(groups):
            q = q_refs[gg][...]
            for p in range(2):
                qp = _head_rmsnorm(q[:, p * LANES:(p + 1) * LANES], qn_ref[...])
                if has_rope:
                    qp = _rope(qp, cq_ref[...], sq_ref[...])
                q2 = (qp * (ATTN_SCALE * LOG2E)).astype(BF16)
                if has_sink:
                    h0 = g * 4 + 2 * p
                    sink0 = jnp.full((1, 1), sink_ref[h0] * LOG2E, F32)
                    sink1 = jnp.full((1, 1), sink_ref[h0 + 1] * LOG2E, F32)
                if fast:
                    carry = (zeros, zeros, zeros)
                    for segs in chunks:
                        carry = fast_step(2 * gg, q2, segs, carry)
                    ls0, ls1, acc = carry
                    l = jnp.where(lo, jnp.sum(ls0, axis=-1, keepdims=True),
                                  jnp.sum(ls1, axis=-1, keepdims=True))
                    if has_sink:
                        l = l + jnp.where(lo, jnp.exp2(sink0), jnp.exp2(sink1))
                else:
                    if has_sink:
                        m0 = jnp.broadcast_to(sink0, (tq, 1))
                        m1 = jnp.broadcast_to(sink1, (tq, 1))
                        l = jnp.ones((tq, LANES), F32)
                    else:
                        m0 = jnp.full((tq, 1), NEG, F32)
                        m1 = jnp.full((tq, 1), NEG, F32)
                        l = zeros
                    carry = (m0, m1, l, zeros)
                    if banded:
                        carry = step_fn(2 * gg, q2, chunks[0], carry)
                    else:
                        carry = lax.fori_loop(
                            0, len(chunks),
                            lambda c, cr: step_fn(2 * gg, q2, [(pl.multiple_of(c * ck, ck), ck, None)], cr),
                            carry)
                    _, _, l, acc = carry
                col = (2 * gg + p) * LANES
                o_ref[:, col:col + LANES] = (acc / l).astype(o_ref.dtype)

    fast_ok = bound_scr[0] <= FAST_SOFTMAX_BOUND
    pl.when(fast_ok)(lambda: attend(True))
    pl.when(jnp.logical_not(fast_ok))(lambda: attend(False))


def _gqa_call(z, *, b, t, tq, qname, kname, vname, qn, kn, sink, rope, cache, layer, banded,
              kv_out, gps, name):
    nq = t // tq
    n_ctx = PAST_LEN if cache is not None else 0
    rows = t + n_ctx + (2 * BLOCK if banded else 0)
    ck = min(ATTN_CHUNK, t + n_ctx)
    qblk, kblk, vblk = _COL[qname] // 256, _COL[kname] // LANES, _COL[vname] // LANES
    const = lambda bb, g, i: (0, 0)
    in_specs = [pl.BlockSpec((tq, 256), (lambda gg: lambda bb, g, i: (bb * nq + i, qblk + g * gps + gg))(gg))
                for gg in range(gps)]
    in_specs += [pl.BlockSpec((t, LANES), lambda bb, g, i: (bb, kblk)),
                 pl.BlockSpec((t, LANES), lambda bb, g, i: (bb, vblk)),
                 pl.BlockSpec((1, LANES), const),
                 pl.BlockSpec((1, LANES), const)]
    args = [z] * gps + [z, z, jnp.tile(qn, 2).reshape(1, LANES), jnp.tile(kn, 2).reshape(1, LANES)]
    if rope is not None:
        cos, sin = rope
        in_specs += [pl.BlockSpec((tq, LANES), lambda bb, g, i: (i, 0)),
                     pl.BlockSpec((tq, LANES), lambda bb, g, i: (i, 0)),
                     pl.BlockSpec((t, LANES), const),
                     pl.BlockSpec((t, LANES), const)]
        args += [cos, sin, cos, sin]
    if cache is not None:
        spec = pl.BlockSpec((1, 1, n_ctx, LANES), lambda bb, g, i: (bb, layer, 0, 0))
        in_specs += [spec, spec]
        args += [cache[0], cache[1]]
    if sink is not None:
        in_specs.append(pl.BlockSpec(memory_space=pltpu.SMEM))
        args.append(sink)
    out_shape = [jax.ShapeDtypeStruct((b * t, 512), BF16)]
    out_specs = [pl.BlockSpec((tq, 256 * gps), lambda bb, g, i: (bb * nq + i, g))]
    aliases = {}
    if kv_out is not None:
        for n_out, arr in enumerate(kv_out):
            aliases[len(args)] = 1 + n_out
            in_specs.append(pl.BlockSpec(memory_space=pl.ANY))
            args.append(arr)
            out_shape.append(jax.ShapeDtypeStruct(arr.shape, arr.dtype))
            out_specs.append(pl.BlockSpec((1, 1, t, LANES), lambda bb, g, i: (bb, layer, 0, 0)))
    res = pl.pallas_call(
        functools.partial(_gqa_kernel, t=t, tq=tq, n_ctx=n_ctx, banded=banded,
                          has_sink=sink is not None, has_rope=rope is not None,
                          emit_k=kv_out is not None, ck=ck, gps=gps),
        grid=(b, WIN_KV // gps, nq),
        in_specs=in_specs,
        out_specs=out_specs,
        out_shape=out_shape,
        input_output_aliases=aliases,
        scratch_shapes=[pltpu.VMEM((2 * gps, rows, LANES), BF16), pltpu.VMEM((2 * gps, rows, LANES), BF16),
                        pltpu.SMEM((1,), F32)],
        compiler_params=_cparams(),
        name=name,
    )(*args)
    return res[0], tuple(res[1:])


def _diff_kernel(*refs, t, tq, n_ctx, has_rope, emit_k, ck, lam_init, hps):
    it = iter(refs)
    q_ref, k_ref, v_ref, qn_ref, kn_ref, lp_ref, og_ref = (next(it) for _ in range(7))
    if has_rope:
        cq_ref, sq_ref, cka_ref, ska_ref = (next(it) for _ in range(4))
    if n_ctx:
        kc_ref, vc_ref = next(it), next(it)
    if emit_k:
        next(it), next(it)
    o_ref = next(it)
    if emit_k:
        ko_ref, vo_ref = next(it), next(it)
    k_scr, v_scr, bound_scr = next(it), next(it), next(it)

    qi = pl.program_id(2)
    lo = _lane_lo()
    heads = [(hh, slice(hh * LANES, (hh + 1) * LANES)) for hh in range(hps)]

    def put(hh, dst, kx, vx):
        zero = jnp.zeros_like(kx)
        k_scr[2 * hh, dst, :] = jnp.where(lo, kx, zero).astype(BF16)
        k_scr[2 * hh + 1, dst, :] = jnp.where(lo, zero, kx).astype(BF16)
        v_scr[hh, dst, :] = vx.astype(BF16)

    @pl.when(qi == 0)
    def _build():
        step = min(t, 512)
        for hh, cols in heads:
            for r in range(0, t, step):
                kx = _head_rmsnorm(k_ref[r:r + step, cols], kn_ref[...])
                if emit_k:
                    ko_ref[0, 0, r:r + step, cols] = kx
                    vo_ref[0, 0, r:r + step, cols] = v_ref[r:r + step, cols]
                if has_rope:
                    kx = _rope(kx, cka_ref[r:r + step, :], ska_ref[r:r + step, :])
                put(hh, slice(r, r + step), kx, v_ref[r:r + step, cols])
            if n_ctx:
                put(hh, slice(t, t + n_ctx), kc_ref[0, 0, :, cols], vc_ref[0, 0, :, cols])
        bound_scr[0] = _score_bound(qn_ref, kn_ref, kc_ref[0, 0] if n_ctx else None)

    def query(cols):
        qp = _head_rmsnorm(q_ref[:, cols], qn_ref[...])
        if has_rope:
            qp = _rope(qp, cq_ref[...], sq_ref[...])
        return (qp * (ATTN_SCALE * LOG2E)).astype(BF16)

    def step_fn(hh, q2, start, carry):
        m0, m1, l0, l1, a0, a1 = carry
        v = v_scr[hh, pl.ds(start, ck), :]
        s0 = _dot_nt(q2, k_scr[2 * hh, pl.ds(start, ck), :])
        s1 = _dot_nt(q2, k_scr[2 * hh + 1, pl.ds(start, ck), :])
        n0 = jnp.maximum(m0, jnp.max(s0, axis=-1, keepdims=True))
        n1 = jnp.maximum(m1, jnp.max(s1, axis=-1, keepdims=True))
        p0 = jnp.exp2(s0 - n0)
        p1 = jnp.exp2(s1 - n1)
        e0 = jnp.exp2(m0 - n0)
        e1 = jnp.exp2(m1 - n1)
        l0 = e0 * l0 + jnp.sum(p0, axis=-1, keepdims=True)
        l1 = e1 * l1 + jnp.sum(p1, axis=-1, keepdims=True)
        a0 = e0 * a0 + _dot(p0.astype(BF16), v)
        a1 = e1 * a1 + _dot(p1.astype(BF16), v)
        return n0, n1, l0, l1, a0, a1

    def fast_step(hh, q2, start, carry):
        ls0, ls1, a0, a1 = carry
        v = v_scr[hh, pl.ds(start, ck), :]
        p0 = jnp.exp2(_dot_nt(q2, k_scr[2 * hh, pl.ds(start, ck), :]))
        p1 = jnp.exp2(_dot_nt(q2, k_scr[2 * hh + 1, pl.ds(start, ck), :]))
        return (ls0 + _lane_tile_sum(p0), ls1 + _lane_tile_sum(p1),
                a0 + _dot(p0.astype(BF16), v), a1 + _dot(p1.astype(BF16), v))

    def finish(cols, l0, l1, a0, a1):
        lp = lp_ref[...]
        lam = (jnp.exp(jnp.sum(lp[0:1] * lp[1:2], axis=-1, keepdims=True))
               - jnp.exp(jnp.sum(lp[2:3] * lp[3:4], axis=-1, keepdims=True)) + lam_init)
        o = a0 / l0 - lam * (a1 / l1)
        var = jnp.mean(o * o, axis=-1, keepdims=True)
        o = o * lax.rsqrt(var + RMS_EPS) * og_ref[...] * (1.0 - lam_init)
        o_ref[:, cols] = o.astype(o_ref.dtype)

    zeros = jnp.zeros((tq, LANES), F32)
    starts = [c * ck for c in range((t + n_ctx) // ck)]

    def attend_fast():
        for hh, cols in heads:
            q2 = query(cols)
            carry = (zeros, zeros, zeros, zeros)
            for start in starts:
                carry = fast_step(hh, q2, start, carry)
            ls0, ls1, a0, a1 = carry
            finish(cols, jnp.sum(ls0, axis=-1, keepdims=True), jnp.sum(ls1, axis=-1, keepdims=True),
                   a0, a1)

    def attend_online():
        col = lambda val: jnp.full((tq, 1), val, F32)
        for hh, cols in heads:
            q2 = query(cols)
            carry = (col(NEG), col(NEG), col(0.0), col(0.0), zeros, zeros)
            carry = lax.fori_loop(
                0, len(starts), lambda c, cr: step_fn(hh, q2, pl.multiple_of(c * ck, ck), cr), carry)
            finish(cols, *carry[2:])

    fast_ok = bound_scr[0] <= FAST_SOFTMAX_BOUND
    pl.when(fast_ok)(attend_fast)
    pl.when(jnp.logical_not(fast_ok))(attend_online)


def _diff_call(z, *, b, t, tq, qn, kn, lparams, out_g, rope, cache, layer, kv_out, hps, name):
    nq = t // tq
    n_ctx = PAST_LEN if cache is not None else 0
    ck = min(ATTN_CHUNK, t + n_ctx)
    width = LANES * hps
    qblk, kblk, vblk = _COL['qd'] // width, _COL['kd'] // width, _COL['vd'] // width
    const = lambda bb, h, i: (0, 0)
    in_specs = [pl.BlockSpec((tq, width), lambda bb, h, i: (bb * nq + i, qblk + h)),
                pl.BlockSpec((t, width), lambda bb, h, i: (bb, kblk + h)),
                pl.BlockSpec((t, width), lambda bb, h, i: (bb, vblk + h)),
                pl.BlockSpec((1, LANES), const),
                pl.BlockSpec((1, LANES), const),
                pl.BlockSpec((4, HEAD_DIM), const),
                pl.BlockSpec((1, LANES), const)]
    args = [z, z, z, jnp.tile(qn, 2).reshape(1, LANES), jnp.tile(kn, 2).reshape(1, LANES),
            lparams, out_g.reshape(1, LANES)]
    if rope is not None:
        cos, sin = rope
        in_specs += [pl.BlockSpec((tq, LANES), lambda bb, h, i: (i, 0)),
                     pl.BlockSpec((tq, LANES), lambda bb, h, i: (i, 0)),
                     pl.BlockSpec((t, LANES), const),
                     pl.BlockSpec((t, LANES), const)]
        args += [cos, sin, cos, sin]
    if cache is not None:
        spec = pl.BlockSpec((1, 1, n_ctx, width), lambda bb, h, i: (bb, layer, 0, h))
        in_specs += [spec, spec]
        args += [cache[0], cache[1]]
    out_shape = [jax.ShapeDtypeStruct((b * t, 512), BF16)]
    out_specs = [pl.BlockSpec((tq, width), lambda bb, h, i: (bb * nq + i, h))]
    aliases = {}
    if kv_out is not None:
        for n_out, arr in enumerate(kv_out):
            aliases[len(args)] = 1 + n_out
            in_specs.append(pl.BlockSpec(memory_space=pl.ANY))
            args.append(arr)
            out_shape.append(jax.ShapeDtypeStruct(arr.shape, arr.dtype))
            out_specs.append(pl.BlockSpec((1, 1, t, width), lambda bb, h, i: (bb, layer, 0, h)))
    lam_init = 0.8 - 0.6 * math.exp(-0.3 * layer)
    res = pl.pallas_call(
        functools.partial(_diff_kernel, t=t, tq=tq, n_ctx=n_ctx, has_rope=rope is not None,
                          emit_k=kv_out is not None, ck=ck, lam_init=lam_init, hps=hps),
        grid=(b, DIFF_HEADS // hps, nq),
        in_specs=in_specs,
        out_specs=out_specs,
        out_shape=out_shape,
        input_output_aliases=aliases,
        scratch_shapes=[pltpu.VMEM((2 * hps, t + n_ctx, LANES), BF16),
                        pltpu.VMEM((hps, t + n_ctx, LANES), BF16), pltpu.SMEM((1,), F32)],
        compiler_params=_cparams(),
        name=name,
    )(*args)
    return res[0], tuple(res[1:])


def _softplus(x):
    return jnp.maximum(x, 0.0) + jnp.log1p(jnp.exp(-jnp.abs(x)))


def _lru_kernel(*refs, tc, nc, has_h0, emit_final):
    it = iter(refs)
    xp_ref, xm_ref, xn_ref, ya_ref, cw_ref, cb_ref, wl_ref, bl_ref, lam_ref = (next(it) for _ in range(9))
    if has_h0:
        h0_ref = next(it)
    if emit_final:
        next(it)
    o_ref = next(it)
    if emit_final:
        fin_ref = next(it)
    hf_scr, u_scr, a_scr, b_scr, carry_scr = (next(it) for _ in range(5))

    s = pl.program_id(1)
    fwd = s < nc
    c = jnp.where(fwd, s, 2 * nc - 1 - s)
    t0 = pl.multiple_of(c * tc, tc)

    @pl.when(s == 0)
    def _():
        if has_h0:
            carry_scr[...] = h0_ref[0, 0]
        else:
            carry_scr[...] = jnp.zeros_like(carry_scr)

    def conv():
        prev = jnp.where(c > 0, xp_ref[...], 0.0)
        nxt = jnp.where(c < nc - 1, xn_ref[...], 0.0)
        ext = jnp.concatenate([prev, xm_ref[...], nxt], axis=0)
        n_ext = tc + 2 * SUBLANES
        u = cb_ref[...] + cw_ref[1:2] * xm_ref[...]
        u = u + cw_ref[0:1] * pltpu.roll(ext, 1, 0)[SUBLANES:SUBLANES + tc]
        u = u + cw_ref[2:3] * pltpu.roll(ext, n_ext - 1, 0)[SUBLANES:SUBLANES + tc]
        u = u + cw_ref[3:4] * pltpu.roll(ext, n_ext - 2, 0)[SUBLANES:SUBLANES + tc]
        u_scr[pl.ds(t0, tc), :] = u
        return u

    def gate_terms(u):
        gates = _dot(u.astype(BF16), wl_ref[0]) + bl_ref[0]
        r = jax.nn.sigmoid(gates[:, :LRU_W])
        ig = jax.nn.sigmoid(gates[:, LRU_W:])
        log_a = -LRU_C * r * _softplus(-lam_ref[0])
        a = jnp.exp(log_a)
        a_scr[...] = a
        b_scr[...] = jnp.sqrt(-jnp.tanh(log_a) * (a * a + 1.0)) * ig * u

    ntile = tc // SUBLANES
    row = lax.broadcasted_iota(jnp.int32, (SUBLANES, LRU_W), 0)

    def scan(forward):
        def tile(i, carry):
            j = i if forward else ntile - 1 - i
            r0 = pl.multiple_of(j * SUBLANES, SUBLANES)
            a = a_scr[pl.ds(r0, SUBLANES), :]
            bv = b_scr[pl.ds(r0, SUBLANES), :]
            for d in (1, 2, 4):
                shift = d if forward else SUBLANES - d
                msk = (row >= d) if forward else (row < SUBLANES - d)
                ap = pltpu.roll(a, shift, 0)
                bp = pltpu.roll(bv, shift, 0)
                bv = jnp.where(msk, a * bp + bv, bv)
                a = jnp.where(msk, a * ap, a)
            h = a * carry + bv
            g0 = pl.multiple_of(t0 + r0, SUBLANES)
            if forward:
                hf_scr[pl.ds(g0, SUBLANES), :] = h
                return h[SUBLANES - 1:SUBLANES, :]
            b_scr[pl.ds(r0, SUBLANES), :] = h
            return h[0:1, :]

        gate_terms(conv() if forward else u_scr[pl.ds(t0, tc), :])
        idx = 0 if forward else 1
        last = lax.fori_loop(0, ntile, tile, carry_scr[idx:idx + 1, :])
        carry_scr[idx:idx + 1, :] = last
        if emit_final:
            fin_ref[0, 0, idx:idx + 1, :] = last
        if not forward:
            o_ref[...] = ((hf_scr[pl.ds(t0, tc), :] + b_scr[...])
                          * jax.nn.gelu(ya_ref[...])).astype(o_ref.dtype)

    pl.when(fwd)(lambda: scan(True))
    pl.when(jnp.logical_not(fwd))(lambda: scan(False))


def _lru_call(z, *, b, t, conv_w, conv_b, w_lru, b_lru, lam, h0, state_out, layer, name):
    tc = min(t, 512)
    nc = t // tc
    per8 = tc // SUBLANES
    nrow8 = b * t // SUBLANES

    def chunk(s):
        return jnp.where(s < nc, s, 2 * nc - 1 - s)

    def hold(s):
        return jnp.where(s < nc, nc - 1, 2 * nc - 1 - s)

    in_specs = [
        pl.BlockSpec((SUBLANES, LRU_W),
                     lambda bb, s: (jnp.maximum((bb * nc + chunk(s)) * per8 - 1, 0), 0)),
        pl.BlockSpec((tc, LRU_W), lambda bb, s: (bb * nc + chunk(s), 0)),
        pl.BlockSpec((SUBLANES, LRU_W),
                     lambda bb, s: (jnp.minimum((bb * nc + chunk(s) + 1) * per8, nrow8 - 1), 0)),
        pl.BlockSpec((tc, LRU_W), lambda bb, s: (bb * nc + hold(s), 1)),
        pl.BlockSpec((CONV_W, LRU_W), lambda bb, s: (0, 0)),
        pl.BlockSpec((1, LRU_W), lambda bb, s: (0, 0)),
        pl.BlockSpec((1, LRU_W, 2 * LRU_W), lambda bb, s: (s // nc, 0, 0)),
        pl.BlockSpec((1, 1, 2 * LRU_W), lambda bb, s: (s // nc, 0, 0)),
        pl.BlockSpec((1, 1, LRU_W), lambda bb, s: (s // nc, 0, 0)),
    ]
    args = [z, z, z, z, conv_w, conv_b.reshape(1, LRU_W), w_lru, b_lru, lam.reshape(2, 1, LRU_W)]
    if h0 is not None:
        in_specs.append(pl.BlockSpec((1, 1, 2, LRU_W), lambda bb, s: (bb, layer, 0, 0)))
        args.append(h0)
    out_shape = [jax.ShapeDtypeStruct((b * t, LRU_W), BF16)]
    out_specs = [pl.BlockSpec((tc, LRU_W), lambda bb, s: (bb * nc + hold(s), 0))]
    aliases = {}
    if state_out is not None:
        aliases[len(args)] = 1
        in_specs.append(pl.BlockSpec(memory_space=pl.ANY))
        args.append(state_out)
        out_shape.append(jax.ShapeDtypeStruct(state_out.shape, state_out.dtype))
        out_specs.append(pl.BlockSpec((1, 1, 2, LRU_W), lambda bb, s: (bb, layer, 0, 0)))
    res = pl.pallas_call(
        functools.partial(_lru_kernel, tc=tc, nc=nc, has_h0=h0 is not None,
                          emit_final=state_out is not None),
        grid=(b, 2 * nc),
        in_specs=in_specs,
        out_specs=out_specs,
        out_shape=out_shape,
        input_output_aliases=aliases,
        scratch_shapes=[pltpu.VMEM((t, LRU_W), F32), pltpu.VMEM((t, LRU_W), F32),
                        pltpu.VMEM((tc, LRU_W), F32), pltpu.VMEM((tc, LRU_W), F32),
                        pltpu.VMEM((2, LRU_W), F32)],
        compiler_params=_cparams(),
        name=name,
    )(*args)
    return res[0], (res[1] if state_out is not None else None)


def _merge_kernel(x_ref, gt_ref, a_ref, w_ref, g_ref, d_ref, wb_ref, wo_ref, mod_ref, o_ref):
    merged = None
    for bi, br_ref in enumerate((a_ref, w_ref, g_ref, d_ref)):
        gate = gt_ref[:, bi * D_MODEL:(bi + 1) * D_MODEL].astype(F32)
        term = gate * _dot(br_ref[...], wb_ref[0, bi])
        merged = term if merged is None else merged + term
    o_ref[...] = x_ref[...] + mod_ref[0][2:3] * _dot(merged.astype(BF16), wo_ref[0])


def _merge_call(x, gates, branches, w_branch, w_o, layer, mod, latent, name):
    m = x.shape[0]
    tm = 256
    row = lambda i: (i, 0)
    resident = pl.Buffered(1)
    in_specs = [pl.BlockSpec((tm, D_MODEL), row),
                pl.BlockSpec((tm, N_BRANCH * D_MODEL), row)]
    in_specs += [pl.BlockSpec((tm, BRANCH_W), row)] * N_BRANCH
    in_specs += [pl.BlockSpec((1, N_BRANCH, BRANCH_W, D_MODEL), lambda i: (layer, 0, 0, 0),
                              pipeline_mode=resident),
                 pl.BlockSpec((1, D_MODEL, D_MODEL), lambda i: (layer, 0, 0), pipeline_mode=resident),
                 pl.BlockSpec((1, N_MOD, D_MODEL), _mod_index(latent, tm))]
    return pl.pallas_call(
        _merge_kernel,
        grid=(m // tm,),
        in_specs=in_specs,
        out_specs=pl.BlockSpec((tm, D_MODEL), row),
        out_shape=jax.ShapeDtypeStruct((m, D_MODEL), F32),
        compiler_params=_cparams(),
        name=name,
    )(x, gates, *branches, w_branch, w_o, mod)


def _ffn_kernel(x_ref, mod_ref, g_ref, w1_ref, b1_ref, w2_ref, b2_ref, o_ref, h_scr, acc_scr):
    j = pl.program_id(1)

    def chunk(h):
        a = jnp.maximum(_dot(h, w1_ref[0]) + b1_ref[...], 0.0)
        return _dot((a * a).astype(BF16), w2_ref[0])

    @pl.when(j == 0)
    def _():
        for r in range(0, x_ref.shape[0], NORM_PIECE):
            rows = slice(r, r + NORM_PIECE)
            h = _norm_mod(x_ref[rows, :], g_ref[...], mod_ref[0], 3, 4).astype(BF16)
            h_scr[rows, :] = h
            acc_scr[rows, :] = chunk(h)

    @pl.when(j > 0)
    def _():
        acc_scr[...] += chunk(h_scr[...])

    @pl.when(j == pl.num_programs(1) - 1)
    def _():
        o_ref[...] = x_ref[...] + mod_ref[0][5:6] * (acc_scr[...] + b2_ref[...])


def _ffn_call(x, mod, g, w1, b1, w2, b2, layer, latent, name):
    m = x.shape[0]
    tm, tf = 512, 1024
    return pl.pallas_call(
        _ffn_kernel,
        grid=(m // tm, D_FF // tf),
        in_specs=[pl.BlockSpec((tm, D_MODEL), lambda i, j: (i, 0)),
                  pl.BlockSpec((1, N_MOD, D_MODEL), _mod_index(latent, tm)),
                  pl.BlockSpec((1, D_MODEL), lambda i, j: (0, 0)),
                  pl.BlockSpec((1, D_MODEL, tf), lambda i, j: (layer, 0, j)),
                  pl.BlockSpec((1, tf), lambda i, j: (0, j)),
                  pl.BlockSpec((1, tf, D_MODEL), lambda i, j: (layer, j, 0)),
                  pl.BlockSpec((1, D_MODEL), lambda i, j: (0, 0))],
        out_specs=pl.BlockSpec((tm, D_MODEL), lambda i, j: (i, 0)),
        out_shape=jax.ShapeDtypeStruct((m, D_MODEL), F32),
        scratch_shapes=[pltpu.VMEM((tm, D_MODEL), BF16), pltpu.VMEM((tm, D_MODEL), F32)],
        compiler_params=_cparams(),
        name=name,
    )(x, mod, g.reshape(1, D_MODEL), w1, b1.reshape(1, D_FF), w2, b2.reshape(1, D_MODEL))


def _block_diag(w):
    eye = jnp.eye(LRU_BLOCKS, dtype=w.dtype)
    return jnp.einsum('ncd,nm->ncmd', w, eye).reshape(LRU_W, LRU_W)


def _trunk_layer(x, mod, wts, layer, latent, rope, cached, new):
    b, t = (DEC_BATCH, DEC_SEQ) if latent else (BATCH, SEQ)
    tag = ('lat' if latent else 'ctx') + str(layer)
    gates, h = _gates_call(x, mod, wts['norm1_g'], wts['w_gate'], layer, wts['b_gate'], latent,
                           'gates_' + tag)
    z = _inproj_call(h, wts['w_in'], layer, 'inproj_' + tag)
    a_out, new_state = _lru_call(
        z, b=b, t=t, conv_w=wts['conv_w'], conv_b=wts['conv_b'], w_lru=wts['w_lru'], b_lru=wts['b_lru'],
        lam=wts['lru_lambda'], h0=cached[6] if latent else None, state_out=None if latent else new[6],
        layer=layer, name='lru_' + tag)
    w_out, new_win = _gqa_call(
        z, b=b, t=t, tq=256, qname='qw', kname='kw', vname='vw',
        qn=wts['win_qn'], kn=wts['win_kn'], sink=wts['win_sink'], rope=rope,
        cache=(cached[0], cached[1]) if latent else None, layer=layer, banded=latent,
        kv_out=None if latent else new[0:2], gps=1 if latent else WIN_KV, name='win_' + tag)
    g_out, new_grid = _gqa_call(
        z, b=b, t=t, tq=1024 if latent else SEQ, qname='qg', kname='kg', vname='vg',
        qn=wts['grid_qn'], kn=wts['grid_kn'], sink=None, rope=rope,
        cache=(cached[2], cached[3]) if latent else None, layer=layer, banded=False,
        kv_out=None if latent else new[2:4], gps=1 if latent else WIN_KV, name='grid_' + tag)
    d_out, new_diff = _diff_call(
        z, b=b, t=t, tq=1024 if latent else SEQ, qn=wts['diff_qn'], kn=wts['diff_kn'],
        lparams=wts['diff_lp'],
        out_g=wts['diff_out_g'], rope=rope, cache=(cached[4], cached[5]) if latent else None,
        layer=layer, kv_out=None if latent else new[4:6], hps=1 if latent else DIFF_HEADS,
        name='diff_' + tag)
    x = _merge_call(x, gates, (a_out, w_out, g_out, d_out), wts['w_branch'], wts['w_o'], layer, mod,
                    latent, 'merge_' + tag)
    x = _ffn_call(x, mod, wts['norm2_g'], wts['w_ff1'], wts['b_ff1'], wts['w_ff2'], wts['b_ff2'],
                  layer, latent, 'ffn_' + tag)
    if latent:
        return x, None
    return x, new_win + new_grid + new_diff + (new_state,)


def kernel(x_prompt, x_sample, cache_win_k, cache_win_v, cache_grid_k, cache_grid_v, cache_diff_k, cache_diff_v, state_lru, c, c_ctx, w_ada, b_ada, norm1_g, norm2_g, w_in, conv_w, conv_b, lru_wr, lru_br, lru_wi, lru_bi, lru_lambda, win_qn, win_kn, win_sink, grid_qn, grid_kn, diff_qn, diff_kn, diff_lq1, diff_lk1, diff_lq2, diff_lk2, diff_out_g, w_branch, w_gate, b_gate, w_o, w_ff1, b_ff1, w_ff2, b_ff2):
    cond = jnp.zeros((MOD_ROWS, D_MODEL), F32).at[0].set(c_ctx).at[1:1 + DEC_BATCH].set(c)
    mod_all = _modulation(cond, w_ada, b_ada).reshape(DEPTH, MOD_ROWS, N_MOD, D_MODEL)
    rope = _rope_tables(DEC_SEQ)
    cached_all = (cache_win_k.reshape(DEC_BATCH, DEPTH, PAST_LEN, LANES),
                  cache_win_v.reshape(DEC_BATCH, DEPTH, PAST_LEN, LANES),
                  cache_grid_k.reshape(DEC_BATCH, DEPTH, PAST_LEN, LANES),
                  cache_grid_v.reshape(DEC_BATCH, DEPTH, PAST_LEN, LANES),
                  cache_diff_k.reshape(DEC_BATCH, DEPTH, PAST_LEN, 512),
                  cache_diff_v.reshape(DEC_BATCH, DEPTH, PAST_LEN, 512))
    w_in_b, w_gate_b, w_branch_b, w_o_b, w_ff1_b, w_ff2_b = (
        w.astype(BF16) for w in (w_in, w_gate, w_branch, w_o, w_ff1, w_ff2))
    y_p = x_prompt.reshape(BATCH * SEQ, D_MODEL)
    y_s = x_sample.reshape(DEC_BATCH * DEC_SEQ, D_MODEL)
    new = tuple(jnp.zeros((BATCH, DEPTH, SEQ, w), F32) for w in (LANES,) * 4 + (512,) * 2)
    new += (jnp.zeros((BATCH, DEPTH, 2, LRU_W), F32),)
    for l in range(DEPTH):
        w_lru = jnp.stack([
            jnp.concatenate([_block_diag(lru_wr[l, k]), _block_diag(lru_wi[l, k])], axis=1)
            for k in range(2)]).astype(BF16)
        b_lru = jnp.concatenate([lru_br[l], lru_bi[l]], axis=-1).reshape(2, 1, 2 * LRU_W)
        wts = {
            'norm1_g': norm1_g[l], 'norm2_g': norm2_g[l],
            'w_in': w_in_b, 'w_gate': w_gate_b, 'b_gate': b_gate[l],
            'conv_w': conv_w[l], 'conv_b': conv_b[l], 'w_lru': w_lru, 'b_lru': b_lru,
            'lru_lambda': lru_lambda[l],
            'win_qn': win_qn[l], 'win_kn': win_kn[l], 'win_sink': win_sink[l],
            'grid_qn': grid_qn[l], 'grid_kn': grid_kn[l],
            'diff_qn': diff_qn[l], 'diff_kn': diff_kn[l],
            'diff_lp': jnp.stack([diff_lq1[l], diff_lk1[l], diff_lq2[l], diff_lk2[l]]),
            'diff_out_g': diff_out_g[l],
            'w_branch': w_branch_b, 'w_o': w_o_b,
            'w_ff1': w_ff1_b, 'b_ff1': b_ff1[l], 'w_ff2': w_ff2_b, 'b_ff2': b_ff2[l],
        }
        y_p, new = _trunk_layer(y_p, mod_all[l], wts, l, False, None, None, new)
        y_s, _ = _trunk_layer(y_s, mod_all[l], wts, l, True, rope, cached_all + (state_lru,), None)

    kv_shape = (BATCH, DEPTH, SEQ, WIN_KV, HEAD_DIM)
    return (y_p.reshape(BATCH, SEQ, D_MODEL),
            y_s.reshape(DEC_BATCH, DEC_SEQ, D_MODEL),
            new[0].reshape(kv_shape), new[1].reshape(kv_shape),
            new[2].reshape(kv_shape), new[3].reshape(kv_shape),
            new[4].reshape(BATCH, DEPTH, SEQ, DIFF_HEADS, 2, HEAD_DIM),
            new[5].reshape(BATCH, DEPTH, SEQ, DIFF_HEADS, 2 * HEAD_DIM),
            new[6])
```
